```python
import jax, jax.numpy as jnp
from jax import lax
import numpy as np

D_MODEL = 1024
BATCH = 2
SEQ = 16384
DEPTH = 2
DEC_BATCH = 32
DEC_SEQ = 2048
PAST_LEN = 128

HEAD_DIM = 64
A_HEADS = 8
A_WIDTH = A_HEADS * HEAD_DIM
A_PATTERNS = ((128, 1), (512, 4), (2048, 16))
ROPE_THETA = 500000.0
ROPE_DIM = HEAD_DIM // 4
B_HEADS = 4
B_KDIM = 32
B_VDIM = 64
B_QK = B_HEADS * B_KDIM
B_WIDTH = B_HEADS * B_VDIM
B_GATE_RANK = 16
B_GATE_TAU = 16.0
B_CHUNK = 64
C_HEADS = 4
C_DIM = 64
C_WIDTH = C_HEADS * C_DIM
C_CHUNK = 128
RET_THETA = 10000.0
IN_SPLITS = (A_WIDTH, A_WIDTH, A_WIDTH,
             B_QK, B_QK, B_WIDTH, B_WIDTH, 2 * B_GATE_RANK,
             C_WIDTH, C_WIDTH, C_WIDTH, C_WIDTH)
N_IN = 3 * A_WIDTH + 2 * B_QK + 2 * B_WIDTH + 2 * B_GATE_RANK + 4 * C_WIDTH
MIX_WIDTH = A_WIDTH + B_WIDTH + C_WIDTH
D_FF = 4 * D_MODEL
PLE_DIM = 256
EPS = 1e-6
NEG = -1e30

kernel_name = 'hybrid_bidir_dilated_gla_retention_encoder'


def rmsnorm(x, g):
    xf = x.astype(jnp.float32)
    y = xf * lax.rsqrt(jnp.mean(xf * xf, axis=-1, keepdims=True) + EPS)
    return (y * g.astype(jnp.float32)).astype(x.dtype)


def head_rmsnorm(x, g):
    b, s, h, d = x.shape
    return rmsnorm(x, g.reshape(h, d)).reshape(b, s, h * d)


def rotary(x, pos, rot_dim, theta):
    half = rot_dim // 2
    inv_freq = 1.0 / (theta ** (jnp.arange(half, dtype=jnp.float32) * (2.0 / rot_dim)))
    ang = pos[:, None] * inv_freq[None, :]
    cos = jnp.cos(ang)[None, :, None, :]
    sin = jnp.sin(ang)[None, :, None, :]
    x1 = x[..., :half]
    x2 = x[..., half:rot_dim]
    return jnp.concatenate([x1 * cos - x2 * sin, x2 * cos + x1 * sin, x[..., rot_dim:]], axis=-1)


def banded_attention(q, k, v, radius):
    n, L, h, d = q.shape
    w = radius
    nb = -(-L // w)
    lp = nb * w
    qb = jnp.pad(q, ((0, 0), (0, lp - L), (0, 0), (0, 0))).reshape(n, nb, w, h, d)

    def windows(t):
        tb = jnp.pad(t, ((0, 0), (w, lp - L + w), (0, 0), (0, 0))).reshape(n, nb + 2, w, h, d)
        return jnp.concatenate([tb[:, :-2], tb[:, 1:-1], tb[:, 2:]], axis=2)

    kw = windows(k)
    vw = windows(v)
    qpos = jnp.arange(nb)[:, None] * w + jnp.arange(w)[None, :]
    kpos = jnp.arange(nb)[:, None] * w - w + jnp.arange(3 * w)[None, :]
    rel = qpos[:, :, None] - kpos[:, None, :]
    mask = (jnp.abs(rel) <= radius) & (kpos[:, None, :] >= 0) & (kpos[:, None, :] < L)
    s = jnp.einsum('nbqhd,nbkhd->nbhqk', qb, kw)
    s = jnp.where(mask[None, :, None], s, NEG)
    m = jnp.max(s, axis=-1, keepdims=True)
    p = jnp.exp(s - m)
    den = jnp.sum(p, axis=-1, keepdims=True)
    o = jnp.einsum('nbhqk,nbkhd->nbqhd', p, vw) / jnp.swapaxes(den, 2, 3)
    lse = jnp.swapaxes((m + jnp.log(den))[..., 0], 2, 3)
    return o.reshape(n, lp, h, d)[:, :L], lse.reshape(n, lp, h)[:, :L]


def dilated_attention(q, k, v):
    b, s, h, d = q.shape
    outs = []
    lses = []
    for window, dil in A_PATTERNS:
        radius = window // (2 * dil)
        n_sub = s // dil

        def to_res(t):
            return t.reshape(b, n_sub, dil, h, d).transpose(0, 2, 1, 3, 4).reshape(b * dil, n_sub, h, d)

        o, lse = banded_attention(to_res(q), to_res(k), to_res(v), radius)
        outs.append(o.reshape(b, dil, n_sub, h, d).transpose(0, 2, 1, 3, 4).reshape(b, s, h, d))
        lses.append(lse.reshape(b, dil, n_sub, h).transpose(0, 2, 1, 3).reshape(b, s, h))
    wts = jax.nn.softmax(jnp.stack(lses), axis=0)
    return jnp.einsum('gbsh,gbshd->bshd', wts, jnp.stack(outs))


def gla_chunked(q, k, v, log_a, strict):
    bn, s, h, kd = q.shape
    vd = v.shape[-1]
    c = B_CHUNK
    nc = s // c

    def chunks(t):
        return t.reshape(bn, nc, c, h, t.shape[-1]).transpose(1, 0, 3, 2, 4)

    qc, kc, vc = chunks(q), chunks(k), chunks(v)
    bc = jnp.cumsum(chunks(log_a), axis=3)
    idx = jnp.arange(c)
    mask = (idx[None, :] < idx[:, None]) if strict else (idx[None, :] <= idx[:, None])

    def step(state, inp):
        qt, kt, vt, bt = inp
        inter = jnp.einsum('bhtk,bhkv->bhtv', qt * jnp.exp(bt), state)
        diff = bt[:, :, :, None, :] - bt[:, :, None, :, :]
        decay = jnp.exp(jnp.where(mask[:, :, None], diff, NEG))
        att = jnp.einsum('bhtk,bhsk,bhtsk->bhts', qt, kt, decay)
        intra = jnp.einsum('bhts,bhsv->bhtv', att, vt)
        blast = bt[:, :, -1:, :]
        new_state = jnp.exp(blast[:, :, 0, :])[..., None] * state + jnp.einsum(
            'bhsk,bhsv->bhkv', kt * jnp.exp(blast - bt), vt)
        return new_state, inter + intra

    _, o = lax.scan(step, jnp.zeros((bn, h, kd, vd), jnp.float32), (qc, kc, vc, bc))
    return o.transpose(1, 0, 3, 2, 4).reshape(bn, s, h, vd)


def retention_chunked(q, k, v, log_gamma, strict):
    bn, s, h, d = q.shape
    c = C_CHUNK
    nc = s // c
    qc = q.reshape(bn, nc, c, h, d)
    kc = k.reshape(bn, nc, c, h, d)
    vc = v.reshape(bn, nc, c, h, d)
    idx = jnp.arange(c, dtype=jnp.float32)
    rel = idx[:, None] - idx[None, :]
    mask = (rel > 0) if strict else (rel >= 0)
    dmat = jnp.where(mask[None], jnp.exp(jnp.where(mask, rel, 0.0)[None] * log_gamma[:, None, None]), 0.0)
    scores = jnp.einsum('bnthd,bnshd->bnhts', qc, kc) * dmat[None, None]
    intra = jnp.einsum('bnhts,bnshe->bnthe', scores, vc)
    kdec = jnp.exp((c - 1.0 - idx)[None, :] * log_gamma[:, None])
    chunk_kv = jnp.einsum('bnshd,bnshe,hs->nbhde', kc, vc, kdec)
    chunk_decay = jnp.exp(c * log_gamma)[None, :, None, None]

    def step(r, kv):
        return chunk_decay * r + kv, r

    _, r_prev = lax.scan(step, jnp.zeros((bn, h, d, d), jnp.float32), chunk_kv)
    qdec = jnp.exp((idx + 1.0)[None, :] * log_gamma[:, None])
    inter = jnp.einsum('bnthd,nbhde,ht->bnthe', qc, r_prev, qdec)
    return (intra + inter).reshape(bn, s, h, d)


def rev(t):
    return t[:, ::-1]


def trunk_layer(h, ple, ln_mix, w_in, attn_q_norm, attn_k_norm, gla_gate_up, gla_gate_bias,
                gla_out_norm, ret_decay_raw, ret_out_norm, w_out, ln_mlp, w_mlp_in, w_mlp_out,
                ln_pe, w_pe_gate, w_pe_proj):
    bn, s, _ = h.shape
    dt = h.dtype
    f32 = jnp.float32
    u = rmsnorm(h, ln_mix)
    z = (u @ w_in).astype(f32)
    points = []
    acc = 0
    for width in IN_SPLITS[:-1]:
        acc += width
        points.append(acc)
    aq, ak, av, bq, bk, bv, br, bg, cq, ck, cv, cg = jnp.split(z, points, axis=-1)
    pos = jnp.arange(s, dtype=f32)

    aq = rotary(rmsnorm(aq.reshape(bn, s, A_HEADS, HEAD_DIM), attn_q_norm), pos, ROPE_DIM, ROPE_THETA) * (HEAD_DIM ** -0.5)
    ak = rotary(rmsnorm(ak.reshape(bn, s, A_HEADS, HEAD_DIM), attn_k_norm), pos, ROPE_DIM, ROPE_THETA)
    av = av.reshape(bn, s, A_HEADS, HEAD_DIM)
    o_a = dilated_attention(aq, ak, av).reshape(bn, s, A_WIDTH)

    bq = bq.reshape(bn, s, B_HEADS, B_KDIM) * (B_KDIM ** -0.5)
    bk = bk.reshape(bn, s, B_HEADS, B_KDIM)
    bv = bv.reshape(bn, s, B_HEADS, B_VDIM)
    glr = bg.reshape(bn, s, 2, B_GATE_RANK)
    gate_logits = jnp.einsum('bsjr,jrk->jbsk', glr, gla_gate_up.astype(f32)) + gla_gate_bias.astype(f32)[:, None, None, :]
    log_a = (jax.nn.log_sigmoid(gate_logits) / B_GATE_TAU).reshape(2, bn, s, B_HEADS, B_KDIM)
    o_bf = gla_chunked(bq, bk, bv, log_a[0], False)
    o_bb = rev(gla_chunked(rev(bq), rev(bk), rev(bv), rev(log_a[1]), True))
    o_b = head_rmsnorm(o_bf + o_bb, gla_out_norm) * jax.nn.silu(br)

    cq = rotary(cq.reshape(bn, s, C_HEADS, C_DIM), pos, C_DIM, RET_THETA)
    ck = rotary(ck.reshape(bn, s, C_HEADS, C_DIM), pos, C_DIM, RET_THETA) * (C_DIM ** -0.5)
    cv = cv.reshape(bn, s, C_HEADS, C_DIM)
    log_gamma = jax.nn.log_sigmoid(ret_decay_raw.astype(f32))
    o_cf = retention_chunked(cq, ck, cv, log_gamma[0], False)
    o_cb = rev(retention_chunked(rev(cq), rev(ck), rev(cv), log_gamma[1], True))
    o_c = head_rmsnorm(o_cf + o_cb, ret_out_norm) * jax.nn.silu(cg)

    mix = jnp.concatenate([o_a, o_b, o_c], axis=-1).astype(dt)
    h = h + mix @ w_out

    m = rmsnorm(h, ln_mlp)
    h = h + jnp.square(jax.nn.relu(m @ w_mlp_in)) @ w_mlp_out

    gate = jax.nn.sigmoid(rmsnorm(h, ln_pe) @ w_pe_gate)
    return h + gate * (ple @ w_pe_proj)


def run_trunk(x, p, ln_mix, w_in, attn_q_norm, attn_k_norm, gla_gate_up, gla_gate_bias,
              gla_out_norm, ret_decay_raw, ret_out_norm, w_out, ln_mlp, w_mlp_in, w_mlp_out,
              ln_pe, w_pe_gate, w_pe_proj):
    h = x
    for i in range(DEPTH):
        h = trunk_layer(h, p[i], ln_mix[i], w_in[i], attn_q_norm[i], attn_k_norm[i],
                        gla_gate_up[i], gla_gate_bias[i], gla_out_norm[i], ret_decay_raw[i],
                        ret_out_norm[i], w_out[i], ln_mlp[i], w_mlp_in[i], w_mlp_out[i],
                        ln_pe[i], w_pe_gate[i], w_pe_proj[i])
    return h


def setup_inputs(seed: int = 0) -> dict:
    key = jax.random.key(seed)
    ks = jax.random.split(key, 24)
    f32 = jnp.float32

    def nrm(k, shape, scale):
        return jax.random.normal(k, shape, f32) * scale

    ret_base = jnp.log(2.0 ** (5.0 + jnp.arange(C_HEADS, dtype=f32)) - 1.0)
    return {
        'x_prompt': nrm(ks[0], (BATCH, SEQ, D_MODEL), 1.0),
        'x_sample': nrm(ks[1], (DEC_BATCH, DEC_SEQ, D_MODEL), 1.0),
        'p_prompt': nrm(ks[2], (DEPTH, BATCH, SEQ, PLE_DIM), 1.0),
        'p_sample': nrm(ks[3], (DEPTH, DEC_BATCH, DEC_SEQ, PLE_DIM), 1.0),
        'ln_mix': 1.0 + nrm(ks[4], (DEPTH, D_MODEL), 0.02),
        'w_in': nrm(ks[5], (DEPTH, D_MODEL, N_IN), D_MODEL ** -0.5),
        'attn_q_norm': 1.0 + nrm(ks[6], (DEPTH, HEAD_DIM), 0.02),
        'attn_k_norm': 1.0 + nrm(ks[7], (DEPTH, HEAD_DIM), 0.02),
        'gla_gate_up': nrm(ks[8], (DEPTH, 2, B_GATE_RANK, B_QK), B_GATE_RANK ** -0.5),
        'gla_gate_bias': nrm(ks[9], (DEPTH, 2, B_QK), 0.1),
        'gla_out_norm': 1.0 + nrm(ks[10], (DEPTH, B_WIDTH), 0.02),
        'ret_decay_raw': ret_base[None, None, :] + nrm(ks[11], (DEPTH, 2, C_HEADS), 0.1),
        'ret_out_norm': 1.0 + nrm(ks[12], (DEPTH, C_WIDTH), 0.02),
        'w_out': nrm(ks[13], (DEPTH, MIX_WIDTH, D_MODEL), MIX_WIDTH ** -0.5),
        'ln_mlp': 1.0 + nrm(ks[14], (DEPTH, D_MODEL), 0.02),
        'w_mlp_in': nrm(ks[15], (DEPTH, D_MODEL, D_FF), D_MODEL ** -0.5),
        'w_mlp_out': nrm(ks[16], (DEPTH, D_FF, D_MODEL), D_FF ** -0.5),
        'ln_pe': 1.0 + nrm(ks[17], (DEPTH, D_MODEL), 0.02),
        'w_pe_gate': nrm(ks[18], (DEPTH, D_MODEL, D_MODEL), D_MODEL ** -0.5),
        'w_pe_proj': nrm(ks[19], (DEPTH, PLE_DIM, D_MODEL), PLE_DIM ** -0.5),
    }


def reference(x_prompt, x_sample, p_prompt, p_sample, ln_mix, w_in, attn_q_norm, attn_k_norm,
              gla_gate_up, gla_gate_bias, gla_out_norm, ret_decay_raw, ret_out_norm, w_out,
              ln_mlp, w_mlp_in, w_mlp_out, ln_pe, w_pe_gate, w_pe_proj):
    y_prompt = run_trunk(x_prompt, p_prompt, ln_mix, w_in, attn_q_norm, attn_k_norm,
                         gla_gate_up, gla_gate_bias, gla_out_norm, ret_decay_raw, ret_out_norm,
                         w_out, ln_mlp, w_mlp_in, w_mlp_out, ln_pe, w_pe_gate, w_pe_proj)
    y_sample = run_trunk(x_sample, p_sample, ln_mix, w_in, attn_q_norm, attn_k_norm,
                         gla_gate_up, gla_gate_bias, gla_out_norm, ret_decay_raw, ret_out_norm,
                         w_out, ln_mlp, w_mlp_in, w_mlp_out, ln_pe, w_pe_gate, w_pe_proj)
    return (y_prompt, y_sample)
```

```python
import functools

import jax
import jax.numpy as jnp
from jax import lax
from jax.experimental import pallas as pl
from jax.experimental.pallas import tpu as pltpu

F32 = jnp.float32
BF16 = jnp.bfloat16

D_MODEL = 1024
HEAD_DIM = 64
A_HEADS = 8
A_WIDTH = 512
A_PATTERNS = ((128, 1), (512, 4), (2048, 16))
A_RADIUS = 64
A_MAX_REACH = 1024
ROPE_THETA = 500000.0
ROPE_DIM = 16
B_HEADS = 4
B_KDIM = 32
B_QK = 128
B_WIDTH = 256
B_GATE_RANK = 16
B_GATE_TAU = 16.0
C_HEADS = 4
C_DIM = 64
C_WIDTH = 256
RET_THETA = 10000.0
N_IN = 3360
D_FF = 4096
PLE_DIM = 256
EPS = 1e-6
NEG = -1e30

LANES = 128
VMEM_LIMIT = 56 * 1024 * 1024

Z_AQ, Z_AK, Z_AV = 0, 512, 1024
Z_BQ, Z_BK, Z_BV, Z_BR = 1536, 1664, 1792, 2048
Z_CQ, Z_CK, Z_CV, Z_CG = 2304, 2560, 2816, 3072
Z_BG = 3328
Z_WIDTH = 3456

GLA_CHUNK = 64
GLA_SUB = 16
RET_CHUNK = 128
ATT_TILE = 1024
ATT_SUB = 64
ATT_KEYS = 3 * ATT_SUB


def _cparams(sem):
    return pltpu.CompilerParams(dimension_semantics=sem, vmem_limit_bytes=VMEM_LIMIT)


def _const_spec(shape):
    nd = len(shape)
    return pl.BlockSpec(shape, lambda *_: (0,) * nd, pipeline_mode=pl.Buffered(1))


def _sigmoid(x):
    return 1.0 / (1.0 + jnp.exp(-x))


def _log_sigmoid(x):
    return jnp.minimum(x, 0.0) - jnp.log1p(jnp.exp(-jnp.abs(x)))


def _iota(shape, dim):
    return lax.broadcasted_iota(jnp.int32, shape, dim)


def _dot(a, b):
    return jnp.dot(a, b, preferred_element_type=F32)


def _dot_nt(a, b):
    return lax.dot_general(a, b, (((1,), (1,)), ((), ())), preferred_element_type=F32)


def _dot_tn(a, b):
    return lax.dot_general(a, b, (((0,), (0,)), ((), ())), preferred_element_type=F32)


def _head_norm(o, ones_ref, gain):
    ssum = _dot((o * o).astype(BF16), ones_ref[...])
    return o * lax.rsqrt(ssum * (1.0 / HEAD_DIM) + EPS) * gain


def _rope(y, tab_ref, shift):
    c = tab_ref[:, 0:LANES]
    sn = tab_ref[:, LANES:2 * LANES]
    sp = tab_ref[:, 2 * LANES:3 * LANES]
    outs = []
    for j in range(y.shape[1] // LANES):
        yj = y[:, j * LANES:(j + 1) * LANES]
        outs.append(yj * c + pltpu.roll(yj, LANES - shift, 1) * sn + pltpu.roll(yj, shift, 1) * sp)
    return jnp.concatenate(outs, axis=1)


def _proj_in_body(x_ref, g_ref, w_ref, ones_ref, qg_ref, kg_ref, ra_ref, rc_ref, z_ref):
    x = x_ref[...]
    ms = jnp.mean(x * x, axis=-1, keepdims=True)
    u = (x * lax.rsqrt(ms + EPS) * g_ref[...]).astype(BF16)

    def proj(a, b):
        return _dot(u, w_ref[:, a:b])

    def qk_norm(y, gain_ref):
        halves = []
        for j in range(2):
            yj = y[:, 256 * j:256 * (j + 1)]
            halves.append(_head_norm(yj, ones_ref, gain_ref[...]))
        return jnp.concatenate(halves, axis=1)

    aq = _rope(qk_norm(proj(Z_AQ, Z_AK), qg_ref), ra_ref, ROPE_DIM // 2)
    z_ref[:, Z_AQ:Z_AK] = aq * (HEAD_DIM ** -0.5)
    z_ref[:, Z_AK:Z_AV] = _rope(qk_norm(proj(Z_AK, Z_AV), kg_ref), ra_ref, ROPE_DIM // 2)
    z_ref[:, Z_AV:Z_BQ] = proj(Z_AV, Z_BQ)
    z_ref[:, Z_BQ:Z_BK] = proj(Z_BQ, Z_BK) * (B_KDIM ** -0.5)
    z_ref[:, Z_BK:Z_CQ] = proj(Z_BK, Z_CQ)
    z_ref[:, Z_CQ:Z_CK] = _rope(proj(Z_CQ, Z_CK), rc_ref, C_DIM // 2)
    z_ref[:, Z_CK:Z_CV] = _rope(proj(Z_CK, Z_CV), rc_ref, C_DIM // 2) * (C_DIM ** -0.5)
    z_ref[:, Z_CV:Z_WIDTH] = proj(Z_CV, Z_WIDTH)


def _proj_in(x2, seq, ln, w_p, ones256, qg, kg, rope_a, rope_c, tm):
    n = x2.shape[0]
    per_seq = seq // tm
    return pl.pallas_call(
        _proj_in_body,
        grid=(n // tm,),
        in_specs=[
            pl.BlockSpec((tm, D_MODEL), lambda i: (i, 0)),
            _const_spec((1, D_MODEL)),
            _const_spec((D_MODEL, Z_WIDTH)),
            _const_spec((256, 256)),
            _const_spec((1, 256)),
            _const_spec((1, 256)),
            pl.BlockSpec((tm, 3 * LANES), lambda i: (i % per_seq, 0)),
            pl.BlockSpec((tm, 3 * LANES), lambda i: (i % per_seq, 0)),
        ],
        out_specs=pl.BlockSpec((tm, Z_WIDTH), lambda i: (i, 0)),
        out_shape=jax.ShapeDtypeStruct((n, Z_WIDTH), F32),
        compiler_params=_cparams(("parallel",)),
        name="proj_in",
    )(x2, ln, w_p, ones256, qg, kg, rope_a, rope_c)


def _attn_body(seq, q_ref, kp_ref, kc_ref, kn_ref, vp_ref, vc_ref, vn_ref, o_ref,
               kbuf, vbuf, m_ref, l_ref, acc_ref):
    t = ATT_TILE
    halo = A_MAX_REACH
    tile_start = pl.program_id(2) * t
    kbuf[0:halo, :] = kp_ref[...]
    kbuf[halo:halo + t, :] = kc_ref[...]
    kbuf[halo + t:, :] = kn_ref[...]
    vbuf[0:halo, :] = vp_ref[...]
    vbuf[halo:halo + t, :] = vc_ref[...]
    vbuf[halo + t:, :] = vn_ref[...]
    m_ref[...] = jnp.full((t, LANES), NEG, F32)
    l_ref[...] = jnp.zeros((t, LANES), F32)
    acc_ref[...] = jnp.zeros((t, LANES), F32)

    rows2 = 2 * ATT_SUB
    lane_q = _iota((ATT_SUB, LANES), 1)
    low_half = lane_q < HEAD_DIM
    a_idx = _iota((rows2, ATT_KEYS), 0) & (ATT_SUB - 1)
    c_idx = _iota((rows2, ATT_KEYS), 1)
    rel = c_idx - a_idx
    band = (rel >= 0) & (rel <= 2 * A_RADIUS)

    def two_heads(col):
        return jnp.where(low_half, col[0:ATT_SUB], col[ATT_SUB:rows2])

    def sub_block(q_start, k_start, dil, key_pos0):
        q_sub = q_ref[pl.ds(q_start, ATT_SUB, stride=dil), :]
        q_st = jnp.concatenate([jnp.where(low_half, q_sub, 0.0),
                                jnp.where(low_half, 0.0, q_sub)], axis=0).astype(BF16)
        k_sub = kbuf[pl.ds(k_start, ATT_KEYS, stride=dil), :].astype(BF16)
        v_sub = vbuf[pl.ds(k_start, ATT_KEYS, stride=dil), :].astype(BF16)
        s = _dot_nt(q_st, k_sub)
        key_pos = key_pos0 + dil * c_idx
        ok = band & (key_pos >= 0) & (key_pos < seq)
        s = jnp.where(ok, s, NEG)
        m2 = m_ref[pl.ds(q_start, ATT_SUB, stride=dil), :]
        l2 = l_ref[pl.ds(q_start, ATT_SUB, stride=dil), :]
        a2 = acc_ref[pl.ds(q_start, ATT_SUB, stride=dil), :]
        m_old = jnp.concatenate([m2[:, 0:1], m2[:, HEAD_DIM:HEAD_DIM + 1]], axis=0)
        m_new = jnp.maximum(m_old, jnp.max(s, axis=1, keepdims=True))
        p = jnp.exp(s - m_new)
        alpha = two_heads(jnp.exp(m_old - m_new))
        pv = _dot(p.astype(BF16), v_sub)
        pv2 = jnp.where(low_half, pv[0:ATT_SUB], pv[ATT_SUB:rows2])
        m_ref[pl.ds(q_start, ATT_SUB, stride=dil), :] = two_heads(m_new)
        l_ref[pl.ds(q_start, ATT_SUB, stride=dil), :] = alpha * l2 + two_heads(
            jnp.sum(p, axis=1, keepdims=True))
        acc_ref[pl.ds(q_start, ATT_SUB, stride=dil), :] = alpha * a2 + pv2

    for _, dil in A_PATTERNS:
        span = ATT_SUB * dil
        for r in range(dil):
            def step(j, carry, dil=dil, r=r, span=span):
                q_start = r + j * span
                sub_block(q_start, halo + q_start - span, dil, tile_start + q_start - span)
                return carry
            lax.fori_loop(0, t // span, step, 0)

    o_ref[...] = (acc_ref[...] / l_ref[...]).astype(o_ref.dtype)


def _attn(z3):
    b, seq, _ = z3.shape
    t = ATT_TILE
    nt = seq // t
    qc, kc, vc = Z_AQ // LANES, Z_AK // LANES, Z_AV // LANES

    def cur(c0):
        return pl.BlockSpec((None, t, LANES), lambda bi, hp, i: (bi, i, c0 + hp))

    def prev(c0):
        return pl.BlockSpec((None, t, LANES), lambda bi, hp, i: (bi, jnp.maximum(i - 1, 0), c0 + hp))

    def nxt(c0):
        return pl.BlockSpec((None, t, LANES), lambda bi, hp, i: (bi, jnp.minimum(i + 1, nt - 1), c0 + hp))

    return pl.pallas_call(
        functools.partial(_attn_body, seq),
        grid=(b, A_HEADS // 2, nt),
        in_specs=[cur(qc), prev(kc), cur(kc), nxt(kc), prev(vc), cur(vc), nxt(vc)],
        out_specs=pl.BlockSpec((None, t, LANES), lambda bi, hp, i: (bi, i, hp)),
        out_shape=jax.ShapeDtypeStruct((b, seq, A_WIDTH), BF16),
        scratch_shapes=[
            pltpu.VMEM((t + 2 * A_MAX_REACH, LANES), F32),
            pltpu.VMEM((t + 2 * A_MAX_REACH, LANES), F32),
            pltpu.VMEM((t, LANES), F32),
            pltpu.VMEM((t, LANES), F32),
            pltpu.VMEM((t, LANES), F32),
        ],
        compiler_params=_cparams(("parallel", "parallel", "parallel")),
        name="attn",
    )(z3, z3, z3, z3, z3, z3, z3)


def _split3(x):
    hi = x.astype(BF16)
    r1 = x - hi.astype(F32)
    mid = r1.astype(BF16)
    lo = (r1 - mid.astype(F32)).astype(BF16)
    return hi, mid, lo


def _gla_log_decay(g, gup_ref, gbias_ref):
    logits = _dot(g.astype(BF16), gup_ref[...]) + gbias_ref[...]
    return _log_sigmoid(logits) * (1.0 / B_GATE_TAU)


def _cumsum_rows(la, tri3_ref):
    hi, mid, lo = _split3(la)
    return _dot(tri3_ref[...], jnp.concatenate([hi, mid, lo], axis=0))


def _gla_bwd_body(nch, k_ref, v_ref, g_ref, gup_ref, gbias_ref, tri3_ref, emask_ref, sb_ref, state):
    @pl.when(pl.program_id(1) == 0)
    def _():
        state[...] = jnp.zeros_like(state)

    c = GLA_CHUNK
    ones = jnp.ones((2 * c, 2 * LANES), BF16)

    def step(i, carry):
        ci = nch - 1 - i
        rows = pl.ds(pl.multiple_of(ci * c, c), c)
        la = _gla_log_decay(g_ref[rows, :], gup_ref, gbias_ref)[:, B_QK:]
        cbx = _cumsum_rows(la, tri3_ref) - la
        sb_ref[ci] = state[...].astype(BF16)
        kt = (k_ref[rows, :] * jnp.exp(cbx)).astype(BF16)
        upd = _dot_tn(kt, v_ref[rows, :].astype(BF16))
        hi = la.astype(BF16)
        lo = (la - hi.astype(F32)).astype(BF16)
        tot = _dot_tn(jnp.concatenate([hi, lo], axis=0), ones)
        state[...] = jnp.exp(tot) * state[...] + upd * emask_ref[...]
        return carry

    lax.fori_loop(0, nch, step, 0)


def _gla_main_body(nch, q_ref, k_ref, v_ref, r_ref, g_ref, sb_ref, gup_ref, gbias_ref, tri3_ref,
                   emask_ref, ebf_ref, ones_ref, gain_ref, o_ref, state):
    @pl.when(pl.program_id(1) == 0)
    def _():
        state[...] = jnp.zeros_like(state)

    c = GLA_CHUNK
    n = GLA_SUB
    nsub = c // n
    ones = jnp.ones((2 * c, 2 * LANES), BF16)
    row = _iota((c, B_QK), 0)
    row_n = _iota((n, B_QK), 0)
    rho = _iota((B_HEADS * c, B_QK), 0)
    lane_k = _iota((B_HEADS * c, B_QK), 1)
    head_ok = (rho >> 6) == (lane_k >> 5)
    sblk = (rho & (c - 1)) >> 4
    rho_v = _iota((B_HEADS * c, B_WIDTH), 0)
    lane_v = _iota((B_HEADS * c, B_WIDTH), 1)
    vhead_ok = (rho_v >> 6) == (lane_v >> 6)

    def step(ci, carry):
        rows = pl.ds(pl.multiple_of(ci * c, c), c)
        q = q_ref[rows, :]
        k = k_ref[rows, :]
        v = v_ref[rows, :]
        la = _gla_log_decay(g_ref[rows, :], gup_ref, gbias_ref)
        cum = _cumsum_rows(la, tri3_ref)
        bf = cum[:, :B_QK]
        la_b = la[:, B_QK:]
        cbx = cum[:, B_QK:] - la_b
        tot_b = cum[c - 1:c, B_QK:]

        s_f = state[...]
        qf = q * jnp.exp(bf)
        qb = q * jnp.exp(tot_b - cbx)
        lhs = jnp.concatenate([qf, qb], axis=1).astype(BF16)
        rhs = jnp.concatenate([s_f.astype(BF16), sb_ref[ci]], axis=0)
        o = _dot(lhs, rhs)

        e_rows = [bf[n * j + n - 1:n * j + n, :] for j in range(nsub)]
        f_rows = [cbx[n * j:n * j + 1, :] for j in range(nsub)]
        e_blk = jnp.concatenate([jnp.broadcast_to(e, (n, B_QK)) for e in e_rows], axis=0)
        f_blk = jnp.concatenate([jnp.broadcast_to(f, (n, B_QK)) for f in f_rows], axis=0)
        kf = k * jnp.exp(e_blk - bf)
        kb = k * jnp.exp(cbx - f_blk)
        kf4 = jnp.concatenate([kf] * B_HEADS, axis=0)
        kb4 = jnp.concatenate([kb] * B_HEADS, axis=0)
        lhs_parts = []
        rhs_parts = []
        for j in range(nsub - 1):
            qj = jnp.where(row >= n * (j + 1), q * jnp.exp(jnp.minimum(bf - e_rows[j], 0.0)), 0.0)
            lhs_parts.append(qj)
            rhs_parts.append(jnp.where(head_ok & (sblk == j), kf4, 0.0))
        for j in range(1, nsub):
            qj = jnp.where(row < n * j, q * jnp.exp(jnp.minimum(f_rows[j] - cbx, 0.0)), 0.0)
            lhs_parts.append(qj)
            rhs_parts.append(jnp.where(head_ok & (sblk == j), kb4, 0.0))
        att = _dot_nt(jnp.concatenate(lhs_parts, axis=1).astype(BF16),
                      jnp.concatenate(rhs_parts, axis=1).astype(BF16))
        v4 = jnp.where(vhead_ok, jnp.concatenate([v] * B_HEADS, axis=0), 0.0).astype(BF16)
        o = o + _dot(att.astype(BF16), v4)

        diag = []
        for i in range(nsub):
            lo_, hi_ = n * i, n * (i + 1)
            qi, ki, vi = q[lo_:hi_], k[lo_:hi_], v[lo_:hi_]
            bfi, cbi = bf[lo_:hi_], cbx[lo_:hi_]
            slabs = []
            for s in range(n):
                arg = jnp.where(row_n >= s, bfi - bfi[s:s + 1], cbi[s:s + 1] - cbi)
                slabs.append(((qi * ki[s:s + 1]) * jnp.exp(jnp.minimum(arg, 0.0))).astype(BF16))
            zz = _dot(jnp.concatenate(slabs, axis=0), ebf_ref[...])
            acc = zz[0:n] * vi[0:1]
            for s in range(1, n):
                acc = acc + zz[n * s:n * (s + 1)] * vi[s:s + 1]
            diag.append(acc)
        o = o + jnp.concatenate(diag, axis=0)

        last = bf[c - 1:c, :]
        kt = (k * jnp.exp(last - bf)).astype(BF16)
        upd = _dot_tn(kt, v.astype(BF16))
        la_f = la[:, :B_QK]
        hi = la_f.astype(BF16)
        lo = (la_f - hi.astype(F32)).astype(BF16)
        tot = _dot_tn(jnp.concatenate([hi, lo], axis=0), ones)
        state[...] = jnp.exp(tot) * s_f + upd * emask_ref[...]

        gate = r_ref[rows, :]
        out = _head_norm(o, ones_ref, gain_ref[...]) * (gate * _sigmoid(gate))
        o_ref[rows, :] = out.astype(o_ref.dtype)
        return carry

    lax.fori_loop(0, nch, step, 0)


def _gla(z3, gup_bd, gbias, tri3, emask, ebf, ones256, gain, tb):
    b, seq, _ = z3.shape
    nblk = seq // tb
    nch = tb // GLA_CHUNK
    ntot = seq // GLA_CHUNK

    def zspec(col, width, rev):
        blk = col // width
        if rev:
            return pl.BlockSpec((None, tb, width), lambda bi, i: (bi, nblk - 1 - i, blk))
        return pl.BlockSpec((None, tb, width), lambda bi, i: (bi, i, blk))

    consts = [gup_bd, gbias, tri3, emask]
    const_specs = [_const_spec(a.shape) for a in consts]
    sb = pl.pallas_call(
        functools.partial(_gla_bwd_body, nch),
        grid=(b, nblk),
        in_specs=[zspec(Z_BK, B_QK, True), zspec(Z_BV, B_WIDTH, True), zspec(Z_BG, LANES, True)]
        + const_specs,
        out_specs=pl.BlockSpec((None, nch, B_QK, B_WIDTH), lambda bi, i: (bi, nblk - 1 - i, 0, 0)),
        out_shape=jax.ShapeDtypeStruct((b, ntot, B_QK, B_WIDTH), BF16),
        scratch_shapes=[pltpu.VMEM((B_QK, B_WIDTH), F32)],
        compiler_params=_cparams(("parallel", "arbitrary")),
        name="gla_bwd",
    )(z3, z3, z3, *consts)

    consts2 = [gup_bd, gbias, tri3, emask, ebf, ones256, gain]
    return pl.pallas_call(
        functools.partial(_gla_main_body, nch),
        grid=(b, nblk),
        in_specs=[zspec(Z_BQ, B_QK, False), zspec(Z_BK, B_QK, False), zspec(Z_BV, B_WIDTH, False),
                  zspec(Z_BR, B_WIDTH, False), zspec(Z_BG, LANES, False),
                  pl.BlockSpec((None, nch, B_QK, B_WIDTH), lambda bi, i: (bi, i, 0, 0))]
        + [_const_spec(a.shape) for a in consts2],
        out_specs=pl.BlockSpec((None, tb, B_WIDTH), lambda bi, i: (bi, i, 0)),
        out_shape=jax.ShapeDtypeStruct((b, seq, B_WIDTH), BF16),
        scratch_shapes=[pltpu.VMEM((B_QK, B_WIDTH), F32)],
        compiler_params=_cparams(("parallel", "arbitrary")),
        name="gla_main",
    )(z3, z3, z3, z3, z3, sb, *consts2)


def _ret_bwd_body(nch, k_ref, v_ref, raw_ref, bmask_ref, rb_ref, state):
    @pl.when(pl.program_id(1) == 0)
    def _():
        state[...] = jnp.zeros_like(state)

    c = RET_CHUNK
    lg1 = _log_sigmoid(raw_ref[1:2, :])
    pos = _iota((c, C_WIDTH), 0).astype(F32)
    kdec = jnp.exp(pos * lg1)
    chunk_decay = jnp.exp(float(c) * lg1)

    def step(i, carry):
        ci = nch - 1 - i
        rows = pl.ds(pl.multiple_of(ci * c, c), c)
        rb_ref[ci] = state[...].astype(BF16)
        kt = (k_ref[rows, :] * kdec).astype(BF16)
        upd = _dot_tn(kt, v_ref[rows, :].astype(BF16))
        state[...] = chunk_decay * state[...] + upd * bmask_ref[...]
        return carry

    lax.fori_loop(0, nch, step, 0)


def _ret_main_body(nch, q_ref, k_ref, v_ref, g_ref, rb_ref, raw_ref, raw_s_ref, bmask_ref, ones_ref,
                   gain_ref, o_ref, state):
    @pl.when(pl.program_id(1) == 0)
    def _():
        state[...] = jnp.zeros_like(state)

    c = RET_CHUNK
    lg = _log_sigmoid(raw_ref[...])
    lg0, lg1 = lg[0:1, :], lg[1:2, :]
    lgs = _log_sigmoid(raw_s_ref[...])
    pos = _iota((c, C_WIDTH), 0).astype(F32)
    qdec_f = jnp.exp((pos + 1.0) * lg0)
    qdec_b = jnp.exp((float(c) - pos) * lg1)
    kdec_f = jnp.exp((float(c) - 1.0 - pos) * lg0)
    chunk_decay = jnp.exp(float(c) * lg0)
    t_idx = _iota((c, C_HEADS * c), 0)
    s_idx = _iota((c, C_HEADS * c), 1) & (c - 1)
    rel = (t_idx - s_idx).astype(F32)
    dmat = jnp.where(rel >= 0.0, jnp.exp(jnp.maximum(rel, 0.0) * lgs[0:1, :]),
                     jnp.exp(jnp.maximum(-rel, 0.0) * lgs[1:2, :]))
    rho = _iota((C_HEADS * c, C_WIDTH), 0)
    lane = _iota((C_HEADS * c, C_WIDTH), 1)
    head_ok = (rho >> 7) == (lane >> 6)

    def step(ci, carry):
        rows = pl.ds(pl.multiple_of(ci * c, c), c)
        q = q_ref[rows, :]
        k = k_ref[rows, :]
        v = v_ref[rows, :]
        k4 = jnp.where(head_ok, jnp.concatenate([k] * C_HEADS, axis=0), 0.0).astype(BF16)
        v4 = jnp.where(head_ok, jnp.concatenate([v] * C_HEADS, axis=0), 0.0).astype(BF16)
        scores = _dot_nt(q.astype(BF16), k4) * dmat
        o = _dot(scores.astype(BF16), v4)
        r_f = state[...]
        lhs = jnp.concatenate([q * qdec_f, q * qdec_b], axis=1).astype(BF16)
        rhs = jnp.concatenate([r_f.astype(BF16), rb_ref[ci]], axis=0)
        o = o + _dot(lhs, rhs)
        upd = _dot_tn((k * kdec_f).astype(BF16), v.astype(BF16))
        state[...] = chunk_decay * r_f + upd * bmask_ref[...]
        gate = g_ref[rows, :]
        out = _head_norm(o, ones_ref, gain_ref[...]) * (gate * _sigmoid(gate))
        o_ref[rows, :] = out.astype(o_ref.dtype)
        return carry

    lax.fori_loop(0, nch, step, 0)


def _ret(z3, raw256, raw512, ones256, gain, tb):
    b, seq, _ = z3.shape
    nblk = seq // tb
    nch = tb // RET_CHUNK
    ntot = seq // RET_CHUNK

    def zspec(col, rev):
        blk = col // C_WIDTH
        if rev:
            return pl.BlockSpec((None, tb, C_WIDTH), lambda bi, i: (bi, nblk - 1 - i, blk))
        return pl.BlockSpec((None, tb, C_WIDTH), lambda bi, i: (bi, i, blk))

    rb = pl.pallas_call(
        functools.partial(_ret_bwd_body, nch),
        grid=(b, nblk),
        in_specs=[zspec(Z_CK, True), zspec(Z_CV, True), _const_spec(raw256.shape),
                  _const_spec(ones256.shape)],
        out_specs=pl.BlockSpec((None, nch, C_WIDTH, C_WIDTH), lambda bi, i: (bi, nblk - 1 - i, 0, 0)),
        out_shape=jax.ShapeDtypeStruct((b, ntot, C_WIDTH, C_WIDTH), BF16),
        scratch_shapes=[pltpu.VMEM((C_WIDTH, C_WIDTH), F32)],
        compiler_params=_cparams(("parallel", "arbitrary")),
        name="ret_bwd",
    )(z3, z3, raw256, ones256)

    consts = [raw256, raw512, ones256, ones256, gain]
    return pl.pallas_call(
        functools.partial(_ret_main_body, nch),
        grid=(b, nblk),
        in_specs=[zspec(Z_CQ, False), zspec(Z_CK, False), zspec(Z_CV, False), zspec(Z_CG, False),
                  pl.BlockSpec((None, nch, C_WIDTH, C_WIDTH), lambda bi, i: (bi, i, 0, 0))]
        + [_const_spec(a.shape) for a in consts],
        out_specs=pl.BlockSpec((None, tb, C_WIDTH), lambda bi, i: (bi, i, 0)),
        out_shape=jax.ShapeDtypeStruct((b, seq, C_WIDTH), BF16),
        scratch_shapes=[pltpu.VMEM((C_WIDTH, C_WIDTH), F32)],
        compiler_params=_cparams(("parallel", "arbitrary")),
        name="ret_main",
    )(z3, z3, z3, z3, rb, *consts)


def _post_body(h_ref, oa_ref, ob_ref, oc_ref, p_ref, wo_ref, lm_ref, w1_ref, w2_ref, lp_ref,
               wg_ref, wp_ref, y_ref):
    def rms(x, g_ref):
        ms = jnp.mean(x * x, axis=-1, keepdims=True)
        return (x * lax.rsqrt(ms + EPS) * g_ref[...]).astype(BF16)

    h = h_ref[...]
    h = h + (_dot(oa_ref[...], wo_ref[0:A_WIDTH, :])
             + _dot(ob_ref[...], wo_ref[A_WIDTH:A_WIDTH + B_WIDTH, :])
             + _dot(oc_ref[...], wo_ref[A_WIDTH + B_WIDTH:, :]))
    m = rms(h, lm_ref)
    ff = D_FF // 4
    mlp = None
    for j in range(4):
        hid = _dot(m, w1_ref[:, ff * j:ff * (j + 1)])
        hid = jnp.square(jnp.maximum(hid, 0.0)).astype(BF16)
        part = _dot(hid, w2_ref[ff * j:ff * (j + 1), :])
        mlp = part if mlp is None else mlp + part
    h = h + mlp
    gate = _sigmoid(_dot(rms(h, lp_ref), wg_ref[...]))
    y_ref[...] = h + gate * _dot(p_ref[...].astype(BF16), wp_ref[...])


def _post(h2, oa, ob, oc, ple, wo, lm, w1, w2, lp, wg, wp, tm):
    n = h2.shape[0]

    def tok(width):
        return pl.BlockSpec((tm, width), lambda i: (i, 0))

    consts = [wo, lm, w1, w2, lp, wg, wp]
    return pl.pallas_call(
        _post_body,
        grid=(n // tm,),
        in_specs=[tok(D_MODEL), tok(A_WIDTH), tok(B_WIDTH), tok(C_WIDTH), tok(PLE_DIM)]
        + [_const_spec(a.shape) for a in consts],
        out_specs=tok(D_MODEL),
        out_shape=jax.ShapeDtypeStruct((n, D_MODEL), F32),
        compiler_params=_cparams(("parallel",)),
        name="post",
    )(h2, oa, ob, oc, ple, *consts)


def _rope_tables(seq, rot_dim, theta):
    half = rot_dim // 2
    inv_freq = 1.0 / (theta ** (jnp.arange(half, dtype=F32) * (2.0 / rot_dim)))
    ang = jnp.arange(seq, dtype=F32)[:, None] * inv_freq[None, :]
    cos, sin = jnp.cos(ang), jnp.sin(ang)
    pad = HEAD_DIM - rot_dim
    c = jnp.concatenate([cos, cos, jnp.ones((seq, pad), F32)], axis=1)
    sn = jnp.concatenate([-sin, jnp.zeros((seq, half + pad), F32)], axis=1)
    sp = jnp.concatenate([jnp.zeros((seq, half), F32), sin, jnp.zeros((seq, pad), F32)], axis=1)
    return jnp.concatenate([jnp.tile(t, (1, LANES // HEAD_DIM)) for t in (c, sn, sp)], axis=1)


def _block_ones(rows, cols, rblk, cblk):
    r = jnp.arange(rows)[:, None] // rblk
    c = jnp.arange(cols)[None, :] // cblk
    return r == c


def _layer_consts(i, ln_mix, w_in, attn_q_norm, attn_k_norm, gla_gate_up, gla_gate_bias, gla_out_norm,
                  ret_decay_raw, ret_out_norm, w_out, ln_mlp, w_mlp_in, w_mlp_out, ln_pe, w_pe_gate,
                  w_pe_proj):
    w = w_in[i]
    w_p = jnp.concatenate([w[:, :2304], w[:, 2336:N_IN], w[:, 2304:2336],
                           jnp.zeros((D_MODEL, Z_WIDTH - N_IN), F32)], axis=1).astype(BF16)
    gup = gla_gate_up[i].astype(BF16)
    gup_bd = jnp.zeros((LANES, 2 * B_QK), BF16)
    gup_bd = gup_bd.at[0:B_GATE_RANK, 0:B_QK].set(gup[0])
    gup_bd = gup_bd.at[B_GATE_RANK:2 * B_GATE_RANK, B_QK:].set(gup[1])
    return dict(
        ln_mix=ln_mix[i][None, :], w_p=w_p,
        qg=jnp.tile(attn_q_norm[i], 4)[None, :], kg=jnp.tile(attn_k_norm[i], 4)[None, :],
        gup_bd=gup_bd, gbias=gla_gate_bias[i].reshape(1, 2 * B_QK),
        gla_gain=gla_out_norm[i][None, :],
        raw256=jnp.repeat(ret_decay_raw[i], C_DIM, axis=1),
        raw512=jnp.repeat(ret_decay_raw[i], RET_CHUNK, axis=1),
        ret_gain=ret_out_norm[i][None, :],
        wo=w_out[i].astype(BF16), lm=ln_mlp[i][None, :], w1=w_mlp_in[i].astype(BF16),
        w2=w_mlp_out[i].astype(BF16), lp=ln_pe[i][None, :], wg=w_pe_gate[i].astype(BF16),
        wp=w_pe_proj[i].astype(BF16),
    )


def _run_trunk(x, p, layers, shared):
    b, seq, _ = x.shape
    n = b * seq
    tm = 512
    tb = 512
    rope_a = _rope_tables(seq, ROPE_DIM, ROPE_THETA)
    rope_c = _rope_tables(seq, C_DIM, RET_THETA)
    h = x.reshape(n, D_MODEL)
    for i, lc in enumerate(layers):
        z = _proj_in(h, seq, lc["ln_mix"], lc["w_p"], shared["ones256"], lc["qg"], lc["kg"],
                     rope_a, rope_c, tm)
        z3 = z.reshape(b, seq, Z_WIDTH)
        oa = _attn(z3)
        ob = _gla(z3, lc["gup_bd"], lc["gbias"], shared["tri3"], shared["emask"], shared["ebf"],
                  shared["ones256"], lc["gla_gain"], tb)
        oc = _ret(z3, lc["raw256"], lc["raw512"], shared["ones256"], lc["ret_gain"], tb)
        h = _post(h, oa.reshape(n, A_WIDTH), ob.reshape(n, B_WIDTH), oc.reshape(n, C_WIDTH),
                  p[i].reshape(n, PLE_DIM), lc["wo"], lc["lm"], lc["w1"], lc["w2"], lc["lp"],
                  lc["wg"], lc["wp"], tm)
    return h.reshape(b, seq, D_MODEL)


def kernel(x_prompt, x_sample, p_prompt, p_sample, ln_mix, w_in, attn_q_norm, attn_k_norm, gla_gate_up, gla_gate_bias, gla_out_norm, ret_decay_raw, ret_out_norm, w_out, ln_mlp, w_mlp_in, w_mlp_out, ln_pe, w_pe_gate, w_pe_proj):
    depth = w_in.shape[0]
    layers = [_layer_consts(i, ln_mix, w_in, attn_q_norm, attn_k_norm, gla_gate_up, gla_gate_bias,
                            gla_out_norm, ret_decay_raw, ret_out_norm, w_out, ln_mlp, w_mlp_in,
                            w_mlp_out, ln_pe, w_pe_gate, w_pe_proj) for i in range(depth)]
    tri = jnp.arange(GLA_CHUNK)[:, None] >= jnp.arange(GLA_CHUNK)[None, :]
    emask = _block_ones(B_QK, B_WIDTH, B_KDIM, HEAD_DIM)
    shared = dict(
        ones256=_block_ones(256, 256, HEAD_DIM, HEAD_DIM).astype(BF16),
        tri3=jnp.tile(tri, (1, 3)).astype(BF16),
        emask=emask.astype(F32),
        ebf=emask.astype(BF16),
    )
    y_prompt = _run_trunk(x_prompt, p_prompt, layers, shared)
    y_sample = _run_trunk(x_sample, p_sample, layers, shared)
    return (y_prompt, y_sample)
```

```python
import functools

import jax
import jax.numpy as jnp
from jax import lax
from jax.experimental import pallas as pl
from jax.experimental.pallas import tpu as pltpu

F32 = jnp.float32
BF16 = jnp.bfloat16

D_MODEL = 1024
HEAD_DIM = 64
A_HEADS = 8
A_WIDTH = 512
A_PATTERNS = ((128, 1), (512, 4), (2048, 16))
A_RADIUS = 64
A_MAX_REACH = 1024
ROPE_THETA = 500000.0
ROPE_DIM = 16
B_HEADS = 4
B_KDIM = 32
B_QK = 128
B_WIDTH = 256
B_GATE_RANK = 16
B_GATE_TAU = 16.0
C_HEADS = 4
C_DIM = 64
C_WIDTH = 256
RET_THETA = 10000.0
N_IN = 3360
D_FF = 4096
PLE_DIM = 256
EPS = 1e-6
NEG = -1e30

LANES = 128
VMEM_LIMIT = 56 * 1024 * 1024

Z_AQ, Z_AK, Z_AV = 0, 512, 1024
Z_BQ, Z_BK, Z_BV, Z_BR = 1536, 1664, 1792, 2048
Z_CQ, Z_CK, Z_CV, Z_CG = 2304, 2560, 2816, 3072
Z_BG = 3328
Z_WIDTH = 3456

GLA_CHUNK = 64
GLA_SUB = 16
RET_CHUNK = 128
ATT_TILE = 2048
ATT_SUB = 128
ATT_KEYS = ATT_SUB + 2 * A_RADIUS


def _cparams(sem):
    return pltpu.CompilerParams(dimension_semantics=sem, vmem_limit_bytes=VMEM_LIMIT)


def _const_spec(shape):
    nd = len(shape)
    return pl.BlockSpec(shape, lambda *_: (0,) * nd, pipeline_mode=pl.Buffered(1))


def _sigmoid(x):
    return 1.0 / (1.0 + jnp.exp(-x))


def _log_sigmoid(x):
    return jnp.minimum(x, 0.0) - jnp.log1p(jnp.exp(-jnp.abs(x)))


def _iota(shape, dim):
    return lax.broadcasted_iota(jnp.int32, shape, dim)


def _dot(a, b):
    return jnp.dot(a, b, preferred_element_type=F32)


def _dot_nt(a, b):
    return lax.dot_general(a, b, (((1,), (1,)), ((), ())), preferred_element_type=F32)


def _dot_tn(a, b):
    return lax.dot_general(a, b, (((0,), (0,)), ((), ())), preferred_element_type=F32)


def _head_norm(o, ones_ref, gain):
    ssum = _dot((o * o).astype(BF16), ones_ref[...])
    return o * lax.rsqrt(ssum * (1.0 / HEAD_DIM) + EPS) * gain


def _rope(y, tab_ref, shift):
    c = tab_ref[:, 0:LANES]
    sn = tab_ref[:, LANES:2 * LANES]
    sp = tab_ref[:, 2 * LANES:3 * LANES]
    outs = []
    for j in range(y.shape[1] // LANES):
        yj = y[:, j * LANES:(j + 1) * LANES]
        outs.append(yj * c + pltpu.roll(yj, LANES - shift, 1) * sn + pltpu.roll(yj, shift, 1) * sp)
    return jnp.concatenate(outs, axis=1)


def _proj_in_body(x_ref, g_ref, w_ref, ones_ref, qg_ref, kg_ref, ra_ref, rc_ref, z_ref):
    x = x_ref[...]
    ms = jnp.mean(x * x, axis=-1, keepdims=True)
    u = (x * lax.rsqrt(ms + EPS) * g_ref[...]).astype(BF16)

    def proj(a, b):
        return _dot(u, w_ref[:, a:b])

    def qk_norm(y, gain_ref):
        halves = []
        for j in range(2):
            yj = y[:, 256 * j:256 * (j + 1)]
            halves.append(_head_norm(yj, ones_ref, gain_ref[...]))
        return jnp.concatenate(halves, axis=1)

    aq = _rope(qk_norm(proj(Z_AQ, Z_AK), qg_ref), ra_ref, ROPE_DIM // 2)
    z_ref[:, Z_AQ:Z_AK] = aq * (HEAD_DIM ** -0.5)
    z_ref[:, Z_AK:Z_AV] = _rope(qk_norm(proj(Z_AK, Z_AV), kg_ref), ra_ref, ROPE_DIM // 2)
    z_ref[:, Z_AV:Z_BQ] = proj(Z_AV, Z_BQ)
    z_ref[:, Z_BQ:Z_BK] = proj(Z_BQ, Z_BK) * (B_KDIM ** -0.5)
    z_ref[:, Z_BK:Z_CQ] = proj(Z_BK, Z_CQ)
    z_ref[:, Z_CQ:Z_CK] = _rope(proj(Z_CQ, Z_CK), rc_ref, C_DIM // 2)
    z_ref[:, Z_CK:Z_CV] = _rope(proj(Z_CK, Z_CV), rc_ref, C_DIM // 2) * (C_DIM ** -0.5)
    z_ref[:, Z_CV:Z_WIDTH] = proj(Z_CV, Z_WIDTH)


def _proj_in(x2, seq, ln, w_p, ones256, qg, kg, rope_a, rope_c, tm):
    n = x2.shape[0]
    per_seq = seq // tm
    return pl.pallas_call(
        _proj_in_body,
        grid=(n // tm,),
        in_specs=[
            pl.BlockSpec((tm, D_MODEL), lambda i: (i, 0)),
            _const_spec((1, D_MODEL)),
            _const_spec((D_MODEL, Z_WIDTH)),
            _const_spec((256, 256)),
            _const_spec((1, 256)),
            _const_spec((1, 256)),
            pl.BlockSpec((tm, 3 * LANES), lambda i: (i % per_seq, 0)),
            pl.BlockSpec((tm, 3 * LANES), lambda i: (i % per_seq, 0)),
        ],
        out_specs=pl.BlockSpec((tm, Z_WIDTH), lambda i: (i, 0)),
        out_shape=jax.ShapeDtypeStruct((n, Z_WIDTH), F32),
        compiler_params=_cparams(("parallel",)),
        name="proj_in",
    )(x2, ln, w_p, ones256, qg, kg, rope_a, rope_c)


def _attn_body(seq, q_ref, kp_ref, kc_ref, kn_ref, vp_ref, vc_ref, vn_ref, o_ref,
               kbuf, vbuf, m_ref, l_ref, acc_ref):
    t = ATT_TILE
    halo = A_MAX_REACH
    sub = ATT_SUB
    tile_start = pl.program_id(2) * t
    kbuf[0:halo, :] = kp_ref[...]
    kbuf[halo:halo + t, :] = kc_ref[...]
    kbuf[halo + t:, :] = kn_ref[...]
    vbuf[0:halo, :] = vp_ref[...]
    vbuf[halo:halo + t, :] = vc_ref[...]
    vbuf[halo + t:, :] = vn_ref[...]
    m_ref[...] = jnp.full((2, t, LANES), NEG, F32)
    l_ref[...] = jnp.zeros((2, t, LANES), F32)
    acc_ref[...] = jnp.zeros((t, LANES), F32)

    low_half = _iota((sub, LANES), 1) < HEAD_DIM
    a_idx = _iota((2 * sub, ATT_KEYS), 0) & (sub - 1)
    rel = _iota((2 * sub, ATT_KEYS), 1) - a_idx
    band = (rel >= 0) & (rel <= 2 * A_RADIUS)
    c_row = _iota((1, ATT_KEYS), 1)
    ones_v = jnp.ones((ATT_KEYS, LANES), BF16)

    def sub_blocks(q_starts, dil):
        span = A_RADIUS * dil
        rows = [pl.ds(qs, sub, stride=dil) for qs in q_starts]
        keys = [pl.ds(halo + qs - span, ATT_KEYS, stride=dil) for qs in q_starts]
        scores = []
        for qs, rw, ky in zip(q_starts, rows, keys):
            q_sub = q_ref[rw, :]
            q_st = jnp.concatenate([jnp.where(low_half, q_sub, 0.0),
                                    jnp.where(low_half, 0.0, q_sub)], axis=0).astype(BF16)
            s = _dot_nt(q_st, kbuf[ky, :].astype(BF16))
            key_pos = (tile_start + qs - span) + dil * c_row
            bias = jnp.where((key_pos >= 0) & (key_pos < seq), 0.0, NEG)
            scores.append(jnp.where(band, s + bias, NEG))
        m_old = [jnp.concatenate([m_ref[0, rw, :], m_ref[1, rw, :]], axis=0) for rw in rows]
        m_new = [jnp.maximum(mo, jnp.max(s, axis=1, keepdims=True)) for mo, s in zip(m_old, scores)]
        outs = []
        for s, mn, ky in zip(scores, m_new, keys):
            p = jnp.exp(s - jnp.concatenate([mn] * (ATT_KEYS // LANES), axis=1)).astype(BF16)
            vo = jnp.concatenate([vbuf[ky, :].astype(BF16), ones_v], axis=1)
            outs.append(_dot(p, vo))
        for rw, mo, mn, pvl in zip(rows, m_old, m_new, outs):
            alpha = jnp.exp(mo - mn)
            l_old = jnp.concatenate([l_ref[0, rw, :], l_ref[1, rw, :]], axis=0)
            l_new = alpha * l_old + pvl[:, LANES:]
            m_ref[0, rw, :] = mn[0:sub]
            m_ref[1, rw, :] = mn[sub:]
            l_ref[0, rw, :] = l_new[0:sub]
            l_ref[1, rw, :] = l_new[sub:]
            alpha2 = jnp.where(low_half, alpha[0:sub], alpha[sub:])
            pv2 = jnp.where(low_half, pvl[0:sub, 0:LANES], pvl[sub:, 0:LANES])
            acc_ref[rw, :] = alpha2 * acc_ref[rw, :] + pv2

    group = 2
    for _, dil in A_PATTERNS:
        per_res = t // (sub * dil)
        if per_res >= group:
            for r in range(dil):
                def step(j, carry, dil=dil, r=r):
                    base = r + j * (group * sub * dil)
                    sub_blocks([base + g * sub * dil for g in range(group)], dil)
                    return carry
                lax.fori_loop(0, per_res // group, step, 0)
        else:
            for r in range(0, dil, group):
                sub_blocks([r + g for g in range(group)], dil)

    l2 = jnp.where(_iota((t, LANES), 1) < HEAD_DIM, l_ref[0], l_ref[1])
    o_ref[...] = (acc_ref[...] / l2).astype(o_ref.dtype)


def _attn(z3):
    b, seq, _ = z3.shape
    t = ATT_TILE
    halo = A_MAX_REACH
    nt = seq // t
    per_tile = t // halo
    n_halo = seq // halo
    qc, kc, vc = Z_AQ // LANES, Z_AK // LANES, Z_AV // LANES

    def cur(c0):
        return pl.BlockSpec((None, t, LANES), lambda bi, hp, i: (bi, i, c0 + hp))

    def prev(c0):
        return pl.BlockSpec((None, halo, LANES),
                            lambda bi, hp, i: (bi, jnp.maximum(i * per_tile - 1, 0), c0 + hp))

    def nxt(c0):
        return pl.BlockSpec((None, halo, LANES),
                            lambda bi, hp, i: (bi, jnp.minimum((i + 1) * per_tile, n_halo - 1), c0 + hp))

    return pl.pallas_call(
        functools.partial(_attn_body, seq),
        grid=(b, A_HEADS // 2, nt),
        in_specs=[cur(qc), prev(kc), cur(kc), nxt(kc), prev(vc), cur(vc), nxt(vc)],
        out_specs=pl.BlockSpec((None, t, LANES), lambda bi, hp, i: (bi, i, hp)),
        out_shape=jax.ShapeDtypeStruct((b, seq, A_WIDTH), BF16),
        scratch_shapes=[
            pltpu.VMEM((t + 2 * halo, LANES), F32),
            pltpu.VMEM((t + 2 * halo, LANES), F32),
            pltpu.VMEM((2, t, LANES), F32),
            pltpu.VMEM((2, t, LANES), F32),
            pltpu.VMEM((t, LANES), F32),
        ],
        compiler_params=_cparams(("parallel", "parallel", "parallel")),
        name="attn",
    )(z3, z3, z3, z3, z3, z3, z3)


def _split3(x):
    hi = x.astype(BF16)
    r1 = x - hi.astype(F32)
    mid = r1.astype(BF16)
    lo = (r1 - mid.astype(F32)).astype(BF16)
    return hi, mid, lo


def _gla_log_decay(g, gup_ref, gbias_ref):
    logits = _dot(g.astype(BF16), gup_ref[...]) + gbias_ref[...]
    return _log_sigmoid(logits) * (1.0 / B_GATE_TAU)


def _cumsum_rows(la, tri3_ref):
    hi, mid, lo = _split3(la)
    return _dot(tri3_ref[...], jnp.concatenate([hi, mid, lo], axis=0))


def _gla_bwd_body(nch, k_ref, v_ref, g_ref, gup_ref, gbias_ref, tri3_ref, emask_ref, sb_ref, state):
    @pl.when(pl.program_id(1) == 0)
    def _():
        state[...] = jnp.zeros_like(state)

    c = GLA_CHUNK
    ones = jnp.ones((2 * c, 2 * LANES), BF16)

    def step(i, carry):
        ci = nch - 1 - i
        rows = pl.ds(pl.multiple_of(ci * c, c), c)
        la = _gla_log_decay(g_ref[rows, :], gup_ref, gbias_ref)[:, B_QK:]
        cbx = _cumsum_rows(la, tri3_ref) - la
        sb_ref[ci] = state[...].astype(BF16)
        kt = (k_ref[rows, :] * jnp.exp(cbx)).astype(BF16)
        upd = _dot_tn(kt, v_ref[rows, :].astype(BF16))
        hi = la.astype(BF16)
        lo = (la - hi.astype(F32)).astype(BF16)
        tot = _dot_tn(jnp.concatenate([hi, lo], axis=0), ones)
        state[...] = jnp.exp(tot) * state[...] + upd * emask_ref[...]
        return carry

    lax.fori_loop(0, nch, step, 0)


def _gla_main_body(nch, q_ref, k_ref, v_ref, r_ref, g_ref, sb_ref, gup_ref, gbias_ref, tri3_ref,
                   emask_ref, ebf_ref, ones_ref, gain_ref, o_ref, state):
    @pl.when(pl.program_id(1) == 0)
    def _():
        state[...] = jnp.zeros_like(state)

    c = GLA_CHUNK
    n = GLA_SUB
    nsub = c // n
    ones = jnp.ones((2 * c, 2 * LANES), BF16)
    row = _iota((c, B_QK), 0)
    row_n = _iota((n, B_QK), 0)
    rho = _iota((B_HEADS * c, B_QK), 0)
    lane_k = _iota((B_HEADS * c, B_QK), 1)
    head_ok = (rho >> 6) == (lane_k >> 5)
    sblk = (rho & (c - 1)) >> 4
    rho_v = _iota((B_HEADS * c, B_WIDTH), 0)
    lane_v = _iota((B_HEADS * c, B_WIDTH), 1)
    vhead_ok = (rho_v >> 6) == (lane_v >> 6)

    def step(ci, carry):
        rows = pl.ds(pl.multiple_of(ci * c, c), c)
        q = q_ref[rows, :]
        k = k_ref[rows, :]
        v = v_ref[rows, :]
        la = _gla_log_decay(g_ref[rows, :], gup_ref, gbias_ref)
        cum = _cumsum_rows(la, tri3_ref)
        bf = cum[:, :B_QK]
        la_b = la[:, B_QK:]
        cbx = cum[:, B_QK:] - la_b
        tot_b = cum[c - 1:c, B_QK:]

        s_f = state[...]
        qf = q * jnp.exp(bf)
        qb = q * jnp.exp(tot_b - cbx)
        lhs = jnp.concatenate([qf, qb], axis=1).astype(BF16)
        rhs = jnp.concatenate([s_f.astype(BF16), sb_ref[ci]], axis=0)
        o = _dot(lhs, rhs)

        e_rows = [bf[n * j + n - 1:n * j + n, :] for j in range(nsub)]
        f_rows = [cbx[n * j:n * j + 1, :] for j in range(nsub)]
        e_blk = jnp.concatenate([jnp.broadcast_to(e, (n, B_QK)) for e in e_rows], axis=0)
        f_blk = jnp.concatenate([jnp.broadcast_to(f, (n, B_QK)) for f in f_rows], axis=0)
        kf = k * jnp.exp(e_blk - bf)
        kb = k * jnp.exp(cbx - f_blk)
        kf4 = jnp.concatenate([kf] * B_HEADS, axis=0)
        kb4 = jnp.concatenate([kb] * B_HEADS, axis=0)
        lhs_parts = []
        rhs_parts = []
        for j in range(nsub - 1):
            qj = jnp.where(row >= n * (j + 1), q * jnp.exp(jnp.minimum(bf - e_rows[j], 0.0)), 0.0)
            lhs_parts.append(qj)
            rhs_parts.append(jnp.where(head_ok & (sblk == j), kf4, 0.0))
        for j in range(1, nsub):
            qj = jnp.where(row < n * j, q * jnp.exp(jnp.minimum(f_rows[j] - cbx, 0.0)), 0.0)
            lhs_parts.append(qj)
            rhs_parts.append(jnp.where(head_ok & (sblk == j), kb4, 0.0))
        att = _dot_nt(jnp.concatenate(lhs_parts, axis=1).astype(BF16),
                      jnp.concatenate(rhs_parts, axis=1).astype(BF16))
        v4 = jnp.where(vhead_ok, jnp.concatenate([v] * B_HEADS, axis=0), 0.0).astype(BF16)
        o = o + _dot(att.astype(BF16), v4)

        diag = []
        for i in range(nsub):
            lo_, hi_ = n * i, n * (i + 1)
            qi, ki, vi = q[lo_:hi_], k[lo_:hi_], v[lo_:hi_]
            bfi, cbi = bf[lo_:hi_], cbx[lo_:hi_]
            slabs = []
            for s in range(n):
                arg = jnp.where(row_n >= s, bfi - bfi[s:s + 1], cbi[s:s + 1] - cbi)
                slabs.append(((qi * ki[s:s + 1]) * jnp.exp(jnp.minimum(arg, 0.0))).astype(BF16))
            zz = _dot(jnp.concatenate(slabs, axis=0), ebf_ref[...])
            acc = zz[0:n] * vi[0:1]
            for s in range(1, n):
                acc = acc + zz[n * s:n * (s + 1)] * vi[s:s + 1]
            diag.append(acc)
        o = o + jnp.concatenate(diag, axis=0)

        last = bf[c - 1:c, :]
        kt = (k * jnp.exp(last - bf)).astype(BF16)
        upd = _dot_tn(kt, v.astype(BF16))
        la_f = la[:, :B_QK]
        hi = la_f.astype(BF16)
        lo = (la_f - hi.astype(F32)).astype(BF16)
        tot = _dot_tn(jnp.concatenate([hi, lo], axis=0), ones)
        state[...] = jnp.exp(tot) * s_f + upd * emask_ref[...]

        gate = r_ref[rows, :]
        out = _head_norm(o, ones_ref, gain_ref[...]) * (gate * _sigmoid(gate))
        o_ref[rows, :] = out.astype(o_ref.dtype)
        return carry

    lax.fori_loop(0, nch, step, 0)


def _gla(z3, gup_bd, gbias, tri3, emask, ebf, ones256, gain, tb):
    b, seq, _ = z3.shape
    nblk = seq // tb
    nch = tb // GLA_CHUNK
    ntot = seq // GLA_CHUNK

    def zspec(col, width, rev):
        blk = col // width
        if rev:
            return pl.BlockSpec((None, tb, width), lambda bi, i: (bi, nblk - 1 - i, blk))
        return pl.BlockSpec((None, tb, width), lambda bi, i: (bi, i, blk))

    consts = [gup_bd, gbias, tri3, emask]
    const_specs = [_const_spec(a.shape) for a in consts]
    sb = pl.pallas_call(
        functools.partial(_gla_bwd_body, nch),
        grid=(b, nblk),
        in_specs=[zspec(Z_BK, B_QK, True), zspec(Z_BV, B_WIDTH, True), zspec(Z_BG, LANES, True)]
        + const_specs,
        out_specs=pl.BlockSpec((None, nch, B_QK, B_WIDTH), lambda bi, i: (bi, nblk - 1 - i, 0, 0)),
        out_shape=jax.ShapeDtypeStruct((b, ntot, B_QK, B_WIDTH), BF16),
        scratch_shapes=[pltpu.VMEM((B_QK, B_WIDTH), F32)],
        compiler_params=_cparams(("parallel", "arbitrary")),
        name="gla_bwd",
    )(z3, z3, z3, *consts)

    consts2 = [gup_bd, gbias, tri3, emask, ebf, ones256, gain]
    return pl.pallas_call(
        functools.partial(_gla_main_body, nch),
        grid=(b, nblk),
        in_specs=[zspec(Z_BQ, B_QK, False), zspec(Z_BK, B_QK, False), zspec(Z_BV, B_WIDTH, False),
                  zspec(Z_BR, B_WIDTH, False), zspec(Z_BG, LANES, False),
                  pl.BlockSpec((None, nch, B_QK, B_WIDTH), lambda bi, i: (bi, i, 0, 0))]
        + [_const_spec(a.shape) for a in consts2],
        out_specs=pl.BlockSpec((None, tb, B_WIDTH), lambda bi, i: (bi, i, 0)),
        out_shape=jax.ShapeDtypeStruct((b, seq, B_WIDTH), BF16),
        scratch_shapes=[pltpu.VMEM((B_QK, B_WIDTH), F32)],
        compiler_params=_cparams(("parallel", "arbitrary")),
        name="gla_main",
    )(z3, z3, z3, z3, z3, sb, *consts2)


def _ret_bwd_body(nch, k_ref, v_ref, raw_ref, bmask_ref, rb_ref, state):
    @pl.when(pl.program_id(1) == 0)
    def _():
        state[...] = jnp.zeros_like(state)

    c = RET_CHUNK
    lg1 = _log_sigmoid(raw_ref[1:2, :])
    pos = _iota((c, C_WIDTH), 0).astype(F32)
    kdec = jnp.exp(pos * lg1)
    chunk_decay = jnp.exp(float(c) * lg1)

    def step(i, carry):
        ci = nch - 1 - i
        rows = pl.ds(pl.multiple_of(ci * c, c), c)
        rb_ref[ci] = state[...].astype(BF16)
        kt = (k_ref[rows, :] * kdec).astype(BF16)
        upd = _dot_tn(kt, v_ref[rows, :].astype(BF16))
        state[...] = chunk_decay * state[...] + upd * bmask_ref[...]
        return carry

    lax.fori_loop(0, nch, step, 0)


def _ret_main_body(nch, q_ref, k_ref, v_ref, g_ref, rb_ref, raw_ref, raw_s_ref, bmask_ref, ones_ref,
                   gain_ref, o_ref, state):
    @pl.when(pl.program_id(1) == 0)
    def _():
        state[...] = jnp.zeros_like(state)

    c = RET_CHUNK
    lg = _log_sigmoid(raw_ref[...])
    lg0, lg1 = lg[0:1, :], lg[1:2, :]
    lgs = _log_sigmoid(raw_s_ref[...])
    pos = _iota((c, C_WIDTH), 0).astype(F32)
    qdec_f = jnp.exp((pos + 1.0) * lg0)
    qdec_b = jnp.exp((float(c) - pos) * lg1)
    kdec_f = jnp.exp((float(c) - 1.0 - pos) * lg0)
    chunk_decay = jnp.exp(float(c) * lg0)
    t_idx = _iota((c, C_HEADS * c), 0)
    s_idx = _iota((c, C_HEADS * c), 1) & (c - 1)
    rel = (t_idx - s_idx).astype(F32)
    dmat = jnp.where(rel >= 0.0, jnp.exp(jnp.maximum(rel, 0.0) * lgs[0:1, :]),
                     jnp.exp(jnp.maximum(-rel, 0.0) * lgs[1:2, :]))
    rho = _iota((C_HEADS * c, C_WIDTH), 0)
    lane = _iota((C_HEADS * c, C_WIDTH), 1)
    head_ok = (rho >> 7) == (lane >> 6)

    def step(ci, carry):
        rows = pl.ds(pl.multiple_of(ci * c, c), c)
        q = q_ref[rows, :]
        k = k_ref[rows, :]
        v = v_ref[rows, :]
        k4 = jnp.where(head_ok, jnp.concatenate([k] * C_HEADS, axis=0), 0.0).astype(BF16)
        v4 = jnp.where(head_ok, jnp.concatenate([v] * C_HEADS, axis=0), 0.0).astype(BF16)
        scores = _dot_nt(q.astype(BF16), k4) * dmat
        o = _dot(scores.astype(BF16), v4)
        r_f = state[...]
        lhs = jnp.concatenate([q * qdec_f, q * qdec_b], axis=1).astype(BF16)
        rhs = jnp.concatenate([r_f.astype(BF16), rb_ref[ci]], axis=0)
        o = o + _dot(lhs, rhs)
        upd = _dot_tn((k * kdec_f).astype(BF16), v.astype(BF16))
        state[...] = chunk_decay * r_f + upd * bmask_ref[...]
        gate = g_ref[rows, :]
        out = _head_norm(o, ones_ref, gain_ref[...]) * (gate * _sigmoid(gate))
        o_ref[rows, :] = out.astype(o_ref.dtype)
        return carry

    lax.fori_loop(0, nch, step, 0)


def _ret(z3, raw256, raw512, ones256, gain, tb):
    b, seq, _ = z3.shape
    nblk = seq // tb
    nch = tb // RET_CHUNK
    ntot = seq // RET_CHUNK

    def zspec(col, rev):
        blk = col // C_WIDTH
        if rev:
            return pl.BlockSpec((None, tb, C_WIDTH), lambda bi, i: (bi, nblk - 1 - i, blk))
        return pl.BlockSpec((None, tb, C_WIDTH), lambda bi, i: (bi, i, blk))

    rb = pl.pallas_call(
        functools.partial(_ret_bwd_body, nch),
        grid=(b, nblk),
        in_specs=[zspec(Z_CK, True), zspec(Z_CV, True), _const_spec(raw256.shape),
                  _const_spec(ones256.shape)],
        out_specs=pl.BlockSpec((None, nch, C_WIDTH, C_WIDTH), lambda bi, i: (bi, nblk - 1 - i, 0, 0)),
        out_shape=jax.ShapeDtypeStruct((b, ntot, C_WIDTH, C_WIDTH), BF16),
        scratch_shapes=[pltpu.VMEM((C_WIDTH, C_WIDTH), F32)],
        compiler_params=_cparams(("parallel", "arbitrary")),
        name="ret_bwd",
    )(z3, z3, raw256, ones256)

    consts = [raw256, raw512, ones256, ones256, gain]
    return pl.pallas_call(
        functools.partial(_ret_main_body, nch),
        grid=(b, nblk),
        in_specs=[zspec(Z_CQ, False), zspec(Z_CK, False), zspec(Z_CV, False), zspec(Z_CG, False),
                  pl.BlockSpec((None, nch, C_WIDTH, C_WIDTH), lambda bi, i: (bi, i, 0, 0))]
        + [_const_spec(a.shape) for a in consts],
        out_specs=pl.BlockSpec((None, tb, C_WIDTH), lambda bi, i: (bi, i, 0)),
        out_shape=jax.ShapeDtypeStruct((b, seq, C_WIDTH), BF16),
        scratch_shapes=[pltpu.VMEM((C_WIDTH, C_WIDTH), F32)],
        compiler_params=_cparams(("parallel", "arbitrary")),
        name="ret_main",
    )(z3, z3, z3, z3, rb, *consts)


def _post_body(h_ref, oa_ref, ob_ref, oc_ref, p_ref, wo_ref, lm_ref, w1_ref, w2_ref, lp_ref,
               wg_ref, wp_ref, y_ref):
    def rms(x, g_ref):
        ms = jnp.mean(x * x, axis=-1, keepdims=True)
        return (x * lax.rsqrt(ms + EPS) * g_ref[...]).astype(BF16)

    h = h_ref[...]
    h = h + (_dot(oa_ref[...], wo_ref[0:A_WIDTH, :])
             + _dot(ob_ref[...], wo_ref[A_WIDTH:A_WIDTH + B_WIDTH, :])
             + _dot(oc_ref[...], wo_ref[A_WIDTH + B_WIDTH:, :]))
    m = rms(h, lm_ref)
    ff = D_FF // 4
    mlp = None
    for j in range(4):
        hid = _dot(m, w1_ref[:, ff * j:ff * (j + 1)])
        hid = jnp.square(jnp.maximum(hid, 0.0)).astype(BF16)
        part = _dot(hid, w2_ref[ff * j:ff * (j + 1), :])
        mlp = part if mlp is None else mlp + part
    h = h + mlp
    gate = _sigmoid(_dot(rms(h, lp_ref), wg_ref[...]))
    y_ref[...] = h + gate * _dot(p_ref[...].astype(BF16), wp_ref[...])


def _post(h2, oa, ob, oc, ple, wo, lm, w1, w2, lp, wg, wp, tm):
    n = h2.shape[0]

    def tok(width):
        return pl.BlockSpec((tm, width), lambda i: (i, 0))

    consts = [wo, lm, w1, w2, lp, wg, wp]
    return pl.pallas_call(
        _post_body,
        grid=(n // tm,),
        in_specs=[tok(D_MODEL), tok(A_WIDTH), tok(B_WIDTH), tok(C_WIDTH), tok(PLE_DIM)]
        + [_const_spec(a.shape) for a in consts],
        out_specs=tok(D_MODEL),
        out_shape=jax.ShapeDtypeStruct((n, D_MODEL), F32),
        compiler_params=_cparams(("parallel",)),
        name="post",
    )(h2, oa, ob, oc, ple, *consts)


def _rope_tables(seq, rot_dim, theta):
    half = rot_dim // 2
    inv_freq = 1.0 / (theta ** (jnp.arange(half, dtype=F32) * (2.0 / rot_dim)))
    ang = jnp.arange(seq, dtype=F32)[:, None] * inv_freq[None, :]
    cos, sin = jnp.cos(ang), jnp.sin(ang)
    pad = HEAD_DIM - rot_dim
    c = jnp.concatenate([cos, cos, jnp.ones((seq, pad), F32)], axis=1)
    sn = jnp.concatenate([-sin, jnp.zeros((seq, half + pad), F32)], axis=1)
    sp = jnp.concatenate([jnp.zeros((seq, half), F32), sin, jnp.zeros((seq, pad), F32)], axis=1)
    return jnp.concatenate([jnp.tile(t, (1, LANES // HEAD_DIM)) for t in (c, sn, sp)], axis=1)


def _block_ones(rows, cols, rblk, cblk):
    r = jnp.arange(rows)[:, None] // rblk
    c = jnp.arange(cols)[None, :] // cblk
    return r == c


def _layer_consts(i, ln_mix, w_in, attn_q_norm, attn_k_norm, gla_gate_up, gla_gate_bias, gla_out_norm,
                  ret_decay_raw, ret_out_norm, w_out, ln_mlp, w_mlp_in, w_mlp_out, ln_pe, w_pe_gate,
                  w_pe_proj):
    w = w_in[i]
    w_p = jnp.concatenate([w[:, :2304], w[:, 2336:N_IN], w[:, 2304:2336],
                           jnp.zeros((D_MODEL, Z_WIDTH - N_IN), F32)], axis=1).astype(BF16)
    gup = gla_gate_up[i].astype(BF16)
    gup_bd = jnp.zeros((LANES, 2 * B_QK), BF16)
    gup_bd = gup_bd.at[0:B_GATE_RANK, 0:B_QK].set(gup[0])
    gup_bd = gup_bd.at[B_GATE_RANK:2 * B_GATE_RANK, B_QK:].set(gup[1])
    return dict(
        ln_mix=ln_mix[i][None, :], w_p=w_p,
        qg=jnp.tile(attn_q_norm[i], 4)[None, :], kg=jnp.tile(attn_k_norm[i], 4)[None, :],
        gup_bd=gup_bd, gbias=gla_gate_bias[i].reshape(1, 2 * B_QK),
        gla_gain=gla_out_norm[i][None, :],
        raw256=jnp.repeat(ret_decay_raw[i], C_DIM, axis=1),
        raw512=jnp.repeat(ret_decay_raw[i], RET_CHUNK, axis=1),
        ret_gain=ret_out_norm[i][None, :],
        wo=w_out[i].astype(BF16), lm=ln_mlp[i][None, :], w1=w_mlp_in[i].astype(BF16),
        w2=w_mlp_out[i].astype(BF16), lp=ln_pe[i][None, :], wg=w_pe_gate[i].astype(BF16),
        wp=w_pe_proj[i].astype(BF16),
    )


def _run_trunk(x, p, layers, shared):
    b, seq, _ = x.shape
    n = b * seq
    tm = 512
    tb = 512
    rope_a = _rope_tables(seq, ROPE_DIM, ROPE_THETA)
    rope_c = _rope_tables(seq, C_DIM, RET_THETA)
    h = x.reshape(n, D_MODEL)
    for i, lc in enumerate(layers):
        z = _proj_in(h, seq, lc["ln_mix"], lc["w_p"], shared["ones256"], lc["qg"], lc["kg"],
                     rope_a, rope_c, tm)
        z3 = z.reshape(b, seq, Z_WIDTH)
        oa = _attn(z3)
        ob = _gla(z3, lc["gup_bd"], lc["gbias"], shared["tri3"], shared["emask"], shared["ebf"],
                  shared["ones256"], lc["gla_gain"], tb)
        oc = _ret(z3, lc["raw256"], lc["raw512"], shared["ones256"], lc["ret_gain"], tb)
        h = _post(h, oa.reshape(n, A_WIDTH), ob.reshape(n, B_WIDTH), oc.reshape(n, C_WIDTH),
                  p[i].reshape(n, PLE_DIM), lc["wo"], lc["lm"], lc["w1"], lc["w2"], lc["lp"],
                  lc["wg"], lc["wp"], tm)
    return h.reshape(b, seq, D_MODEL)


def kernel(x_prompt, x_sample, p_prompt, p_sample, ln_mix, w_in, attn_q_norm, attn_k_norm, gla_gate_up, gla_gate_bias, gla_out_norm, ret_decay_raw, ret_out_norm, w_out, ln_mlp, w_mlp_in, w_mlp_out, ln_pe, w_pe_gate, w_pe_proj):
    depth = w_in.shape[0]
    layers = [_layer_consts(i, ln_mix, w_in, attn_q_norm, attn_k_norm, gla_gate_up, gla_gate_bias,
                            gla_out_norm, ret_decay_raw, ret_out_norm, w_out, ln_mlp, w_mlp_in,
                            w_mlp_out, ln_pe, w_pe_gate, w_pe_proj) for i in range(depth)]
    tri = jnp.arange(GLA_CHUNK)[:, None] >= jnp.arange(GLA_CHUNK)[None, :]
    emask = _block_ones(B_QK, B_WIDTH, B_KDIM, HEAD_DIM)
    shared = dict(
        ones256=_block_ones(256, 256, HEAD_DIM, HEAD_DIM).astype(BF16),
        tri3=jnp.tile(tri, (1, 3)).astype(BF16),
        emask=emask.astype(F32),
        ebf=emask.astype(BF16),
    )
    y_prompt = _run_trunk(x_prompt, p_prompt, layers, shared)
    y_sample = _run_trunk(x_sample, p_sample, layers, shared)
    return (y_prompt, y_sample)
```

```python
import functools

import jax
import jax.numpy as jnp
from jax import lax
from jax.experimental import pallas as pl
from jax.experimental.pallas import tpu as pltpu

F32 = jnp.float32
BF16 = jnp.bfloat16

D_MODEL = 1024
HEAD_DIM = 64
A_HEADS = 8
A_WIDTH = 512
A_PATTERNS = ((128, 1), (512, 4), (2048, 16))
A_RADIUS = 64
A_MAX_REACH = 1024
ROPE_THETA = 500000.0
ROPE_DIM = 16
B_HEADS = 4
B_KDIM = 32
B_QK = 128
B_WIDTH = 256
B_GATE_RANK = 16
B_GATE_TAU = 16.0
C_HEADS = 4
C_DIM = 64
C_WIDTH = 256
RET_THETA = 10000.0
N_IN = 3360
D_FF = 4096
PLE_DIM = 256
EPS = 1e-6
NEG = -1e30
LOG2E = 1.4426950408889634

LANES = 128
VMEM_LIMIT = 56 * 1024 * 1024

Z_AQ, Z_AK, Z_AV = 0, 512, 1024
Z_BQ, Z_BK, Z_BV, Z_BR = 1536, 1664, 1792, 2048
Z_CQ, Z_CK, Z_CV, Z_CG = 2304, 2560, 2816, 3072
Z_BG = 3328
Z_WIDTH = 3456

GLA_CHUNK = 64
GLA_SUB = 16
RET_CHUNK = 128
ATT_TILE = 2048
ATT_SUB = 128
ATT_KEYS = ATT_SUB + 2 * A_RADIUS
ATT_SKEW = 2


def _cparams(sem):
    return pltpu.CompilerParams(dimension_semantics=sem, vmem_limit_bytes=VMEM_LIMIT)


def _const_spec(shape):
    nd = len(shape)
    return pl.BlockSpec(shape, lambda *_: (0,) * nd, pipeline_mode=pl.Buffered(1))


def _sigmoid(x):
    return 1.0 / (1.0 + jnp.exp(-x))


def _log_sigmoid(x):
    return jnp.minimum(x, 0.0) - jnp.log1p(jnp.exp(-jnp.abs(x)))


def _iota(shape, dim):
    return lax.broadcasted_iota(jnp.int32, shape, dim)


def _dot(a, b):
    return jnp.dot(a, b, preferred_element_type=F32)


def _dot_nt(a, b):
    return lax.dot_general(a, b, (((1,), (1,)), ((), ())), preferred_element_type=F32)


def _dot_tn(a, b):
    return lax.dot_general(a, b, (((0,), (0,)), ((), ())), preferred_element_type=F32)


def _head_norm(o, ones_ref, gain):
    ssum = _dot((o * o).astype(BF16), ones_ref[...])
    return o * lax.rsqrt(ssum * (1.0 / HEAD_DIM) + EPS) * gain


def _rope(y, tab_ref, shift):
    c = tab_ref[:, 0:LANES]
    sn = tab_ref[:, LANES:2 * LANES]
    sp = tab_ref[:, 2 * LANES:3 * LANES]
    outs = []
    for j in range(y.shape[1] // LANES):
        yj = y[:, j * LANES:(j + 1) * LANES]
        outs.append(yj * c + pltpu.roll(yj, LANES - shift, 1) * sn + pltpu.roll(yj, shift, 1) * sp)
    return jnp.concatenate(outs, axis=1)


def _proj_in_body(x_ref, g_ref, w_ref, ones_ref, qg_ref, kg_ref, ra_ref, rc_ref, z_ref):
    x = x_ref[...]
    ms = jnp.mean(x * x, axis=-1, keepdims=True)
    u = (x * lax.rsqrt(ms + EPS) * g_ref[...]).astype(BF16)

    def proj(a, b):
        return _dot(u, w_ref[:, a:b])

    def qk_norm(y, gain_ref):
        halves = []
        for j in range(2):
            yj = y[:, 256 * j:256 * (j + 1)]
            halves.append(_head_norm(yj, ones_ref, gain_ref[...]))
        return jnp.concatenate(halves, axis=1)

    aq = _rope(qk_norm(proj(Z_AQ, Z_AK), qg_ref), ra_ref, ROPE_DIM // 2)
    z_ref[:, Z_AQ:Z_AK] = aq * (HEAD_DIM ** -0.5 * LOG2E)
    z_ref[:, Z_AK:Z_AV] = _rope(qk_norm(proj(Z_AK, Z_AV), kg_ref), ra_ref, ROPE_DIM // 2)
    z_ref[:, Z_AV:Z_BQ] = proj(Z_AV, Z_BQ)
    z_ref[:, Z_BQ:Z_BK] = proj(Z_BQ, Z_BK) * (B_KDIM ** -0.5)
    z_ref[:, Z_BK:Z_CQ] = proj(Z_BK, Z_CQ)
    z_ref[:, Z_CQ:Z_CK] = _rope(proj(Z_CQ, Z_CK), rc_ref, C_DIM // 2)
    z_ref[:, Z_CK:Z_CV] = _rope(proj(Z_CK, Z_CV), rc_ref, C_DIM // 2) * (C_DIM ** -0.5)
    z_ref[:, Z_CV:Z_WIDTH] = proj(Z_CV, Z_WIDTH)


def _proj_in(x2, seq, ln, w_p, ones256, qg, kg, rope_a, rope_c, tm):
    n = x2.shape[0]
    per_seq = seq // tm
    return pl.pallas_call(
        _proj_in_body,
        grid=(n // tm,),
        in_specs=[
            pl.BlockSpec((tm, D_MODEL), lambda i: (i, 0)),
            _const_spec((1, D_MODEL)),
            _const_spec((D_MODEL, Z_WIDTH)),
            _const_spec((256, 256)),
            _const_spec((1, 256)),
            _const_spec((1, 256)),
            pl.BlockSpec((tm, 3 * LANES), lambda i: (i % per_seq, 0)),
            pl.BlockSpec((tm, 3 * LANES), lambda i: (i % per_seq, 0)),
        ],
        out_specs=pl.BlockSpec((tm, Z_WIDTH), lambda i: (i, 0)),
        out_shape=jax.ShapeDtypeStruct((n, Z_WIDTH), F32),
        compiler_params=_cparams(("parallel",)),
        name="proj_in",
    )(x2, ln, w_p, ones256, qg, kg, rope_a, rope_c)


def _attn_body(seq, q_ref, kp_ref, kc_ref, kn_ref, vp_ref, vc_ref, vn_ref, o_ref,
               kbuf, vbuf, m_ref, l_ref, acc_ref, band_ref):
    t = ATT_TILE
    halo = A_MAX_REACH
    sub = ATT_SUB
    tile_start = pl.program_id(2) * t
    kbuf[0:halo, :] = kp_ref[...]
    kbuf[halo:halo + t, :] = kc_ref[...]
    kbuf[halo + t:, :] = kn_ref[...]
    vbuf[0:halo, :] = vp_ref[...]
    vbuf[halo:halo + t, :] = vc_ref[...]
    vbuf[halo + t:, :] = vn_ref[...]

    low_half = _iota((sub, LANES), 1) < HEAD_DIM
    a_idx = _iota((2 * sub, ATT_KEYS), 0) & (sub - 1)
    rel = _iota((2 * sub, ATT_KEYS), 1) - a_idx
    band_ref[...] = jnp.where((rel >= 0) & (rel <= 2 * A_RADIUS), 0.0, NEG)
    c_row = _iota((1, ATT_KEYS), 1)
    ones_v = jnp.ones((ATT_KEYS, LANES), BF16)

    def scores_stage(tile):
        pat, qs, dil = tile
        span = A_RADIUS * dil
        q_sub = q_ref[pl.ds(qs, sub, stride=dil), :]
        q_st = jnp.concatenate([jnp.where(low_half, q_sub, 0.0),
                                jnp.where(low_half, 0.0, q_sub)], axis=0).astype(BF16)
        k_sub = kbuf[pl.ds(halo + qs - span, ATT_KEYS, stride=dil), :].astype(BF16)
        s = _dot_nt(q_st, k_sub)
        key_pos = (tile_start + qs - span) + dil * c_row
        bias = jnp.where((key_pos >= 0) & (key_pos < seq), 0.0, NEG)
        s = (s + band_ref[...]) + bias
        return s, jnp.broadcast_to(jnp.max(s, axis=1, keepdims=True), (2 * sub, LANES))

    def values_stage(tile, s, m_row):
        pat, qs, dil = tile
        p = jnp.exp2(s - jnp.concatenate([m_row] * (ATT_KEYS // LANES), axis=1)).astype(BF16)
        v_sub = vbuf[pl.ds(halo + qs - A_RADIUS * dil, ATT_KEYS, stride=dil), :].astype(BF16)
        return _dot(p, jnp.concatenate([v_sub, ones_v], axis=1))

    def store_stage(tile, m_row, pvl):
        pat, qs, dil = tile
        rw = pl.ds(qs, sub, stride=dil)
        m_ref[pat, 0, rw, :] = m_row[0:sub]
        m_ref[pat, 1, rw, :] = m_row[sub:]
        l_ref[pat, 0, rw, :] = pvl[0:sub, LANES:]
        l_ref[pat, 1, rw, :] = pvl[sub:, LANES:]
        acc_ref[pat, rw, :] = jnp.where(low_half, pvl[0:sub, 0:LANES], pvl[sub:, 0:LANES])

    tiles = [(pat, r + j * sub * dil, dil)
             for pat, (_, dil) in enumerate(A_PATTERNS)
             for r in range(dil) for j in range(t // (sub * dil))]
    scored, valued = {}, {}
    for step in range(len(tiles) + 2 * ATT_SKEW):
        i_store, i_val = step - 2 * ATT_SKEW, step - ATT_SKEW
        if 0 <= i_store < len(tiles):
            store_stage(tiles[i_store], *valued.pop(i_store))
        if 0 <= i_val < len(tiles):
            s, m_row = scored.pop(i_val)
            valued[i_val] = (m_row, values_stage(tiles[i_val], s, m_row))
        if step < len(tiles):
            scored[step] = scores_stage(tiles[step])

    npat = len(A_PATTERNS)
    blk = 2 * sub
    low_blk = _iota((blk, LANES), 1) < HEAD_DIM
    for i in range(t // blk):
        rs = pl.ds(i * blk, blk)
        m_g = [m_ref[g, :, rs, :] for g in range(npat)]
        m_all = functools.reduce(jnp.maximum, m_g)
        w_g = [jnp.exp2(m - m_all) for m in m_g]
        l_all = functools.reduce(lambda a, b: a + b, [w * l_ref[g, :, rs, :] for g, w in enumerate(w_g)])
        num = functools.reduce(lambda a, b: a + b,
                               [jnp.where(low_blk, w[0], w[1]) * acc_ref[g, rs, :] for g, w in enumerate(w_g)])
        o_ref[rs, :] = (num / jnp.where(low_blk, l_all[0], l_all[1])).astype(o_ref.dtype)


def _attn(z3):
    b, seq, _ = z3.shape
    t = ATT_TILE
    halo = A_MAX_REACH
    nt = seq // t
    per_tile = t // halo
    n_halo = seq // halo
    qc, kc, vc = Z_AQ // LANES, Z_AK // LANES, Z_AV // LANES

    def cur(c0):
        return pl.BlockSpec((None, t, LANES), lambda bi, hp, i: (bi, i, c0 + hp))

    def prev(c0):
        return pl.BlockSpec((None, halo, LANES),
                            lambda bi, hp, i: (bi, jnp.maximum(i * per_tile - 1, 0), c0 + hp))

    def nxt(c0):
        return pl.BlockSpec((None, halo, LANES),
                            lambda bi, hp, i: (bi, jnp.minimum((i + 1) * per_tile, n_halo - 1), c0 + hp))

    return pl.pallas_call(
        functools.partial(_attn_body, seq),
        grid=(b, A_HEADS // 2, nt),
        in_specs=[cur(qc), prev(kc), cur(kc), nxt(kc), prev(vc), cur(vc), nxt(vc)],
        out_specs=pl.BlockSpec((None, t, LANES), lambda bi, hp, i: (bi, i, hp)),
        out_shape=jax.ShapeDtypeStruct((b, seq, A_WIDTH), BF16),
        scratch_shapes=[
            pltpu.VMEM((t + 2 * halo, LANES), F32),
            pltpu.VMEM((t + 2 * halo, LANES), F32),
            pltpu.VMEM((len(A_PATTERNS), 2, t, LANES), F32),
            pltpu.VMEM((len(A_PATTERNS), 2, t, LANES), F32),
            pltpu.VMEM((len(A_PATTERNS), t, LANES), F32),
            pltpu.VMEM((2 * ATT_SUB, ATT_KEYS), F32),
        ],
        compiler_params=_cparams(("parallel", "parallel", "parallel")),
        name="attn",
    )(z3, z3, z3, z3, z3, z3, z3)


def _split3(x):
    hi = x.astype(BF16)
    r1 = x - hi.astype(F32)
    mid = r1.astype(BF16)
    lo = (r1 - mid.astype(F32)).astype(BF16)
    return hi, mid, lo


def _gla_log_decay(g, gup_ref, gbias_ref):
    logits = _dot(g.astype(BF16), gup_ref[...]) + gbias_ref[...]
    return _log_sigmoid(logits) * (1.0 / B_GATE_TAU)


def _cumsum_rows(la, tri3_ref):
    hi, mid, lo = _split3(la)
    return _dot(tri3_ref[...], jnp.concatenate([hi, mid, lo], axis=0))


def _gla_bwd_body(nch, k_ref, v_ref, g_ref, gup_ref, gbias_ref, tri3_ref, emask_ref, sb_ref, state):
    @pl.when(pl.program_id(1) == 0)
    def _():
        state[...] = jnp.zeros_like(state)

    c = GLA_CHUNK
    ones = jnp.ones((2 * c, 2 * LANES), BF16)

    def step(i, carry):
        ci = nch - 1 - i
        rows = pl.ds(pl.multiple_of(ci * c, c), c)
        la = _gla_log_decay(g_ref[rows, :], gup_ref, gbias_ref)[:, B_QK:]
        cbx = _cumsum_rows(la, tri3_ref) - la
        sb_ref[ci] = state[...].astype(BF16)
        kt = (k_ref[rows, :] * jnp.exp(cbx)).astype(BF16)
        upd = _dot_tn(kt, v_ref[rows, :].astype(BF16))
        hi = la.astype(BF16)
        lo = (la - hi.astype(F32)).astype(BF16)
        tot = _dot_tn(jnp.concatenate([hi, lo], axis=0), ones)
        state[...] = jnp.exp(tot) * state[...] + upd * emask_ref[...]
        return carry

    lax.fori_loop(0, nch, step, 0, unroll=2)


def _gla_main_body(nch, q_ref, k_ref, v_ref, r_ref, g_ref, sb_ref, gup_ref, gbias_ref, tri3_ref,
                   emask_ref, ebf_ref, ones_ref, gain_ref, o_ref, state):
    @pl.when(pl.program_id(1) == 0)
    def _():
        state[...] = jnp.zeros_like(state)

    c = GLA_CHUNK
    n = GLA_SUB
    nsub = c // n
    ones = jnp.ones((2 * c, 2 * LANES), BF16)
    row = _iota((c, B_QK), 0)
    row_8 = _iota((8, B_QK), 0)
    rho = _iota((B_HEADS * c, B_QK), 0)
    lane_k = _iota((B_HEADS * c, B_QK), 1)
    head_ok = (rho >> 6) == (lane_k >> 5)
    sblk = (rho & (c - 1)) >> 4
    rho_v = _iota((B_HEADS * c, B_WIDTH), 0)
    lane_v = _iota((B_HEADS * c, B_WIDTH), 1)
    vhead_ok = (rho_v >> 6) == (lane_v >> 6)

    def step(ci, carry):
        rows = pl.ds(pl.multiple_of(ci * c, c), c)
        q = q_ref[rows, :]
        k = k_ref[rows, :]
        v = v_ref[rows, :]
        la = _gla_log_decay(g_ref[rows, :], gup_ref, gbias_ref)
        cum = _cumsum_rows(la, tri3_ref)
        bf = cum[:, :B_QK]
        la_b = la[:, B_QK:]
        cbx = cum[:, B_QK:] - la_b
        tot_b = cum[c - 1:c, B_QK:]
        bf2 = bf * LOG2E
        cb2 = cbx * LOG2E

        s_f = state[...]
        qf = q * jnp.exp(bf)
        qb = q * jnp.exp(tot_b - cbx)
        lhs = jnp.concatenate([qf, qb], axis=1).astype(BF16)
        rhs = jnp.concatenate([s_f.astype(BF16), sb_ref[ci]], axis=0)
        o = _dot(lhs, rhs)

        e_rows = [bf[n * j + n - 1:n * j + n, :] for j in range(nsub)]
        f_rows = [cbx[n * j:n * j + 1, :] for j in range(nsub)]
        e_blk = jnp.concatenate([jnp.broadcast_to(e, (n, B_QK)) for e in e_rows], axis=0)
        f_blk = jnp.concatenate([jnp.broadcast_to(f, (n, B_QK)) for f in f_rows], axis=0)
        kf = k * jnp.exp(e_blk - bf)
        kb = k * jnp.exp(cbx - f_blk)
        kf4 = jnp.concatenate([kf] * B_HEADS, axis=0)
        kb4 = jnp.concatenate([kb] * B_HEADS, axis=0)
        lhs_parts = []
        rhs_parts = []
        for j in range(nsub - 1):
            qj = jnp.where(row >= n * (j + 1), q * jnp.exp(jnp.minimum(bf - e_rows[j], 0.0)), 0.0)
            lhs_parts.append(qj)
            rhs_parts.append(jnp.where(head_ok & (sblk == j), kf4, 0.0))
        for j in range(1, nsub):
            qj = jnp.where(row < n * j, q * jnp.exp(jnp.minimum(f_rows[j] - cbx, 0.0)), 0.0)
            lhs_parts.append(qj)
            rhs_parts.append(jnp.where(head_ok & (sblk == j), kb4, 0.0))
        att = _dot_nt(jnp.concatenate(lhs_parts, axis=1).astype(BF16),
                      jnp.concatenate(rhs_parts, axis=1).astype(BF16))
        v4 = jnp.where(vhead_ok, jnp.concatenate([v] * B_HEADS, axis=0), 0.0).astype(BF16)
        o = o + _dot(att.astype(BF16), v4)

        diag = []
        for i in range(nsub):
            lo_, hi_ = n * i, n * (i + 1)
            qi, ki, vi = q[lo_:hi_], k[lo_:hi_], v[lo_:hi_]
            bfi, cbi = bf2[lo_:hi_], cb2[lo_:hi_]
            slabs = []
            for s in range(n):
                halves = []
                for r0 in range(0, n, 8):
                    fwd = bfi[r0:r0 + 8] - bfi[s:s + 1]
                    bwd = cbi[s:s + 1] - cbi[r0:r0 + 8]
                    if s <= r0:
                        halves.append(fwd)
                    elif s >= r0 + 8:
                        halves.append(bwd)
                    else:
                        halves.append(jnp.where(row_8 >= s - r0, fwd, bwd))
                arg = jnp.concatenate(halves, axis=0)
                slabs.append(((qi * ki[s:s + 1]) * jnp.exp2(arg)).astype(BF16))
            zz = _dot(jnp.concatenate(slabs, axis=0), ebf_ref[...])
            acc = zz[0:n] * vi[0:1]
            for s in range(1, n):
                acc = acc + zz[n * s:n * (s + 1)] * vi[s:s + 1]
            diag.append(acc)
        o = o + jnp.concatenate(diag, axis=0)

        last = bf[c - 1:c, :]
        kt = (k * jnp.exp(last - bf)).astype(BF16)
        upd = _dot_tn(kt, v.astype(BF16))
        la_f = la[:, :B_QK]
        hi = la_f.astype(BF16)
        lo = (la_f - hi.astype(F32)).astype(BF16)
        tot = _dot_tn(jnp.concatenate([hi, lo], axis=0), ones)
        state[...] = jnp.exp(tot) * s_f + upd * emask_ref[...]

        gate = r_ref[rows, :]
        out = _head_norm(o, ones_ref, gain_ref[...]) * (gate * _sigmoid(gate))
        o_ref[rows, :] = out.astype(o_ref.dtype)
        return carry

    lax.fori_loop(0, nch, step, 0, unroll=2)


def _gla(z3, gup_bd, gbias, tri3, emask, ebf, ones256, gain, tb):
    b, seq, _ = z3.shape
    nblk = seq // tb
    nch = tb // GLA_CHUNK
    ntot = seq // GLA_CHUNK

    def zspec(col, width, rev):
        blk = col // width
        if rev:
            return pl.BlockSpec((None, tb, width), lambda bi, i: (bi, nblk - 1 - i, blk))
        return pl.BlockSpec((None, tb, width), lambda bi, i: (bi, i, blk))

    consts = [gup_bd, gbias, tri3, emask]
    const_specs = [_const_spec(a.shape) for a in consts]
    sb = pl.pallas_call(
        functools.partial(_gla_bwd_body, nch),
        grid=(b, nblk),
        in_specs=[zspec(Z_BK, B_QK, True), zspec(Z_BV, B_WIDTH, True), zspec(Z_BG, LANES, True)]
        + const_specs,
        out_specs=pl.BlockSpec((None, nch, B_QK, B_WIDTH), lambda bi, i: (bi, nblk - 1 - i, 0, 0)),
        out_shape=jax.ShapeDtypeStruct((b, ntot, B_QK, B_WIDTH), BF16),
        scratch_shapes=[pltpu.VMEM((B_QK, B_WIDTH), F32)],
        compiler_params=_cparams(("parallel", "arbitrary")),
        name="gla_bwd",
    )(z3, z3, z3, *consts)

    consts2 = [gup_bd, gbias, tri3, emask, ebf, ones256, gain]
    return pl.pallas_call(
        functools.partial(_gla_main_body, nch),
        grid=(b, nblk),
        in_specs=[zspec(Z_BQ, B_QK, False), zspec(Z_BK, B_QK, False), zspec(Z_BV, B_WIDTH, False),
                  zspec(Z_BR, B_WIDTH, False), zspec(Z_BG, LANES, False),
                  pl.BlockSpec((None, nch, B_QK, B_WIDTH), lambda bi, i: (bi, i, 0, 0))]
        + [_const_spec(a.shape) for a in consts2],
        out_specs=pl.BlockSpec((None, tb, B_WIDTH), lambda bi, i: (bi, i, 0)),
        out_shape=jax.ShapeDtypeStruct((b, seq, B_WIDTH), BF16),
        scratch_shapes=[pltpu.VMEM((B_QK, B_WIDTH), F32)],
        compiler_params=_cparams(("parallel", "arbitrary")),
        name="gla_main",
    )(z3, z3, z3, z3, z3, sb, *consts2)


def _ret_bwd_body(nch, k_ref, v_ref, raw_ref, bmask_ref, rb_ref, state):
    @pl.when(pl.program_id(1) == 0)
    def _():
        state[...] = jnp.zeros_like(state)

    c = RET_CHUNK
    lg1 = _log_sigmoid(raw_ref[1:2, :])
    pos = _iota((c, C_WIDTH), 0).astype(F32)
    kdec = jnp.exp(pos * lg1)
    chunk_decay = jnp.exp(float(c) * lg1)

    def step(i, carry):
        ci = nch - 1 - i
        rows = pl.ds(pl.multiple_of(ci * c, c), c)
        rb_ref[ci] = state[...].astype(BF16)
        kt = (k_ref[rows, :] * kdec).astype(BF16)
        upd = _dot_tn(kt, v_ref[rows, :].astype(BF16))
        state[...] = chunk_decay * state[...] + upd * bmask_ref[...]
        return carry

    lax.fori_loop(0, nch, step, 0, unroll=2)


def _ret_main_body(nch, q_ref, k_ref, v_ref, g_ref, rb_ref, raw_ref, raw_s_ref, bmask_ref, ones_ref,
                   gain_ref, o_ref, state):
    @pl.when(pl.program_id(1) == 0)
    def _():
        state[...] = jnp.zeros_like(state)

    c = RET_CHUNK
    lg = _log_sigmoid(raw_ref[...])
    lg0, lg1 = lg[0:1, :], lg[1:2, :]
    lgs = _log_sigmoid(raw_s_ref[...])
    pos = _iota((c, C_WIDTH), 0).astype(F32)
    qdec_f = jnp.exp((pos + 1.0) * lg0)
    qdec_b = jnp.exp((float(c) - pos) * lg1)
    kdec_f = jnp.exp((float(c) - 1.0 - pos) * lg0)
    chunk_decay = jnp.exp(float(c) * lg0)
    t_idx = _iota((c, C_HEADS * c), 0)
    s_idx = _iota((c, C_HEADS * c), 1) & (c - 1)
    rel = (t_idx - s_idx).astype(F32)
    dmat = jnp.where(rel >= 0.0, jnp.exp(jnp.maximum(rel, 0.0) * lgs[0:1, :]),
                     jnp.exp(jnp.maximum(-rel, 0.0) * lgs[1:2, :]))
    rho = _iota((C_HEADS * c, C_WIDTH), 0)
    lane = _iota((C_HEADS * c, C_WIDTH), 1)
    head_ok = (rho >> 7) == (lane >> 6)

    def step(ci, carry):
        rows = pl.ds(pl.multiple_of(ci * c, c), c)
        q = q_ref[rows, :]
        k = k_ref[rows, :]
        v = v_ref[rows, :]
        k4 = jnp.where(head_ok, jnp.concatenate([k] * C_HEADS, axis=0), 0.0).astype(BF16)
        v4 = jnp.where(head_ok, jnp.concatenate([v] * C_HEADS, axis=0), 0.0).astype(BF16)
        scores = _dot_nt(q.astype(BF16), k4) * dmat
        o = _dot(scores.astype(BF16), v4)
        r_f = state[...]
        lhs = jnp.concatenate([q * qdec_f, q * qdec_b], axis=1).astype(BF16)
        rhs = jnp.concatenate([r_f.astype(BF16), rb_ref[ci]], axis=0)
        o = o + _dot(lhs, rhs)
        upd = _dot_tn((k * kdec_f).astype(BF16), v.astype(BF16))
        state[...] = chunk_decay * r_f + upd * bmask_ref[...]
        gate = g_ref[rows, :]
        out = _head_norm(o, ones_ref, gain_ref[...]) * (gate * _sigmoid(gate))
        o_ref[rows, :] = out.astype(o_ref.dtype)
        return carry

    lax.fori_loop(0, nch, step, 0, unroll=2)


def _ret(z3, raw256, raw512, ones256, gain, tb):
    b, seq, _ = z3.shape
    nblk = seq // tb
    nch = tb // RET_CHUNK
    ntot = seq // RET_CHUNK

    def zspec(col, rev):
        blk = col // C_WIDTH
        if rev:
            return pl.BlockSpec((None, tb, C_WIDTH), lambda bi, i: (bi, nblk - 1 - i, blk))
        return pl.BlockSpec((None, tb, C_WIDTH), lambda bi, i: (bi, i, blk))

    rb = pl.pallas_call(
        functools.partial(_ret_bwd_body, nch),
        grid=(b, nblk),
        in_specs=[zspec(Z_CK, True), zspec(Z_CV, True), _const_spec(raw256.shape),
                  _const_spec(ones256.shape)],
        out_specs=pl.BlockSpec((None, nch, C_WIDTH, C_WIDTH), lambda bi, i: (bi, nblk - 1 - i, 0, 0)),
        out_shape=jax.ShapeDtypeStruct((b, ntot, C_WIDTH, C_WIDTH), BF16),
        scratch_shapes=[pltpu.VMEM((C_WIDTH, C_WIDTH), F32)],
        compiler_params=_cparams(("parallel", "arbitrary")),
        name="ret_bwd",
    )(z3, z3, raw256, ones256)

    consts = [raw256, raw512, ones256, ones256, gain]
    return pl.pallas_call(
        functools.partial(_ret_main_body, nch),
        grid=(b, nblk),
        in_specs=[zspec(Z_CQ, False), zspec(Z_CK, False), zspec(Z_CV, False), zspec(Z_CG, False),
                  pl.BlockSpec((None, nch, C_WIDTH, C_WIDTH), lambda bi, i: (bi, i, 0, 0))]
        + [_const_spec(a.shape) for a in consts],
        out_specs=pl.BlockSpec((None, tb, C_WIDTH), lambda bi, i: (bi, i, 0)),
        out_shape=jax.ShapeDtypeStruct((b, seq, C_WIDTH), BF16),
        scratch_shapes=[pltpu.VMEM((C_WIDTH, C_WIDTH), F32)],
        compiler_params=_cparams(("parallel", "arbitrary")),
        name="ret_main",
    )(z3, z3, z3, z3, rb, *consts)


def _post_body(h_ref, oa_ref, ob_ref, oc_ref, p_ref, wo_ref, lm_ref, w1_ref, w2_ref, lp_ref,
               wg_ref, wp_ref, y_ref):
    def rms(x, g_ref):
        ms = jnp.mean(x * x, axis=-1, keepdims=True)
        return (x * lax.rsqrt(ms + EPS) * g_ref[...]).astype(BF16)

    h = h_ref[...]
    h = h + (_dot(oa_ref[...], wo_ref[0:A_WIDTH, :])
             + _dot(ob_ref[...], wo_ref[A_WIDTH:A_WIDTH + B_WIDTH, :])
             + _dot(oc_ref[...], wo_ref[A_WIDTH + B_WIDTH:, :]))
    m = rms(h, lm_ref)
    ff = D_FF // 4
    mlp = None
    for j in range(4):
        hid = _dot(m, w1_ref[:, ff * j:ff * (j + 1)])
        hid = jnp.square(jnp.maximum(hid, 0.0)).astype(BF16)
        part = _dot(hid, w2_ref[ff * j:ff * (j + 1), :])
        mlp = part if mlp is None else mlp + part
    h = h + mlp
    gate = _sigmoid(_dot(rms(h, lp_ref), wg_ref[...]))
    y_ref[...] = h + gate * _dot(p_ref[...].astype(BF16), wp_ref[...])


def _post(h2, oa, ob, oc, ple, wo, lm, w1, w2, lp, wg, wp, tm):
    n = h2.shape[0]

    def tok(width):
        return pl.BlockSpec((tm, width), lambda i: (i, 0))

    consts = [wo, lm, w1, w2, lp, wg, wp]
    return pl.pallas_call(
        _post_body,
        grid=(n // tm,),
        in_specs=[tok(D_MODEL), tok(A_WIDTH), tok(B_WIDTH), tok(C_WIDTH), tok(PLE_DIM)]
        + [_const_spec(a.shape) for a in consts],
        out_specs=tok(D_MODEL),
        out_shape=jax.ShapeDtypeStruct((n, D_MODEL), F32),
        compiler_params=_cparams(("parallel",)),
        name="post",
    )(h2, oa, ob, oc, ple, *consts)


def _rope_tables(seq, rot_dim, theta):
    half = rot_dim // 2
    inv_freq = 1.0 / (theta ** (jnp.arange(half, dtype=F32) * (2.0 / rot_dim)))
    ang = jnp.arange(seq, dtype=F32)[:, None] * inv_freq[None, :]
    cos, sin = jnp.cos(ang), jnp.sin(ang)
    pad = HEAD_DIM - rot_dim
    c = jnp.concatenate([cos, cos, jnp.ones((seq, pad), F32)], axis=1)
    sn = jnp.concatenate([-sin, jnp.zeros((seq, half + pad), F32)], axis=1)
    sp = jnp.concatenate([jnp.zeros((seq, half), F32), sin, jnp.zeros((seq, pad), F32)], axis=1)
    return jnp.concatenate([jnp.tile(t, (1, LANES // HEAD_DIM)) for t in (c, sn, sp)], axis=1)


def _block_ones(rows, cols, rblk, cblk):
    r = jnp.arange(rows)[:, None] // rblk
    c = jnp.arange(cols)[None, :] // cblk
    return r == c


def _layer_consts(i, ln_mix, w_in, attn_q_norm, attn_k_norm, gla_gate_up, gla_gate_bias, gla_out_norm,
                  ret_decay_raw, ret_out_norm, w_out, ln_mlp, w_mlp_in, w_mlp_out, ln_pe, w_pe_gate,
                  w_pe_proj):
    w = w_in[i]
    w_p = jnp.concatenate([w[:, :2304], w[:, 2336:N_IN], w[:, 2304:2336],
                           jnp.zeros((D_MODEL, Z_WIDTH - N_IN), F32)], axis=1).astype(BF16)
    gup = gla_gate_up[i].astype(BF16)
    gup_bd = jnp.zeros((LANES, 2 * B_QK), BF16)
    gup_bd = gup_bd.at[0:B_GATE_RANK, 0:B_QK].set(gup[0])
    gup_bd = gup_bd.at[B_GATE_RANK:2 * B_GATE_RANK, B_QK:].set(gup[1])
    return dict(
        ln_mix=ln_mix[i][None, :], w_p=w_p,
        qg=jnp.tile(attn_q_norm[i], 4)[None, :], kg=jnp.tile(attn_k_norm[i], 4)[None, :],
        gup_bd=gup_bd, gbias=gla_gate_bias[i].reshape(1, 2 * B_QK),
        gla_gain=gla_out_norm[i][None, :],
        raw256=jnp.repeat(ret_decay_raw[i], C_DIM, axis=1),
        raw512=jnp.repeat(ret_decay_raw[i], RET_CHUNK, axis=1),
        ret_gain=ret_out_norm[i][None, :],
        wo=w_out[i].astype(BF16), lm=ln_mlp[i][None, :], w1=w_mlp_in[i].astype(BF16),
        w2=w_mlp_out[i].astype(BF16), lp=ln_pe[i][None, :], wg=w_pe_gate[i].astype(BF16),
        wp=w_pe_proj[i].astype(BF16),
    )


def _run_trunk(x, p, layers, shared):
    b, seq, _ = x.shape
    n = b * seq
    tm = 512
    tb = 512
    rope_a = _rope_tables(seq, ROPE_DIM, ROPE_THETA)
    rope_c = _rope_tables(seq, C_DIM, RET_THETA)
    h = x.reshape(n, D_MODEL)
    for i, lc in enumerate(layers):
        z = _proj_in(h, seq, lc["ln_mix"], lc["w_p"], shared["ones256"], lc["qg"], lc["kg"],
                     rope_a, rope_c, tm)
        z3 = z.reshape(b, seq, Z_WIDTH)
        oa = _attn(z3)
        ob = _gla(z3, lc["gup_bd"], lc["gbias"], shared["tri3"], shared["emask"], shared["ebf"],
                  shared["ones256"], lc["gla_gain"], tb)
        oc = _ret(z3, lc["raw256"], lc["raw512"], shared["ones256"], lc["ret_gain"], tb)
        h = _post(h, oa.reshape(n, A_WIDTH), ob.reshape(n, B_WIDTH), oc.reshape(n, C_WIDTH),
                  p[i].reshape(n, PLE_DIM), lc["wo"], lc["lm"], lc["w1"], lc["w2"], lc["lp"],
                  lc["wg"], lc["wp"], tm)
    return h.reshape(b, seq, D_MODEL)


def kernel(x_prompt, x_sample, p_prompt, p_sample, ln_mix, w_in, attn_q_norm, attn_k_norm, gla_gate_up, gla_gate_bias, gla_out_norm, ret_decay_raw, ret_out_norm, w_out, ln_mlp, w_mlp_in, w_mlp_out, ln_pe, w_pe_gate, w_pe_proj):
    depth = w_in.shape[0]
    layers = [_layer_consts(i, ln_mix, w_in, attn_q_norm, attn_k_norm, gla_gate_up, gla_gate_bias,
                            gla_out_norm, ret_decay_raw, ret_out_norm, w_out, ln_mlp, w_mlp_in,
                            w_mlp_out, ln_pe, w_pe_gate, w_pe_proj) for i in range(depth)]
    tri = jnp.arange(GLA_CHUNK)[:, None] >= jnp.arange(GLA_CHUNK)[None, :]
    emask = _block_ones(B_QK, B_WIDTH, B_KDIM, HEAD_DIM)
    shared = dict(
        ones256=_block_ones(256, 256, HEAD_DIM, HEAD_DIM).astype(BF16),
        tri3=jnp.tile(tri, (1, 3)).astype(BF16),
        emask=emask.astype(F32),
        ebf=emask.astype(BF16),
    )
    y_prompt = _run_trunk(x_prompt, p_prompt, layers, shared)
    y_sample = _run_trunk(x_sample, p_sample, layers, shared)
    return (y_prompt, y_sample)
```

```python
import functools

import jax
import jax.numpy as jnp
from jax import lax
from jax.experimental import pallas as pl
from jax.experimental.pallas import tpu as pltpu

F32 = jnp.float32
BF16 = jnp.bfloat16

D_MODEL = 1024
HEAD_DIM = 64
A_HEADS = 8
A_WIDTH = 512
A_PATTERNS = ((128, 1), (512, 4), (2048, 16))
A_RADIUS = 64
A_MAX_REACH = 1024
ROPE_THETA = 500000.0
ROPE_DIM = 16
B_HEADS = 4
B_KDIM = 32
B_QK = 128
B_WIDTH = 256
B_GATE_RANK = 16
B_GATE_TAU = 16.0
C_HEADS = 4
C_DIM = 64
C_WIDTH = 256
RET_THETA = 10000.0
N_IN = 3360
D_FF = 4096
PLE_DIM = 256
EPS = 1e-6
NEG = -1e30
LOG2E = 1.4426950408889634

LANES = 128
VMEM_LIMIT = 56 * 1024 * 1024

Z_AQ, Z_AK, Z_AV = 0, 512, 1024
Z_BQ, Z_BK, Z_BV, Z_BR = 1536, 1664, 1792, 2048
Z_CQ, Z_CK, Z_CV, Z_CG = 2304, 2560, 2816, 3072
Z_BG = 3328
Z_WIDTH = 3456

GLA_CHUNK = 64
GLA_SUB = 16
GLA_GROUP = 4
RET_CHUNK = 128
ATT_TILE = 2048
ATT_SUB = 128
ATT_KEYS = ATT_SUB + 2 * A_RADIUS
ATT_SKEW = 2


def _cparams(sem):
    return pltpu.CompilerParams(dimension_semantics=sem, vmem_limit_bytes=VMEM_LIMIT)


def _const_spec(shape):
    nd = len(shape)
    return pl.BlockSpec(shape, lambda *_: (0,) * nd, pipeline_mode=pl.Buffered(1))


def _sigmoid(x):
    return 1.0 / (1.0 + jnp.exp(-x))


def _log_sigmoid(x):
    return jnp.minimum(x, 0.0) - jnp.log1p(jnp.exp(-jnp.abs(x)))


def _iota(shape, dim):
    return lax.broadcasted_iota(jnp.int32, shape, dim)


def _dot(a, b):
    return jnp.dot(a, b, preferred_element_type=F32)


def _dot_nt(a, b):
    return lax.dot_general(a, b, (((1,), (1,)), ((), ())), preferred_element_type=F32)


def _dot_tn(a, b):
    return lax.dot_general(a, b, (((0,), (0,)), ((), ())), preferred_element_type=F32)


def _head_norm(o, ones_ref, gain):
    ssum = _dot((o * o).astype(BF16), ones_ref[...])
    return o * lax.rsqrt(ssum * (1.0 / HEAD_DIM) + EPS) * gain


def _rope(y, tab_ref, shift):
    c = tab_ref[:, 0:LANES]
    sn = tab_ref[:, LANES:2 * LANES]
    sp = tab_ref[:, 2 * LANES:3 * LANES]
    outs = []
    for j in range(y.shape[1] // LANES):
        yj = y[:, j * LANES:(j + 1) * LANES]
        outs.append(yj * c + pltpu.roll(yj, LANES - shift, 1) * sn + pltpu.roll(yj, shift, 1) * sp)
    return jnp.concatenate(outs, axis=1)


def _proj_in_body(x_ref, g_ref, w_ref, ones_ref, qg_ref, kg_ref, ra_ref, rc_ref, z_ref):
    x = x_ref[...]
    ms = jnp.mean(x * x, axis=-1, keepdims=True)
    u = (x * lax.rsqrt(ms + EPS) * g_ref[...]).astype(BF16)

    def proj(a, b):
        return _dot(u, w_ref[:, a:b])

    def qk_norm(y, gain_ref):
        halves = []
        for j in range(2):
            yj = y[:, 256 * j:256 * (j + 1)]
            halves.append(_head_norm(yj, ones_ref, gain_ref[...]))
        return jnp.concatenate(halves, axis=1)

    aq = _rope(qk_norm(proj(Z_AQ, Z_AK), qg_ref), ra_ref, ROPE_DIM // 2)
    z_ref[:, Z_AQ:Z_AK] = aq * (HEAD_DIM ** -0.5 * LOG2E)
    z_ref[:, Z_AK:Z_AV] = _rope(qk_norm(proj(Z_AK, Z_AV), kg_ref), ra_ref, ROPE_DIM // 2)
    z_ref[:, Z_AV:Z_BQ] = proj(Z_AV, Z_BQ)
    z_ref[:, Z_BQ:Z_BK] = proj(Z_BQ, Z_BK) * (B_KDIM ** -0.5)
    z_ref[:, Z_BK:Z_CQ] = proj(Z_BK, Z_CQ)
    z_ref[:, Z_CQ:Z_CK] = _rope(proj(Z_CQ, Z_CK), rc_ref, C_DIM // 2)
    z_ref[:, Z_CK:Z_CV] = _rope(proj(Z_CK, Z_CV), rc_ref, C_DIM // 2) * (C_DIM ** -0.5)
    z_ref[:, Z_CV:Z_WIDTH] = proj(Z_CV, Z_WIDTH)


def _proj_in(x2, seq, ln, w_p, ones256, qg, kg, rope_a, rope_c, tm):
    n = x2.shape[0]
    per_seq = seq // tm
    return pl.pallas_call(
        _proj_in_body,
        grid=(n // tm,),
        in_specs=[
            pl.BlockSpec((tm, D_MODEL), lambda i: (i, 0)),
            _const_spec((1, D_MODEL)),
            _const_spec((D_MODEL, Z_WIDTH)),
            _const_spec((256, 256)),
            _const_spec((1, 256)),
            _const_spec((1, 256)),
            pl.BlockSpec((tm, 3 * LANES), lambda i: (i % per_seq, 0)),
            pl.BlockSpec((tm, 3 * LANES), lambda i: (i % per_seq, 0)),
        ],
        out_specs=pl.BlockSpec((tm, Z_WIDTH), lambda i: (i, 0)),
        out_shape=jax.ShapeDtypeStruct((n, Z_WIDTH), F32),
        compiler_params=_cparams(("parallel",)),
        name="proj_in",
    )(x2, ln, w_p, ones256, qg, kg, rope_a, rope_c)


def _attn_body(seq, q_ref, kp_ref, kc_ref, kn_ref, vp_ref, vc_ref, vn_ref, o_ref,
               kbuf, vbuf, m_ref, l_ref, acc_ref, band_ref):
    t = ATT_TILE
    halo = A_MAX_REACH
    sub = ATT_SUB
    tile_start = pl.program_id(2) * t
    kbuf[0:halo, :] = kp_ref[...]
    kbuf[halo:halo + t, :] = kc_ref[...]
    kbuf[halo + t:, :] = kn_ref[...]
    vbuf[0:halo, :] = vp_ref[...]
    vbuf[halo:halo + t, :] = vc_ref[...]
    vbuf[halo + t:, :] = vn_ref[...]

    low_half = _iota((sub, LANES), 1) < HEAD_DIM
    a_idx = _iota((2 * sub, ATT_KEYS), 0) & (sub - 1)
    rel = _iota((2 * sub, ATT_KEYS), 1) - a_idx
    band_ref[...] = jnp.where((rel >= 0) & (rel <= 2 * A_RADIUS), 0.0, NEG)
    c_row = _iota((1, ATT_KEYS), 1)
    ones_v = jnp.ones((ATT_KEYS, LANES), BF16)

    def scores_stage(tile):
        pat, qs, dil = tile
        span = A_RADIUS * dil
        q_sub = q_ref[pl.ds(qs, sub, stride=dil), :]
        q_st = jnp.concatenate([jnp.where(low_half, q_sub, 0.0),
                                jnp.where(low_half, 0.0, q_sub)], axis=0).astype(BF16)
        k_sub = kbuf[pl.ds(halo + qs - span, ATT_KEYS, stride=dil), :].astype(BF16)
        s = _dot_nt(q_st, k_sub)
        key_pos = (tile_start + qs - span) + dil * c_row
        bias = jnp.where((key_pos >= 0) & (key_pos < seq), 0.0, NEG)
        s = (s + band_ref[...]) + bias
        return s, jnp.broadcast_to(jnp.max(s, axis=1, keepdims=True), (2 * sub, LANES))

    def values_stage(tile, s, m_row):
        pat, qs, dil = tile
        p = jnp.exp2(s - jnp.concatenate([m_row] * (ATT_KEYS // LANES), axis=1)).astype(BF16)
        v_sub = vbuf[pl.ds(halo + qs - A_RADIUS * dil, ATT_KEYS, stride=dil), :].astype(BF16)
        return _dot(p, jnp.concatenate([v_sub, ones_v], axis=1))

    def store_stage(tile, m_row, pvl):
        pat, qs, dil = tile
        rw = pl.ds(qs, sub, stride=dil)
        m_ref[pat, 0, rw, :] = m_row[0:sub]
        m_ref[pat, 1, rw, :] = m_row[sub:]
        l_ref[pat, 0, rw, :] = pvl[0:sub, LANES:]
        l_ref[pat, 1, rw, :] = pvl[sub:, LANES:]
        acc_ref[pat, rw, :] = jnp.where(low_half, pvl[0:sub, 0:LANES], pvl[sub:, 0:LANES])

    tiles = [(pat, r + j * sub * dil, dil)
             for pat, (_, dil) in enumerate(A_PATTERNS)
             for r in range(dil) for j in range(t // (sub * dil))]
    scored, valued = {}, {}
    for step in range(len(tiles) + 2 * ATT_SKEW):
        i_store, i_val = step - 2 * ATT_SKEW, step - ATT_SKEW
        if 0 <= i_store < len(tiles):
            store_stage(tiles[i_store], *valued.pop(i_store))
        if 0 <= i_val < len(tiles):
            s, m_row = scored.pop(i_val)
            valued[i_val] = (m_row, values_stage(tiles[i_val], s, m_row))
        if step < len(tiles):
            scored[step] = scores_stage(tiles[step])

    npat = len(A_PATTERNS)
    blk = 2 * sub
    low_blk = _iota((blk, LANES), 1) < HEAD_DIM
    for i in range(t // blk):
        rs = pl.ds(i * blk, blk)
        m_g = [m_ref[g, :, rs, :] for g in range(npat)]
        m_all = functools.reduce(jnp.maximum, m_g)
        w_g = [jnp.exp2(m - m_all) for m in m_g]
        l_all = functools.reduce(lambda a, b: a + b, [w * l_ref[g, :, rs, :] for g, w in enumerate(w_g)])
        num = functools.reduce(lambda a, b: a + b,
                               [jnp.where(low_blk, w[0], w[1]) * acc_ref[g, rs, :] for g, w in enumerate(w_g)])
        o_ref[rs, :] = (num / jnp.where(low_blk, l_all[0], l_all[1])).astype(o_ref.dtype)


def _attn(z3):
    b, seq, _ = z3.shape
    t = ATT_TILE
    halo = A_MAX_REACH
    nt = seq // t
    per_tile = t // halo
    n_halo = seq // halo
    qc, kc, vc = Z_AQ // LANES, Z_AK // LANES, Z_AV // LANES

    def cur(c0):
        return pl.BlockSpec((None, t, LANES), lambda bi, hp, i: (bi, i, c0 + hp))

    def prev(c0):
        return pl.BlockSpec((None, halo, LANES),
                            lambda bi, hp, i: (bi, jnp.maximum(i * per_tile - 1, 0), c0 + hp))

    def nxt(c0):
        return pl.BlockSpec((None, halo, LANES),
                            lambda bi, hp, i: (bi, jnp.minimum((i + 1) * per_tile, n_halo - 1), c0 + hp))

    return pl.pallas_call(
        functools.partial(_attn_body, seq),
        grid=(b, A_HEADS // 2, nt),
        in_specs=[cur(qc), prev(kc), cur(kc), nxt(kc), prev(vc), cur(vc), nxt(vc)],
        out_specs=pl.BlockSpec((None, t, LANES), lambda bi, hp, i: (bi, i, hp)),
        out_shape=jax.ShapeDtypeStruct((b, seq, A_WIDTH), BF16),
        scratch_shapes=[
            pltpu.VMEM((t + 2 * halo, LANES), F32),
            pltpu.VMEM((t + 2 * halo, LANES), F32),
            pltpu.VMEM((len(A_PATTERNS), 2, t, LANES), F32),
            pltpu.VMEM((len(A_PATTERNS), 2, t, LANES), F32),
            pltpu.VMEM((len(A_PATTERNS), t, LANES), F32),
            pltpu.VMEM((2 * ATT_SUB, ATT_KEYS), F32),
        ],
        compiler_params=_cparams(("parallel", "parallel", "parallel")),
        name="attn",
    )(z3, z3, z3, z3, z3, z3, z3)


def _split3(x):
    hi = x.astype(BF16)
    r1 = x - hi.astype(F32)
    mid = r1.astype(BF16)
    lo = (r1 - mid.astype(F32)).astype(BF16)
    return hi, mid, lo


def _gla_log_decay(g, gup_ref, gbias_ref):
    logits = _dot(g.astype(BF16), gup_ref[...]) + gbias_ref[...]
    return _log_sigmoid(logits) * (1.0 / B_GATE_TAU)


def _cumsum_rows(la, tri3_ref):
    hi, mid, lo = _split3(la)
    return _dot(tri3_ref[...], jnp.concatenate([hi, mid, lo], axis=0))


def _gla_bwd_body(nit, k_ref, v_ref, g_ref, gup_ref, gbias_ref, tri3_ref, emask_ref, sb_ref, state):
    @pl.when(pl.program_id(1) == 0)
    def _():
        state[...] = jnp.zeros_like(state)

    c = GLA_CHUNK
    grp = GLA_GROUP
    span = grp * c

    def step(i, carry):
        it = nit - 1 - i
        rows = pl.ds(pl.multiple_of(it * span, span), span)
        logits = _dot(g_ref[rows, :].astype(BF16), gup_ref[:, B_QK:]) + gbias_ref[:, B_QK:]
        la = _log_sigmoid(logits) * (1.0 / B_GATE_TAU)
        cums = [_cumsum_rows(la[c * j:c * (j + 1)], tri3_ref) for j in range(grp)]
        cbx = jnp.concatenate(cums, axis=0) - la
        kt = (k_ref[rows, :] * jnp.exp(cbx)).astype(BF16)
        vb = v_ref[rows, :].astype(BF16)
        upd = [_dot_tn(vb[c * j:c * (j + 1)], kt[c * j:c * (j + 1)]) for j in range(grp)]
        st = state[...]
        for j in reversed(range(grp)):
            sb_ref[it * grp + j] = st.astype(BF16)
            st = st * jnp.exp(cums[j][c - 1:c, :]) + upd[j] * emask_ref[...]
        state[...] = st
        return carry

    lax.fori_loop(0, nit, step, 0)


def _gla_main_body(nit, q_ref, k_ref, v_ref, r_ref, g_ref, sb_ref, gup_ref, gbias_ref, tri3_ref,
                   emask_ref, ebf_ref, ones_ref, gain_ref, o_ref, state):
    @pl.when(pl.program_id(1) == 0)
    def _():
        state[...] = jnp.zeros_like(state)

    c = GLA_CHUNK
    n = GLA_SUB
    nsub = c // n
    grp = GLA_GROUP
    span = grp * c
    row = _iota((c, B_QK), 0)
    row_8 = _iota((8, B_QK), 0)
    rho = _iota((B_HEADS * c, B_QK), 0)
    lane_k = _iota((B_HEADS * c, B_QK), 1)
    head_ok = (rho >> 6) == (lane_k >> 5)
    sblk = (rho & (c - 1)) >> 4
    rho_v = _iota((B_HEADS * c, B_WIDTH), 0)
    lane_v = _iota((B_HEADS * c, B_WIDTH), 1)
    vhead_ok = (rho_v >> 6) == (lane_v >> 6)

    def rows_of(vals, height):
        return jnp.concatenate([jnp.broadcast_to(x, (height, x.shape[1])) for x in vals], axis=0)

    def chunk_local(q, k, v, bf, cbx, bf2, cb2, kf, kb):
        e_rows = [bf[n * j + n - 1:n * j + n, :] for j in range(nsub)]
        f_rows = [cbx[n * j:n * j + 1, :] for j in range(nsub)]
        kf4 = jnp.concatenate([kf] * B_HEADS, axis=0)
        kb4 = jnp.concatenate([kb] * B_HEADS, axis=0)
        lhs_parts = []
        rhs_parts = []
        for j in range(nsub - 1):
            qj = jnp.where(row >= n * (j + 1), q * jnp.exp(jnp.minimum(bf - e_rows[j], 0.0)), 0.0)
            lhs_parts.append(qj)
            rhs_parts.append(jnp.where(head_ok & (sblk == j), kf4, 0.0))
        for j in range(1, nsub):
            qj = jnp.where(row < n * j, q * jnp.exp(jnp.minimum(f_rows[j] - cbx, 0.0)), 0.0)
            lhs_parts.append(qj)
            rhs_parts.append(jnp.where(head_ok & (sblk == j), kb4, 0.0))
        att = _dot_nt(jnp.concatenate(lhs_parts, axis=1).astype(BF16),
                      jnp.concatenate(rhs_parts, axis=1).astype(BF16))
        v4 = jnp.where(vhead_ok, jnp.concatenate([v] * B_HEADS, axis=0), 0.0).astype(BF16)
        o = _dot(att.astype(BF16), v4)

        diag = []
        for i in range(nsub):
            lo_, hi_ = n * i, n * (i + 1)
            qi, ki, vi = q[lo_:hi_], k[lo_:hi_], v[lo_:hi_]
            bfi, cbi = bf2[lo_:hi_], cb2[lo_:hi_]
            slabs = []
            for s in range(n):
                halves = []
                for r0 in range(0, n, 8):
                    fwd = bfi[r0:r0 + 8] - bfi[s:s + 1]
                    bwd = cbi[s:s + 1] - cbi[r0:r0 + 8]
                    if s <= r0:
                        halves.append(fwd)
                    elif s >= r0 + 8:
                        halves.append(bwd)
                    else:
                        halves.append(jnp.where(row_8 >= s - r0, fwd, bwd))
                arg = jnp.concatenate(halves, axis=0)
                slabs.append(((qi * ki[s:s + 1]) * jnp.exp2(arg)).astype(BF16))
            zz = _dot(jnp.concatenate(slabs, axis=0), ebf_ref[...])
            acc = zz[0:n] * vi[0:1]
            for s in range(1, n):
                acc = acc + zz[n * s:n * (s + 1)] * vi[s:s + 1]
            diag.append(acc)
        return o + jnp.concatenate(diag, axis=0)

    def step(it, carry):
        rows = pl.ds(pl.multiple_of(it * span, span), span)
        q = q_ref[rows, :]
        k = k_ref[rows, :]
        v = v_ref[rows, :]
        la = _gla_log_decay(g_ref[rows, :], gup_ref, gbias_ref)
        cums = [_cumsum_rows(la[c * j:c * (j + 1)], tri3_ref) for j in range(grp)]
        cum = jnp.concatenate(cums, axis=0)
        bf = cum[:, :B_QK]
        cbx = cum[:, B_QK:] - la[:, B_QK:]
        tot_f = [cj[c - 1:c, :B_QK] for cj in cums]
        tot_b = [cj[c - 1:c, B_QK:] for cj in cums]
        bf2 = bf * LOG2E
        cb2 = cbx * LOG2E
        e_blk = rows_of([bf[n * j + n - 1:n * j + n, :] for j in range(span // n)], n)
        f_blk = rows_of([cbx[n * j:n * j + 1, :] for j in range(span // n)], n)
        kf = k * jnp.exp(e_blk - bf)
        kb = k * jnp.exp(cbx - f_blk)
        kt = (k * jnp.exp(rows_of(tot_f, c) - bf)).astype(BF16)
        lhs_inter = jnp.concatenate([q * jnp.exp(bf), q * jnp.exp(rows_of(tot_b, c) - cbx)],
                                    axis=1).astype(BF16)
        vb = v.astype(BF16)

        outs = []
        upd = []
        for j in range(grp):
            sl = slice(c * j, c * (j + 1))
            outs.append(chunk_local(q[sl], k[sl], v[sl], bf[sl], cbx[sl], bf2[sl], cb2[sl],
                                    kf[sl], kb[sl]))
            upd.append(_dot_tn(vb[sl], kt[sl]))

        st = state[...]
        for j in range(grp):
            sl = slice(c * j, c * (j + 1))
            rhs = jnp.concatenate([st.astype(BF16), sb_ref[it * grp + j]], axis=1)
            outs[j] = outs[j] + _dot_nt(lhs_inter[sl], rhs)
            st = st * jnp.exp(tot_f[j]) + upd[j] * emask_ref[...]
        state[...] = st

        gate = r_ref[rows, :]
        o = jnp.concatenate(outs, axis=0)
        out = _head_norm(o, ones_ref, gain_ref[...]) * (gate * _sigmoid(gate))
        o_ref[rows, :] = out.astype(o_ref.dtype)
        return carry

    lax.fori_loop(0, nit, step, 0)


def _gla(z3, gup_bd, gbias, tri3, emask, ebf, ones256, gain, tb):
    b, seq, _ = z3.shape
    nblk = seq // tb
    nch = tb // GLA_CHUNK
    ntot = seq // GLA_CHUNK

    def zspec(col, width, rev):
        blk = col // width
        if rev:
            return pl.BlockSpec((None, tb, width), lambda bi, i: (bi, nblk - 1 - i, blk))
        return pl.BlockSpec((None, tb, width), lambda bi, i: (bi, i, blk))

    consts = [gup_bd, gbias, tri3, emask]
    const_specs = [_const_spec(a.shape) for a in consts]
    nit = nch // GLA_GROUP
    sb = pl.pallas_call(
        functools.partial(_gla_bwd_body, nit),
        grid=(b, nblk),
        in_specs=[zspec(Z_BK, B_QK, True), zspec(Z_BV, B_WIDTH, True), zspec(Z_BG, LANES, True)]
        + const_specs,
        out_specs=pl.BlockSpec((None, nch, B_WIDTH, B_QK), lambda bi, i: (bi, nblk - 1 - i, 0, 0)),
        out_shape=jax.ShapeDtypeStruct((b, ntot, B_WIDTH, B_QK), BF16),
        scratch_shapes=[pltpu.VMEM((B_WIDTH, B_QK), F32)],
        compiler_params=_cparams(("parallel", "arbitrary")),
        name="gla_bwd",
    )(z3, z3, z3, *consts)

    consts2 = [gup_bd, gbias, tri3, emask, ebf, ones256, gain]
    return pl.pallas_call(
        functools.partial(_gla_main_body, nit),
        grid=(b, nblk),
        in_specs=[zspec(Z_BQ, B_QK, False), zspec(Z_BK, B_QK, False), zspec(Z_BV, B_WIDTH, False),
                  zspec(Z_BR, B_WIDTH, False), zspec(Z_BG, LANES, False),
                  pl.BlockSpec((None, nch, B_WIDTH, B_QK), lambda bi, i: (bi, i, 0, 0))]
        + [_const_spec(a.shape) for a in consts2],
        out_specs=pl.BlockSpec((None, tb, B_WIDTH), lambda bi, i: (bi, i, 0)),
        out_shape=jax.ShapeDtypeStruct((b, seq, B_WIDTH), BF16),
        scratch_shapes=[pltpu.VMEM((B_WIDTH, B_QK), F32)],
        compiler_params=_cparams(("parallel", "arbitrary")),
        name="gla_main",
    )(z3, z3, z3, z3, z3, sb, *consts2)


def _ret_bwd_body(nch, k_ref, v_ref, raw_ref, bmask_ref, rb_ref, state):
    @pl.when(pl.program_id(1) == 0)
    def _():
        state[...] = jnp.zeros_like(state)

    c = RET_CHUNK
    lg1 = _log_sigmoid(raw_ref[1:2, :])
    pos = _iota((c, C_WIDTH), 0).astype(F32)
    kdec = jnp.exp(pos * lg1)
    chunk_decay = jnp.exp(float(c) * lg1)

    upd = []
    for j in range(nch):
        rows = pl.ds(c * j, c)
        kt = (k_ref[rows, :] * kdec).astype(BF16)
        upd.append(_dot_tn(kt, v_ref[rows, :].astype(BF16)))
    st = state[...]
    for j in reversed(range(nch)):
        rb_ref[j] = st.astype(BF16)
        st = chunk_decay * st + upd[j] * bmask_ref[...]
    state[...] = st


def _ret_main_body(nch, q_ref, k_ref, v_ref, g_ref, rb_ref, raw_ref, raw_s_ref, bmask_ref, ones_ref,
                   gain_ref, o_ref, state):
    @pl.when(pl.program_id(1) == 0)
    def _():
        state[...] = jnp.zeros_like(state)

    c = RET_CHUNK
    lg = _log_sigmoid(raw_ref[...])
    lg0, lg1 = lg[0:1, :], lg[1:2, :]
    lgs = _log_sigmoid(raw_s_ref[...])
    pos = _iota((c, C_WIDTH), 0).astype(F32)
    qdec_f = jnp.exp((pos + 1.0) * lg0)
    qdec_b = jnp.exp((float(c) - pos) * lg1)
    kdec_f = jnp.exp((float(c) - 1.0 - pos) * lg0)
    chunk_decay = jnp.exp(float(c) * lg0)
    t_idx = _iota((c, C_HEADS * c), 0)
    s_idx = _iota((c, C_HEADS * c), 1) & (c - 1)
    rel = (t_idx - s_idx).astype(F32)
    dmat = jnp.where(rel >= 0.0, jnp.exp(jnp.maximum(rel, 0.0) * lgs[0:1, :]),
                     jnp.exp(jnp.maximum(-rel, 0.0) * lgs[1:2, :]))
    rho = _iota((C_HEADS * c, C_WIDTH), 0)
    lane = _iota((C_HEADS * c, C_WIDTH), 1)
    head_ok = (rho >> 7) == (lane >> 6)

    outs, lhs_inter, upd = [], [], []
    for j in range(nch):
        rows = pl.ds(c * j, c)
        q = q_ref[rows, :]
        k = k_ref[rows, :]
        v = v_ref[rows, :]
        k4 = jnp.where(head_ok, jnp.concatenate([k] * C_HEADS, axis=0), 0.0).astype(BF16)
        v4 = jnp.where(head_ok, jnp.concatenate([v] * C_HEADS, axis=0), 0.0).astype(BF16)
        scores = _dot_nt(q.astype(BF16), k4) * dmat
        outs.append(_dot(scores.astype(BF16), v4))
        lhs_inter.append(jnp.concatenate([q * qdec_f, q * qdec_b], axis=1).astype(BF16))
        upd.append(_dot_tn((k * kdec_f).astype(BF16), v.astype(BF16)))
    st = state[...]
    for j in range(nch):
        rhs = jnp.concatenate([st.astype(BF16), rb_ref[j]], axis=0)
        outs[j] = outs[j] + _dot(lhs_inter[j], rhs)
        st = chunk_decay * st + upd[j] * bmask_ref[...]
    state[...] = st
    for j in range(nch):
        rows = pl.ds(c * j, c)
        gate = g_ref[rows, :]
        out = _head_norm(outs[j], ones_ref, gain_ref[...]) * (gate * _sigmoid(gate))
        o_ref[rows, :] = out.astype(o_ref.dtype)


def _ret(z3, raw256, raw512, ones256, gain, tb):
    b, seq, _ = z3.shape
    nblk = seq // tb
    nch = tb // RET_CHUNK
    ntot = seq // RET_CHUNK

    def zspec(col, rev):
        blk = col // C_WIDTH
        if rev:
            return pl.BlockSpec((None, tb, C_WIDTH), lambda bi, i: (bi, nblk - 1 - i, blk))
        return pl.BlockSpec((None, tb, C_WIDTH), lambda bi, i: (bi, i, blk))

    rb = pl.pallas_call(
        functools.partial(_ret_bwd_body, nch),
        grid=(b, nblk),
        in_specs=[zspec(Z_CK, True), zspec(Z_CV, True), _const_spec(raw256.shape),
                  _const_spec(ones256.shape)],
        out_specs=pl.BlockSpec((None, nch, C_WIDTH, C_WIDTH), lambda bi, i: (bi, nblk - 1 - i, 0, 0)),
        out_shape=jax.ShapeDtypeStruct((b, ntot, C_WIDTH, C_WIDTH), BF16),
        scratch_shapes=[pltpu.VMEM((C_WIDTH, C_WIDTH), F32)],
        compiler_params=_cparams(("parallel", "arbitrary")),
        name="ret_bwd",
    )(z3, z3, raw256, ones256)

    consts = [raw256, raw512, ones256, ones256, gain]
    return pl.pallas_call(
        functools.partial(_ret_main_body, nch),
        grid=(b, nblk),
        in_specs=[zspec(Z_CQ, False), zspec(Z_CK, False), zspec(Z_CV, False), zspec(Z_CG, False),
                  pl.BlockSpec((None, nch, C_WIDTH, C_WIDTH), lambda bi, i: (bi, i, 0, 0))]
        + [_const_spec(a.shape) for a in consts],
        out_specs=pl.BlockSpec((None, tb, C_WIDTH), lambda bi, i: (bi, i, 0)),
        out_shape=jax.ShapeDtypeStruct((b, seq, C_WIDTH), BF16),
        scratch_shapes=[pltpu.VMEM((C_WIDTH, C_WIDTH), F32)],
        compiler_params=_cparams(("parallel", "arbitrary")),
        name="ret_main",
    )(z3, z3, z3, z3, rb, *consts)


def _post_body(h_ref, oa_ref, ob_ref, oc_ref, p_ref, wo_ref, lm_ref, w1_ref, w2_ref, lp_ref,
               wg_ref, wp_ref, y_ref):
    def rms(x, g_ref):
        ms = jnp.mean(x * x, axis=-1, keepdims=True)
        return (x * lax.rsqrt(ms + EPS) * g_ref[...]).astype(BF16)

    h = h_ref[...]
    h = h + (_dot(oa_ref[...], wo_ref[0:A_WIDTH, :])
             + _dot(ob_ref[...], wo_ref[A_WIDTH:A_WIDTH + B_WIDTH, :])
             + _dot(oc_ref[...], wo_ref[A_WIDTH + B_WIDTH:, :]))
    m = rms(h, lm_ref)
    ff = D_FF // 4
    mlp = None
    for j in range(4):
        hid = _dot(m, w1_ref[:, ff * j:ff * (j + 1)])
        hid = jnp.square(jnp.maximum(hid, 0.0)).astype(BF16)
        part = _dot(hid, w2_ref[ff * j:ff * (j + 1), :])
        mlp = part if mlp is None else mlp + part
    h = h + mlp
    gate = _sigmoid(_dot(rms(h, lp_ref), wg_ref[...]))
    y_ref[...] = h + gate * _dot(p_ref[...].astype(BF16), wp_ref[...])


def _post(h2, oa, ob, oc, ple, wo, lm, w1, w2, lp, wg, wp, tm):
    n = h2.shape[0]

    def tok(width):
        return pl.BlockSpec((tm, width), lambda i: (i, 0))

    consts = [wo, lm, w1, w2, lp, wg, wp]
    return pl.pallas_call(
        _post_body,
        grid=(n // tm,),
        in_specs=[tok(D_MODEL), tok(A_WIDTH), tok(B_WIDTH), tok(C_WIDTH), tok(PLE_DIM)]
        + [_const_spec(a.shape) for a in consts],
        out_specs=tok(D_MODEL),
        out_shape=jax.ShapeDtypeStruct((n, D_MODEL), F32),
        compiler_params=_cparams(("parallel",)),
        name="post",
    )(h2, oa, ob, oc, ple, *consts)


def _rope_tables(seq, rot_dim, theta):
    half = rot_dim // 2
    inv_freq = 1.0 / (theta ** (jnp.arange(half, dtype=F32) * (2.0 / rot_dim)))
    ang = jnp.arange(seq, dtype=F32)[:, None] * inv_freq[None, :]
    cos, sin = jnp.cos(ang), jnp.sin(ang)
    pad = HEAD_DIM - rot_dim
    c = jnp.concatenate([cos, cos, jnp.ones((seq, pad), F32)], axis=1)
    sn = jnp.concatenate([-sin, jnp.zeros((seq, half + pad), F32)], axis=1)
    sp = jnp.concatenate([jnp.zeros((seq, half), F32), sin, jnp.zeros((seq, pad), F32)], axis=1)
    return jnp.concatenate([jnp.tile(t, (1, LANES // HEAD_DIM)) for t in (c, sn, sp)], axis=1)


def _block_ones(rows, cols, rblk, cblk):
    r = jnp.arange(rows)[:, None] // rblk
    c = jnp.arange(cols)[None, :] // cblk
    return r == c


def _layer_consts(i, ln_mix, w_in, attn_q_norm, attn_k_norm, gla_gate_up, gla_gate_bias, gla_out_norm,
                  ret_decay_raw, ret_out_norm, w_out, ln_mlp, w_mlp_in, w_mlp_out, ln_pe, w_pe_gate,
                  w_pe_proj):
    w = w_in[i]
    w_p = jnp.concatenate([w[:, :2304], w[:, 2336:N_IN], w[:, 2304:2336],
                           jnp.zeros((D_MODEL, Z_WIDTH - N_IN), F32)], axis=1).astype(BF16)
    gup = gla_gate_up[i].astype(BF16)
    gup_bd = jnp.zeros((LANES, 2 * B_QK), BF16)
    gup_bd = gup_bd.at[0:B_GATE_RANK, 0:B_QK].set(gup[0])
    gup_bd = gup_bd.at[B_GATE_RANK:2 * B_GATE_RANK, B_QK:].set(gup[1])
    return dict(
        ln_mix=ln_mix[i][None, :], w_p=w_p,
        qg=jnp.tile(attn_q_norm[i], 4)[None, :], kg=jnp.tile(attn_k_norm[i], 4)[None, :],
        gup_bd=gup_bd, gbias=gla_gate_bias[i].reshape(1, 2 * B_QK),
        gla_gain=gla_out_norm[i][None, :],
        raw256=jnp.repeat(ret_decay_raw[i], C_DIM, axis=1),
        raw512=jnp.repeat(ret_decay_raw[i], RET_CHUNK, axis=1),
        ret_gain=ret_out_norm[i][None, :],
        wo=w_out[i].astype(BF16), lm=ln_mlp[i][None, :], w1=w_mlp_in[i].astype(BF16),
        w2=w_mlp_out[i].astype(BF16), lp=ln_pe[i][None, :], wg=w_pe_gate[i].astype(BF16),
        wp=w_pe_proj[i].astype(BF16),
    )


def _shared_consts():
    tri = jnp.arange(GLA_CHUNK)[:, None] >= jnp.arange(GLA_CHUNK)[None, :]
    return dict(
        ones256=_block_ones(256, 256, HEAD_DIM, HEAD_DIM).astype(BF16),
        tri3=jnp.tile(tri, (1, 3)).astype(BF16),
        emask=_block_ones(B_WIDTH, B_QK, HEAD_DIM, B_KDIM).astype(F32),
        ebf=_block_ones(B_QK, B_WIDTH, B_KDIM, HEAD_DIM).astype(BF16),
    )


def _run_trunk(x, p, layers, shared):
    b, seq, _ = x.shape
    n = b * seq
    tm = 512
    tb = 512
    rope_a = _rope_tables(seq, ROPE_DIM, ROPE_THETA)
    rope_c = _rope_tables(seq, C_DIM, RET_THETA)
    h = x.reshape(n, D_MODEL)
    for i, lc in enumerate(layers):
        z = _proj_in(h, seq, lc["ln_mix"], lc["w_p"], shared["ones256"], lc["qg"], lc["kg"],
                     rope_a, rope_c, tm)
        z3 = z.reshape(b, seq, Z_WIDTH)
        oa = _attn(z3)
        ob = _gla(z3, lc["gup_bd"], lc["gbias"], shared["tri3"], shared["emask"], shared["ebf"],
                  shared["ones256"], lc["gla_gain"], tb)
        oc = _ret(z3, lc["raw256"], lc["raw512"], shared["ones256"], lc["ret_gain"], tb)
        h = _post(h, oa.reshape(n, A_WIDTH), ob.reshape(n, B_WIDTH), oc.reshape(n, C_WIDTH),
                  p[i].reshape(n, PLE_DIM), lc["wo"], lc["lm"], lc["w1"], lc["w2"], lc["lp"],
                  lc["wg"], lc["wp"], tm)
    return h.reshape(b, seq, D_MODEL)


def kernel(x_prompt, x_sample, p_prompt, p_sample, ln_mix, w_in, attn_q_norm, attn_k_norm, gla_gate_up, gla_gate_bias, gla_out_norm, ret_decay_raw, ret_out_norm, w_out, ln_mlp, w_mlp_in, w_mlp_out, ln_pe, w_pe_gate, w_pe_proj):
    depth = w_in.shape[0]
    layers = [_layer_consts(i, ln_mix, w_in, attn_q_norm, attn_k_norm, gla_gate_up, gla_gate_bias,
                            gla_out_norm, ret_decay_raw, ret_out_norm, w_out, ln_mlp, w_mlp_in,
                            w_mlp_out, ln_pe, w_pe_gate, w_pe_proj) for i in range(depth)]
    shared = _shared_consts()
    y_prompt = _run_trunk(x_prompt, p_prompt, layers, shared)
    y_sample = _run_trunk(x_sample, p_sample, layers, shared)
    return (y_prompt, y_sample)
```

```python
import functools

import jax
import jax.numpy as jnp
from jax import lax
from jax.experimental import pallas as pl
from jax.experimental.pallas import tpu as pltpu

F32 = jnp.float32
BF16 = jnp.bfloat16

D_MODEL = 1024
HEAD_DIM = 64
A_HEADS = 8
A_WIDTH = 512
A_PATTERNS = ((128, 1), (512, 4), (2048, 16))
A_RADIUS = 64
A_MAX_REACH = 1024
ROPE_THETA = 500000.0
ROPE_DIM = 16
B_HEADS = 4
B_KDIM = 32
B_QK = 128
B_WIDTH = 256
B_GATE_RANK = 16
B_GATE_TAU = 16.0
C_HEADS = 4
C_DIM = 64
C_WIDTH = 256
RET_THETA = 10000.0
N_IN = 3360
D_FF = 4096
PLE_DIM = 256
EPS = 1e-6
NEG = -1e30
LOG2E = 1.4426950408889634

LANES = 128
VMEM_LIMIT = 56 * 1024 * 1024

Z_AQ, Z_AK, Z_AV = 0, 512, 1024
Z_BQ, Z_BK, Z_BV, Z_BR = 1536, 1664, 1792, 2048
Z_CQ, Z_CK, Z_CV, Z_CG = 2304, 2560, 2816, 3072
Z_BG = 3328
Z_WIDTH = 3456

GLA_CHUNK = 64
GLA_SUB = 16
GLA_GROUP = 4
RET_CHUNK = 128
ATT_TILE = 2048
ATT_SUB = 128
ATT_KEYS = ATT_SUB + 2 * A_RADIUS
ATT_SKEW = 2


def _cparams(sem):
    return pltpu.CompilerParams(dimension_semantics=sem, vmem_limit_bytes=VMEM_LIMIT)


def _const_spec(shape):
    nd = len(shape)
    return pl.BlockSpec(shape, lambda *_: (0,) * nd, pipeline_mode=pl.Buffered(1))


def _sigmoid(x):
    return 1.0 / (1.0 + jnp.exp(-x))


def _log_sigmoid(x):
    return jnp.minimum(x, 0.0) - jnp.log1p(jnp.exp(-jnp.abs(x)))


def _iota(shape, dim):
    return lax.broadcasted_iota(jnp.int32, shape, dim)


def _dot(a, b):
    return jnp.dot(a, b, preferred_element_type=F32)


def _dot_nt(a, b):
    return lax.dot_general(a, b, (((1,), (1,)), ((), ())), preferred_element_type=F32)


def _dot_tn(a, b):
    return lax.dot_general(a, b, (((0,), (0,)), ((), ())), preferred_element_type=F32)


def _head_norm(o, ones_ref, gain):
    ssum = _dot((o * o).astype(BF16), ones_ref[...])
    return o * lax.rsqrt(ssum * (1.0 / HEAD_DIM) + EPS) * gain


def _rope(y, tab_ref, shift):
    c = tab_ref[:, 0:LANES]
    sn = tab_ref[:, LANES:2 * LANES]
    sp = tab_ref[:, 2 * LANES:3 * LANES]
    outs = []
    for j in range(y.shape[1] // LANES):
        yj = y[:, j * LANES:(j + 1) * LANES]
        outs.append(yj * c + pltpu.roll(yj, LANES - shift, 1) * sn + pltpu.roll(yj, shift, 1) * sp)
    return jnp.concatenate(outs, axis=1)


def _proj_in_body(x_ref, g_ref, w_ref, ones_ref, qg_ref, kg_ref, ra_ref, rc_ref, z_ref):
    x = x_ref[...]
    ms = jnp.mean(x * x, axis=-1, keepdims=True)
    u = (x * lax.rsqrt(ms + EPS) * g_ref[...]).astype(BF16)

    def proj(a, b):
        return _dot(u, w_ref[:, a:b])

    def qk_norm(y, gain_ref):
        halves = []
        for j in range(2):
            yj = y[:, 256 * j:256 * (j + 1)]
            halves.append(_head_norm(yj, ones_ref, gain_ref[...]))
        return jnp.concatenate(halves, axis=1)

    aq = _rope(qk_norm(proj(Z_AQ, Z_AK), qg_ref), ra_ref, ROPE_DIM // 2)
    z_ref[:, Z_AQ:Z_AK] = aq * (HEAD_DIM ** -0.5 * LOG2E)
    z_ref[:, Z_AK:Z_AV] = _rope(qk_norm(proj(Z_AK, Z_AV), kg_ref), ra_ref, ROPE_DIM // 2)
    z_ref[:, Z_AV:Z_BQ] = proj(Z_AV, Z_BQ)
    z_ref[:, Z_BQ:Z_BK] = proj(Z_BQ, Z_BK) * (B_KDIM ** -0.5)
    z_ref[:, Z_BK:Z_CQ] = proj(Z_BK, Z_CQ)
    z_ref[:, Z_CQ:Z_CK] = _rope(proj(Z_CQ, Z_CK), rc_ref, C_DIM // 2)
    z_ref[:, Z_CK:Z_CV] = _rope(proj(Z_CK, Z_CV), rc_ref, C_DIM // 2) * (C_DIM ** -0.5)
    z_ref[:, Z_CV:Z_WIDTH] = proj(Z_CV, Z_WIDTH)


def _proj_in(x2, seq, ln, w_p, ones256, qg, kg, rope_a, rope_c, tm):
    n = x2.shape[0]
    per_seq = seq // tm
    return pl.pallas_call(
        _proj_in_body,
        grid=(n // tm,),
        in_specs=[
            pl.BlockSpec((tm, D_MODEL), lambda i: (i, 0)),
            _const_spec((1, D_MODEL)),
            _const_spec((D_MODEL, Z_WIDTH)),
            _const_spec((256, 256)),
            _const_spec((1, 256)),
            _const_spec((1, 256)),
            pl.BlockSpec((tm, 3 * LANES), lambda i: (i % per_seq, 0)),
            pl.BlockSpec((tm, 3 * LANES), lambda i: (i % per_seq, 0)),
        ],
        out_specs=pl.BlockSpec((tm, Z_WIDTH), lambda i: (i, 0)),
        out_shape=jax.ShapeDtypeStruct((n, Z_WIDTH), F32),
        compiler_params=_cparams(("parallel",)),
        name="proj_in",
    )(x2, ln, w_p, ones256, qg, kg, rope_a, rope_c)


def _attn_body(seq, q_ref, kp_ref, kc_ref, kn_ref, vp_ref, vc_ref, vn_ref, o_ref,
               kbuf, vbuf, m_ref, l_ref, acc_ref, band_ref):
    t = ATT_TILE
    halo = A_MAX_REACH
    sub = ATT_SUB
    tile_start = pl.program_id(2) * t
    kbuf[0:halo, :] = kp_ref[...]
    kbuf[halo:halo + t, :] = kc_ref[...]
    kbuf[halo + t:, :] = kn_ref[...]
    vbuf[0:halo, :] = vp_ref[...]
    vbuf[halo:halo + t, :] = vc_ref[...]
    vbuf[halo + t:, :] = vn_ref[...]

    low_half = _iota((sub, LANES), 1) < HEAD_DIM
    a_idx = _iota((2 * sub, ATT_KEYS), 0) & (sub - 1)
    rel = _iota((2 * sub, ATT_KEYS), 1) - a_idx
    band_ref[...] = jnp.where((rel >= 0) & (rel <= 2 * A_RADIUS), 0.0, NEG)
    c_row = _iota((1, ATT_KEYS), 1)
    ones_v = jnp.ones((ATT_KEYS, LANES), BF16)

    def scores_stage(tile):
        pat, qs, dil = tile
        span = A_RADIUS * dil
        q_sub = q_ref[pl.ds(qs, sub, stride=dil), :]
        q_st = jnp.concatenate([jnp.where(low_half, q_sub, 0.0),
                                jnp.where(low_half, 0.0, q_sub)], axis=0).astype(BF16)
        k_sub = kbuf[pl.ds(halo + qs - span, ATT_KEYS, stride=dil), :].astype(BF16)
        s = _dot_nt(q_st, k_sub)
        key_pos = (tile_start + qs - span) + dil * c_row
        bias = jnp.where((key_pos >= 0) & (key_pos < seq), 0.0, NEG)
        s = (s + band_ref[...]) + bias
        return s, jnp.broadcast_to(jnp.max(s, axis=1, keepdims=True), (2 * sub, LANES))

    def values_stage(tile, s, m_row):
        pat, qs, dil = tile
        p = jnp.exp2(s - jnp.concatenate([m_row] * (ATT_KEYS // LANES), axis=1)).astype(BF16)
        v_sub = vbuf[pl.ds(halo + qs - A_RADIUS * dil, ATT_KEYS, stride=dil), :].astype(BF16)
        return _dot(p, jnp.concatenate([v_sub, ones_v], axis=1))

    def store_stage(tile, m_row, pvl):
        pat, qs, dil = tile
        rw = pl.ds(qs, sub, stride=dil)
        m_ref[pat, rw, :] = jnp.where(low_half, m_row[0:sub], m_row[sub:])
        l_ref[pat, rw, :] = jnp.where(low_half, pvl[0:sub, LANES:], pvl[sub:, LANES:])
        acc_ref[pat, rw, :] = jnp.where(low_half, pvl[0:sub, 0:LANES], pvl[sub:, 0:LANES])

    tiles = [(pat, r + j * sub * dil, dil)
             for pat, (_, dil) in enumerate(A_PATTERNS)
             for r in range(dil) for j in range(t // (sub * dil))]
    scored, valued = {}, {}
    for step in range(len(tiles) + 2 * ATT_SKEW):
        i_store, i_val = step - 2 * ATT_SKEW, step - ATT_SKEW
        if 0 <= i_store < len(tiles):
            store_stage(tiles[i_store], *valued.pop(i_store))
        if 0 <= i_val < len(tiles):
            s, m_row = scored.pop(i_val)
            valued[i_val] = (m_row, values_stage(tiles[i_val], s, m_row))
        if step < len(tiles):
            scored[step] = scores_stage(tiles[step])

    npat = len(A_PATTERNS)
    blk = 2 * sub
    for i in range(t // blk):
        rs = pl.ds(i * blk, blk)
        m_g = [m_ref[g, rs, :] for g in range(npat)]
        m_all = functools.reduce(jnp.maximum, m_g)
        w_g = [jnp.exp2(m - m_all) for m in m_g]
        l_all = functools.reduce(lambda a, b: a + b, [w * l_ref[g, rs, :] for g, w in enumerate(w_g)])
        num = functools.reduce(lambda a, b: a + b, [w * acc_ref[g, rs, :] for g, w in enumerate(w_g)])
        o_ref[rs, :] = (num / l_all).astype(o_ref.dtype)


def _attn(z3):
    b, seq, _ = z3.shape
    t = ATT_TILE
    halo = A_MAX_REACH
    nt = seq // t
    per_tile = t // halo
    n_halo = seq // halo
    qc, kc, vc = Z_AQ // LANES, Z_AK // LANES, Z_AV // LANES

    def cur(c0):
        return pl.BlockSpec((None, t, LANES), lambda bi, hp, i: (bi, i, c0 + hp))

    def prev(c0):
        return pl.BlockSpec((None, halo, LANES),
                            lambda bi, hp, i: (bi, jnp.maximum(i * per_tile - 1, 0), c0 + hp))

    def nxt(c0):
        return pl.BlockSpec((None, halo, LANES),
                            lambda bi, hp, i: (bi, jnp.minimum((i + 1) * per_tile, n_halo - 1), c0 + hp))

    return pl.pallas_call(
        functools.partial(_attn_body, seq),
        grid=(b, A_HEADS // 2, nt),
        in_specs=[cur(qc), prev(kc), cur(kc), nxt(kc), prev(vc), cur(vc), nxt(vc)],
        out_specs=pl.BlockSpec((None, t, LANES), lambda bi, hp, i: (bi, i, hp)),
        out_shape=jax.ShapeDtypeStruct((b, seq, A_WIDTH), BF16),
        scratch_shapes=[
            pltpu.VMEM((t + 2 * halo, LANES), F32),
            pltpu.VMEM((t + 2 * halo, LANES), F32),
            pltpu.VMEM((len(A_PATTERNS), t, LANES), F32),
            pltpu.VMEM((len(A_PATTERNS), t, LANES), F32),
            pltpu.VMEM((len(A_PATTERNS), t, LANES), F32),
            pltpu.VMEM((2 * ATT_SUB, ATT_KEYS), F32),
        ],
        compiler_params=_cparams(("parallel", "parallel", "parallel")),
        name="attn",
    )(z3, z3, z3, z3, z3, z3, z3)


def _split3(x):
    hi = x.astype(BF16)
    r1 = x - hi.astype(F32)
    mid = r1.astype(BF16)
    lo = (r1 - mid.astype(F32)).astype(BF16)
    return hi, mid, lo


def _gla_log_decay(g, gup_ref, gbias_ref):
    logits = _dot(g.astype(BF16), gup_ref[...]) + gbias_ref[...]
    return _log_sigmoid(logits) * (1.0 / B_GATE_TAU)


def _cumsum_rows(la, tri3_ref):
    hi, mid, lo = _split3(la)
    return _dot(tri3_ref[...], jnp.concatenate([hi, mid, lo], axis=0))


def _gla_bwd_body(nit, k_ref, v_ref, g_ref, gup_ref, gbias_ref, tri3_ref, emask_ref, sb_ref, state):
    @pl.when(pl.program_id(1) == 0)
    def _():
        state[...] = jnp.zeros_like(state)

    c = GLA_CHUNK
    grp = GLA_GROUP
    span = grp * c

    def step(i, carry):
        it = nit - 1 - i
        rows = pl.ds(pl.multiple_of(it * span, span), span)
        logits = _dot(g_ref[rows, :].astype(BF16), gup_ref[:, B_QK:]) + gbias_ref[:, B_QK:]
        la = _log_sigmoid(logits) * (1.0 / B_GATE_TAU)
        cums = [_cumsum_rows(la[c * j:c * (j + 1)], tri3_ref) for j in range(grp)]
        cbx = jnp.concatenate(cums, axis=0) - la
        kt = (k_ref[rows, :] * jnp.exp(cbx)).astype(BF16)
        vb = v_ref[rows, :].astype(BF16)
        upd = [_dot_tn(vb[c * j:c * (j + 1)], kt[c * j:c * (j + 1)]) for j in range(grp)]
        st = state[...]
        for j in reversed(range(grp)):
            sb_ref[it * grp + j] = st.astype(BF16)
            st = st * jnp.exp(cums[j][c - 1:c, :]) + upd[j] * emask_ref[...]
        state[...] = st
        return carry

    lax.fori_loop(0, nit, step, 0)


def _gla_main_body(nit, q_ref, k_ref, v_ref, r_ref, g_ref, sb_ref, gup_ref, gbias_ref, tri3_ref,
                   emask_ref, estack_ref, ones_ref, gain_ref, o_ref, state):
    @pl.when(pl.program_id(1) == 0)
    def _():
        state[...] = jnp.zeros_like(state)

    c = GLA_CHUNK
    n = GLA_SUB
    nsub = c // n
    grp = GLA_GROUP
    span = grp * c
    row = _iota((c, B_QK), 0)
    row_8 = _iota((8, B_QK), 0)
    rho = _iota((B_HEADS * n, B_QK), 0)
    head_k_ok = (rho >> 4) == (_iota((B_HEADS * n, B_QK), 1) >> 5)
    rho_v = _iota((B_HEADS * n, B_WIDTH), 0)
    head_v_ok = (rho_v >> 4) == (_iota((B_HEADS * n, B_WIDTH), 1) >> 6)
    own_blk = ((_iota((span, B_HEADS * c), 0) & (c - 1)) >> 4) == (_iota((span, B_HEADS * c), 1) >> 6)
    zeros_k = jnp.zeros((B_HEADS * n, B_QK), F32)

    def rows_of(vals, height):
        return jnp.concatenate([jnp.broadcast_to(x, (height, x.shape[1])) for x in vals], axis=0)

    def by_head(x, ok):
        return jnp.where(ok, jnp.concatenate([x] * B_HEADS, axis=0), 0.0)

    def diag_lhs(q, k, bf2, cb2):
        nblk = span // n
        slabs = []
        for s in range(n):
            ks = rows_of([k[n * i + s:n * i + s + 1] for i in range(nblk)], n)
            pieces = []
            for i in range(nblk):
                ref_f = bf2[n * i + s:n * i + s + 1]
                ref_b = cb2[n * i + s:n * i + s + 1]
                for r0 in range(0, n, 8):
                    rs = slice(n * i + r0, n * i + r0 + 8)
                    if s <= r0:
                        pieces.append(bf2[rs] - ref_f)
                    elif s >= r0 + 8:
                        pieces.append(ref_b - cb2[rs])
                    else:
                        pieces.append(jnp.where(row_8 >= s - r0, bf2[rs] - ref_f, ref_b - cb2[rs]))
            arg = jnp.concatenate(pieces, axis=0)
            slabs.append(((q * ks) * jnp.exp2(arg)).astype(BF16))
        return jnp.concatenate(slabs, axis=1)

    def chunk_local(q, k, v, bf, cbx, kf, kb, att_diag):
        e_rows = [bf[n * j + n - 1:n * j + n, :] for j in range(nsub)]
        f_rows = [cbx[n * j:n * j + 1, :] for j in range(nsub)]
        lhs_parts = []
        for j in range(nsub - 1):
            lhs_parts.append(jnp.where(row >= n * (j + 1),
                                       q * jnp.exp(jnp.minimum(bf - e_rows[j], 0.0)), 0.0))
        for j in range(1, nsub):
            lhs_parts.append(jnp.where(row < n * j, q * jnp.exp(jnp.minimum(f_rows[j] - cbx, 0.0)), 0.0))
        rhs_rows = []
        for j in range(nsub):
            kfj = by_head(kf[n * j:n * (j + 1)], head_k_ok)
            kbj = by_head(kb[n * j:n * (j + 1)], head_k_ok)
            parts = [kfj if (jj == j and j < nsub - 1) else zeros_k for jj in range(nsub - 1)]
            parts += [kbj if (jj == j and j > 0) else zeros_k for jj in range(1, nsub)]
            rhs_rows.append(jnp.concatenate(parts, axis=1))
        att = _dot_nt(jnp.concatenate(lhs_parts, axis=1).astype(BF16),
                      jnp.concatenate(rhs_rows, axis=0).astype(BF16))
        v4 = jnp.concatenate([by_head(v[n * j:n * (j + 1)], head_v_ok) for j in range(nsub)],
                             axis=0).astype(BF16)
        return _dot((att + att_diag).astype(BF16), v4)

    def step(it, carry):
        rows = pl.ds(pl.multiple_of(it * span, span), span)
        q = q_ref[rows, :]
        k = k_ref[rows, :]
        v = v_ref[rows, :]
        la = _gla_log_decay(g_ref[rows, :], gup_ref, gbias_ref)
        cums = [_cumsum_rows(la[c * j:c * (j + 1)], tri3_ref) for j in range(grp)]
        cum = jnp.concatenate(cums, axis=0)
        bf = cum[:, :B_QK]
        cbx = cum[:, B_QK:] - la[:, B_QK:]
        tot_f = [cj[c - 1:c, :B_QK] for cj in cums]
        tot_b = [cj[c - 1:c, B_QK:] for cj in cums]
        bf2 = bf * LOG2E
        cb2 = cbx * LOG2E
        e_blk = rows_of([bf[n * j + n - 1:n * j + n, :] for j in range(span // n)], n)
        f_blk = rows_of([cbx[n * j:n * j + 1, :] for j in range(span // n)], n)
        kf = k * jnp.exp(e_blk - bf)
        kb = k * jnp.exp(cbx - f_blk)
        kt = (k * jnp.exp(rows_of(tot_f, c) - bf)).astype(BF16)
        lhs_inter = jnp.concatenate([q * jnp.exp(bf), q * jnp.exp(rows_of(tot_b, c) - cbx)],
                                    axis=1).astype(BF16)
        vb = v.astype(BF16)
        att_diag = jnp.where(own_blk, _dot(diag_lhs(q, k, bf2, cb2), estack_ref[...]), 0.0)

        outs = []
        upd = []
        for j in range(grp):
            sl = slice(c * j, c * (j + 1))
            outs.append(chunk_local(q[sl], k[sl], v[sl], bf[sl], cbx[sl], kf[sl], kb[sl], att_diag[sl]))
            upd.append(_dot_tn(vb[sl], kt[sl]))

        st = state[...]
        for j in range(grp):
            sl = slice(c * j, c * (j + 1))
            rhs = jnp.concatenate([st.astype(BF16), sb_ref[it * grp + j]], axis=1)
            outs[j] = outs[j] + _dot_nt(lhs_inter[sl], rhs)
            st = st * jnp.exp(tot_f[j]) + upd[j] * emask_ref[...]
        state[...] = st

        gate = r_ref[rows, :]
        o = jnp.concatenate(outs, axis=0)
        out = _head_norm(o, ones_ref, gain_ref[...]) * (gate * _sigmoid(gate))
        o_ref[rows, :] = out.astype(o_ref.dtype)
        return carry

    lax.fori_loop(0, nit, step, 0)


def _gla(z3, gup_bd, gbias, tri3, emask, estack, ones256, gain, tb):
    b, seq, _ = z3.shape
    nblk = seq // tb
    nch = tb // GLA_CHUNK
    ntot = seq // GLA_CHUNK

    def zspec(col, width, rev):
        blk = col // width
        if rev:
            return pl.BlockSpec((None, tb, width), lambda bi, i: (bi, nblk - 1 - i, blk))
        return pl.BlockSpec((None, tb, width), lambda bi, i: (bi, i, blk))

    consts = [gup_bd, gbias, tri3, emask]
    const_specs = [_const_spec(a.shape) for a in consts]
    nit = nch // GLA_GROUP
    sb = pl.pallas_call(
        functools.partial(_gla_bwd_body, nit),
        grid=(b, nblk),
        in_specs=[zspec(Z_BK, B_QK, True), zspec(Z_BV, B_WIDTH, True), zspec(Z_BG, LANES, True)]
        + const_specs,
        out_specs=pl.BlockSpec((None, nch, B_WIDTH, B_QK), lambda bi, i: (bi, nblk - 1 - i, 0, 0)),
        out_shape=jax.ShapeDtypeStruct((b, ntot, B_WIDTH, B_QK), BF16),
        scratch_shapes=[pltpu.VMEM((B_WIDTH, B_QK), F32)],
        compiler_params=_cparams(("parallel", "arbitrary")),
        name="gla_bwd",
    )(z3, z3, z3, *consts)

    consts2 = [gup_bd, gbias, tri3, emask, estack, ones256, gain]
    return pl.pallas_call(
        functools.partial(_gla_main_body, nit),
        grid=(b, nblk),
        in_specs=[zspec(Z_BQ, B_QK, False), zspec(Z_BK, B_QK, False), zspec(Z_BV, B_WIDTH, False),
                  zspec(Z_BR, B_WIDTH, False), zspec(Z_BG, LANES, False),
                  pl.BlockSpec((None, nch, B_WIDTH, B_QK), lambda bi, i: (bi, i, 0, 0))]
        + [_const_spec(a.shape) for a in consts2],
        out_specs=pl.BlockSpec((None, tb, B_WIDTH), lambda bi, i: (bi, i, 0)),
        out_shape=jax.ShapeDtypeStruct((b, seq, B_WIDTH), BF16),
        scratch_shapes=[pltpu.VMEM((B_WIDTH, B_QK), F32)],
        compiler_params=_cparams(("parallel", "arbitrary")),
        name="gla_main",
    )(z3, z3, z3, z3, z3, sb, *consts2)


def _ret_bwd_body(nch, k_ref, v_ref, raw_ref, bmask_ref, rb_ref, state):
    @pl.when(pl.program_id(1) == 0)
    def _():
        state[...] = jnp.zeros_like(state)

    c = RET_CHUNK
    lg1 = _log_sigmoid(raw_ref[1:2, :])
    pos = _iota((c, C_WIDTH), 0).astype(F32)
    kdec = jnp.exp(pos * lg1)
    chunk_decay = jnp.exp(float(c) * lg1)

    upd = []
    for j in range(nch):
        rows = pl.ds(c * j, c)
        kt = (k_ref[rows, :] * kdec).astype(BF16)
        upd.append(_dot_tn(kt, v_ref[rows, :].astype(BF16)))
    st = state[...]
    for j in reversed(range(nch)):
        rb_ref[j] = st.astype(BF16)
        st = chunk_decay * st + upd[j] * bmask_ref[...]
    state[...] = st


def _ret_main_body(nch, q_ref, k_ref, v_ref, g_ref, rb_ref, raw_ref, raw_s_ref, bmask_ref, ones_ref,
                   gain_ref, o_ref, state):
    @pl.when(pl.program_id(1) == 0)
    def _():
        state[...] = jnp.zeros_like(state)

    c = RET_CHUNK
    lg = _log_sigmoid(raw_ref[...])
    lg0, lg1 = lg[0:1, :], lg[1:2, :]
    lgs = _log_sigmoid(raw_s_ref[...])
    pos = _iota((c, C_WIDTH), 0).astype(F32)
    qdec_f = jnp.exp((pos + 1.0) * lg0)
    qdec_b = jnp.exp((float(c) - pos) * lg1)
    kdec_f = jnp.exp((float(c) - 1.0 - pos) * lg0)
    chunk_decay = jnp.exp(float(c) * lg0)
    t_idx = _iota((c, C_HEADS * c), 0)
    s_idx = _iota((c, C_HEADS * c), 1) & (c - 1)
    rel = (t_idx - s_idx).astype(F32)
    dmat = jnp.where(rel >= 0.0, jnp.exp(jnp.maximum(rel, 0.0) * lgs[0:1, :]),
                     jnp.exp(jnp.maximum(-rel, 0.0) * lgs[1:2, :]))
    rho = _iota((C_HEADS * c, C_WIDTH), 0)
    lane = _iota((C_HEADS * c, C_WIDTH), 1)
    head_ok = (rho >> 7) == (lane >> 6)

    outs, lhs_inter, upd = [], [], []
    for j in range(nch):
        rows = pl.ds(c * j, c)
        q = q_ref[rows, :]
        k = k_ref[rows, :]
        v = v_ref[rows, :]
        k4 = jnp.where(head_ok, jnp.concatenate([k] * C_HEADS, axis=0), 0.0).astype(BF16)
        v4 = jnp.where(head_ok, jnp.concatenate([v] * C_HEADS, axis=0), 0.0).astype(BF16)
        scores = _dot_nt(q.astype(BF16), k4) * dmat
        outs.append(_dot(scores.astype(BF16), v4))
        lhs_inter.append(jnp.concatenate([q * qdec_f, q * qdec_b], axis=1).astype(BF16))
        upd.append(_dot_tn((k * kdec_f).astype(BF16), v.astype(BF16)))
    st = state[...]
    for j in range(nch):
        rhs = jnp.concatenate([st.astype(BF16), rb_ref[j]], axis=0)
        outs[j] = outs[j] + _dot(lhs_inter[j], rhs)
        st = chunk_decay * st + upd[j] * bmask_ref[...]
    state[...] = st
    for j in range(nch):
        rows = pl.ds(c * j, c)
        gate = g_ref[rows, :]
        out = _head_norm(outs[j], ones_ref, gain_ref[...]) * (gate * _sigmoid(gate))
        o_ref[rows, :] = out.astype(o_ref.dtype)


def _ret(z3, raw256, raw512, ones256, gain, tb):
    b, seq, _ = z3.shape
    nblk = seq // tb
    nch = tb // RET_CHUNK
    ntot = seq // RET_CHUNK

    def zspec(col, rev):
        blk = col // C_WIDTH
        if rev:
            return pl.BlockSpec((None, tb, C_WIDTH), lambda bi, i: (bi, nblk - 1 - i, blk))
        return pl.BlockSpec((None, tb, C_WIDTH), lambda bi, i: (bi, i, blk))

    rb = pl.pallas_call(
        functools.partial(_ret_bwd_body, nch),
        grid=(b, nblk),
        in_specs=[zspec(Z_CK, True), zspec(Z_CV, True), _const_spec(raw256.shape),
                  _const_spec(ones256.shape)],
        out_specs=pl.BlockSpec((None, nch, C_WIDTH, C_WIDTH), lambda bi, i: (bi, nblk - 1 - i, 0, 0)),
        out_shape=jax.ShapeDtypeStruct((b, ntot, C_WIDTH, C_WIDTH), BF16),
        scratch_shapes=[pltpu.VMEM((C_WIDTH, C_WIDTH), F32)],
        compiler_params=_cparams(("parallel", "arbitrary")),
        name="ret_bwd",
    )(z3, z3, raw256, ones256)

    consts = [raw256, raw512, ones256, ones256, gain]
    return pl.pallas_call(
        functools.partial(_ret_main_body, nch),
        grid=(b, nblk),
        in_specs=[zspec(Z_CQ, False), zspec(Z_CK, False), zspec(Z_CV, False), zspec(Z_CG, False),
                  pl.BlockSpec((None, nch, C_WIDTH, C_WIDTH), lambda bi, i: (bi, i, 0, 0))]
        + [_const_spec(a.shape) for a in consts],
        out_specs=pl.BlockSpec((None, tb, C_WIDTH), lambda bi, i: (bi, i, 0)),
        out_shape=jax.ShapeDtypeStruct((b, seq, C_WIDTH), BF16),
        scratch_shapes=[pltpu.VMEM((C_WIDTH, C_WIDTH), F32)],
        compiler_params=_cparams(("parallel", "arbitrary")),
        name="ret_main",
    )(z3, z3, z3, z3, rb, *consts)


def _post_body(h_ref, oa_ref, ob_ref, oc_ref, p_ref, wo_ref, lm_ref, w1_ref, w2_ref, lp_ref,
               wg_ref, wp_ref, y_ref):
    def rms(x, g_ref):
        ms = jnp.mean(x * x, axis=-1, keepdims=True)
        return (x * lax.rsqrt(ms + EPS) * g_ref[...]).astype(BF16)

    h = h_ref[...]
    h = h + (_dot(oa_ref[...], wo_ref[0:A_WIDTH, :])
             + _dot(ob_ref[...], wo_ref[A_WIDTH:A_WIDTH + B_WIDTH, :])
             + _dot(oc_ref[...], wo_ref[A_WIDTH + B_WIDTH:, :]))
    m = rms(h, lm_ref)
    ff = D_FF // 4
    mlp = None
    for j in range(4):
        hid = _dot(m, w1_ref[:, ff * j:ff * (j + 1)])
        hid = jnp.square(jnp.maximum(hid, 0.0)).astype(BF16)
        part = _dot(hid, w2_ref[ff * j:ff * (j + 1), :])
        mlp = part if mlp is None else mlp + part
    h = h + mlp
    gate = _sigmoid(_dot(rms(h, lp_ref), wg_ref[...]))
    y_ref[...] = h + gate * _dot(p_ref[...].astype(BF16), wp_ref[...])


def _post(h2, oa, ob, oc, ple, wo, lm, w1, w2, lp, wg, wp, tm):
    n = h2.shape[0]

    def tok(width):
        return pl.BlockSpec((tm, width), lambda i: (i, 0))

    consts = [wo, lm, w1, w2, lp, wg, wp]
    return pl.pallas_call(
        _post_body,
        grid=(n // tm,),
        in_specs=[tok(D_MODEL), tok(A_WIDTH), tok(B_WIDTH), tok(C_WIDTH), tok(PLE_DIM)]
        + [_const_spec(a.shape) for a in consts],
        out_specs=tok(D_MODEL),
        out_shape=jax.ShapeDtypeStruct((n, D_MODEL), F32),
        compiler_params=_cparams(("parallel",)),
        name="post",
    )(h2, oa, ob, oc, ple, *consts)


def _rope_tables(seq, rot_dim, theta):
    half = rot_dim // 2
    inv_freq = 1.0 / (theta ** (jnp.arange(half, dtype=F32) * (2.0 / rot_dim)))
    ang = jnp.arange(seq, dtype=F32)[:, None] * inv_freq[None, :]
    cos, sin = jnp.cos(ang), jnp.sin(ang)
    pad = HEAD_DIM - rot_dim
    c = jnp.concatenate([cos, cos, jnp.ones((seq, pad), F32)], axis=1)
    sn = jnp.concatenate([-sin, jnp.zeros((seq, half + pad), F32)], axis=1)
    sp = jnp.concatenate([jnp.zeros((seq, half), F32), sin, jnp.zeros((seq, pad), F32)], axis=1)
    return jnp.concatenate([jnp.tile(t, (1, LANES // HEAD_DIM)) for t in (c, sn, sp)], axis=1)


def _block_ones(rows, cols, rblk, cblk):
    r = jnp.arange(rows)[:, None] // rblk
    c = jnp.arange(cols)[None, :] // cblk
    return r == c


def _layer_consts(i, ln_mix, w_in, attn_q_norm, attn_k_norm, gla_gate_up, gla_gate_bias, gla_out_norm,
                  ret_decay_raw, ret_out_norm, w_out, ln_mlp, w_mlp_in, w_mlp_out, ln_pe, w_pe_gate,
                  w_pe_proj):
    w = w_in[i]
    w_p = jnp.concatenate([w[:, :2304], w[:, 2336:N_IN], w[:, 2304:2336],
                           jnp.zeros((D_MODEL, Z_WIDTH - N_IN), F32)], axis=1).astype(BF16)
    gup = gla_gate_up[i].astype(BF16)
    gup_bd = jnp.zeros((LANES, 2 * B_QK), BF16)
    gup_bd = gup_bd.at[0:B_GATE_RANK, 0:B_QK].set(gup[0])
    gup_bd = gup_bd.at[B_GATE_RANK:2 * B_GATE_RANK, B_QK:].set(gup[1])
    return dict(
        ln_mix=ln_mix[i][None, :], w_p=w_p,
        qg=jnp.tile(attn_q_norm[i], 4)[None, :], kg=jnp.tile(attn_k_norm[i], 4)[None, :],
        gup_bd=gup_bd, gbias=gla_gate_bias[i].reshape(1, 2 * B_QK),
        gla_gain=gla_out_norm[i][None, :],
        raw256=jnp.repeat(ret_decay_raw[i], C_DIM, axis=1),
        raw512=jnp.repeat(ret_decay_raw[i], RET_CHUNK, axis=1),
        ret_gain=ret_out_norm[i][None, :],
        wo=w_out[i].astype(BF16), lm=ln_mlp[i][None, :], w1=w_mlp_in[i].astype(BF16),
        w2=w_mlp_out[i].astype(BF16), lp=ln_pe[i][None, :], wg=w_pe_gate[i].astype(BF16),
        wp=w_pe_proj[i].astype(BF16),
    )


def _shared_consts():
    tri = jnp.arange(GLA_CHUNK)[:, None] >= jnp.arange(GLA_CHUNK)[None, :]
    return dict(
        ones256=_block_ones(256, 256, HEAD_DIM, HEAD_DIM).astype(BF16),
        tri3=jnp.tile(tri, (1, 3)).astype(BF16),
        emask=_block_ones(B_WIDTH, B_QK, HEAD_DIM, B_KDIM).astype(F32),
        estack=_gla_diag_selector().astype(BF16),
    )


def _gla_diag_selector():
    r = jnp.arange(GLA_SUB * B_QK)
    c = jnp.arange(B_HEADS * GLA_CHUNK)
    same_s = (r // B_QK)[:, None] == (c % GLA_SUB)[None, :]
    same_head = ((r % B_QK) // B_KDIM)[:, None] == ((c % (B_HEADS * GLA_SUB)) // GLA_SUB)[None, :]
    return same_s & same_head


def _run_trunk(x, p, layers, shared):
    b, seq, _ = x.shape
    n = b * seq
    tm = 512
    tb = 512
    rope_a = _rope_tables(seq, ROPE_DIM, ROPE_THETA)
    rope_c = _rope_tables(seq, C_DIM, RET_THETA)
    h = x.reshape(n, D_MODEL)
    for i, lc in enumerate(layers):
        z = _proj_in(h, seq, lc["ln_mix"], lc["w_p"], shared["ones256"], lc["qg"], lc["kg"],
                     rope_a, rope_c, tm)
        z3 = z.reshape(b, seq, Z_WIDTH)
        oa = _attn(z3)
        ob = _gla(z3, lc["gup_bd"], lc["gbias"], shared["tri3"], shared["emask"], shared["estack"],
                  shared["ones256"], lc["gla_gain"], tb)
        oc = _ret(z3, lc["raw256"], lc["raw512"], shared["ones256"], lc["ret_gain"], tb)
        h = _post(h, oa.reshape(n, A_WIDTH), ob.reshape(n, B_WIDTH), oc.reshape(n, C_WIDTH),
                  p[i].reshape(n, PLE_DIM), lc["wo"], lc["lm"], lc["w1"], lc["w2"], lc["lp"],
                  lc["wg"], lc["wp"], tm)
    return h.reshape(b, seq, D_MODEL)


def kernel(x_prompt, x_sample, p_prompt, p_sample, ln_mix, w_in, attn_q_norm, attn_k_norm, gla_gate_up, gla_gate_bias, gla_out_norm, ret_decay_raw, ret_out_norm, w_out, ln_mlp, w_mlp_in, w_mlp_out, ln_pe, w_pe_gate, w_pe_proj):
    depth = w_in.shape[0]
    layers = [_layer_consts(i, ln_mix, w_in, attn_q_norm, attn_k_norm, gla_gate_up, gla_gate_bias,
                            gla_out_norm, ret_decay_raw, ret_out_norm, w_out, ln_mlp, w_mlp_in,
                            w_mlp_out, ln_pe, w_pe_gate, w_pe_proj) for i in range(depth)]
    shared = _shared_consts()
    y_prompt = _run_trunk(x_prompt, p_prompt, layers, shared)
    y_sample = _run_trunk(x_sample, p_sample, layers, shared)
    return (y_prompt, y_sample)
```

```python
import functools

import jax
import jax.numpy as jnp
from jax import lax
from jax.experimental import pallas as pl
from jax.experimental.pallas import tpu as pltpu

F32 = jnp.float32
BF16 = jnp.bfloat16

D_MODEL = 1024
HEAD_DIM = 64
A_HEADS = 8
A_WIDTH = 512
A_PATTERNS = ((128, 1), (512, 4), (2048, 16))
A_RADIUS = 64
ROPE_THETA = 500000.0
ROPE_DIM = 16
B_HEADS = 4
B_KDIM = 32
B_QK = 128
B_WIDTH = 256
B_GATE_RANK = 16
B_GATE_TAU = 16.0
C_HEADS = 4
C_DIM = 64
C_WIDTH = 256
RET_THETA = 10000.0
N_IN = 3360
D_FF = 4096
PLE_DIM = 256
EPS = 1e-6
NEG = -1e30
LOG2E = 1.4426950408889634

LANES = 128
VMEM_LIMIT = 56 * 1024 * 1024

Z_BQ, Z_BK, Z_BV, Z_BR = 0, 128, 256, 512
Z_CQ, Z_CK, Z_CV, Z_CG = 768, 1024, 1280, 1536
Z_BG = 1792
Z_WIDTH = 1920
A_SLABS = 3 * A_WIDTH // LANES
W_WIDTH = A_SLABS * LANES + Z_WIDTH

GLA_CHUNK = 64
GLA_SUB = 16
GLA_GROUP = 4
RET_CHUNK = 128
ATT_TILE = 2048
ATT_SUB = 128
ATT_KEYS = ATT_SUB + 2 * A_RADIUS
ATT_SKEW = 2


def _cparams(sem):
    return pltpu.CompilerParams(dimension_semantics=sem, vmem_limit_bytes=VMEM_LIMIT)


def _const_spec(shape):
    nd = len(shape)
    return pl.BlockSpec(shape, lambda *_: (0,) * nd, pipeline_mode=pl.Buffered(1))


def _sigmoid(x):
    return 1.0 / (1.0 + jnp.exp(-x))


def _log_sigmoid(x):
    return jnp.minimum(x, 0.0) - jnp.log1p(jnp.exp(-jnp.abs(x)))


def _iota(shape, dim):
    return lax.broadcasted_iota(jnp.int32, shape, dim)


def _dot(a, b):
    return jnp.dot(a, b, preferred_element_type=F32)


def _dot_nt(a, b):
    return lax.dot_general(a, b, (((1,), (1,)), ((), ())), preferred_element_type=F32)


def _dot_tn(a, b):
    return lax.dot_general(a, b, (((0,), (0,)), ((), ())), preferred_element_type=F32)


def _head_norm(o, ones_ref, gain):
    ssum = _dot((o * o).astype(BF16), ones_ref[...])
    return o * lax.rsqrt(ssum * (1.0 / HEAD_DIM) + EPS) * gain


def _rope(y, tab_ref, shift):
    c = tab_ref[:, 0:LANES]
    sn = tab_ref[:, LANES:2 * LANES]
    sp = tab_ref[:, 2 * LANES:3 * LANES]
    outs = []
    for j in range(y.shape[1] // LANES):
        yj = y[:, j * LANES:(j + 1) * LANES]
        outs.append(yj * c + pltpu.roll(yj, LANES - shift, 1) * sn + pltpu.roll(yj, shift, 1) * sp)
    return jnp.concatenate(outs, axis=1)


def _proj_in_body(x_ref, g_ref, w_ref, ones_ref, qg_ref, kg_ref, ra_ref, rc_ref, a_ref, z_ref):
    x = x_ref[...]
    ms = jnp.mean(x * x, axis=-1, keepdims=True)
    u = (x * lax.rsqrt(ms + EPS) * g_ref[...]).astype(BF16)
    w0 = A_SLABS * LANES

    def proj(a, b):
        return _dot(u, w_ref[:, a:b])

    def zproj(a, b):
        return proj(w0 + a, w0 + b)

    def qk_norm(y, gain_ref):
        halves = []
        for j in range(2):
            yj = y[:, 256 * j:256 * (j + 1)]
            halves.append(_head_norm(yj, ones_ref, gain_ref[...]))
        return jnp.concatenate(halves, axis=1)

    def put_slabs(first, y):
        for j in range(A_WIDTH // LANES):
            a_ref[first + j] = y[:, LANES * j:LANES * (j + 1)].astype(BF16)

    aq = _rope(qk_norm(proj(0, A_WIDTH), qg_ref), ra_ref, ROPE_DIM // 2)
    put_slabs(0, aq * (HEAD_DIM ** -0.5 * LOG2E))
    put_slabs(4, _rope(qk_norm(proj(A_WIDTH, 2 * A_WIDTH), kg_ref), ra_ref, ROPE_DIM // 2))
    put_slabs(8, proj(2 * A_WIDTH, 3 * A_WIDTH))
    z_ref[:, Z_BQ:Z_BK] = zproj(Z_BQ, Z_BK) * (B_KDIM ** -0.5)
    z_ref[:, Z_BK:Z_CQ] = zproj(Z_BK, Z_CQ)
    z_ref[:, Z_CQ:Z_CK] = _rope(zproj(Z_CQ, Z_CK), rc_ref, C_DIM // 2)
    z_ref[:, Z_CK:Z_CV] = _rope(zproj(Z_CK, Z_CV), rc_ref, C_DIM // 2) * (C_DIM ** -0.5)
    z_ref[:, Z_CV:Z_WIDTH] = zproj(Z_CV, Z_WIDTH)


def _proj_in(x2, batch, seq, ln, w_p, ones256, qg, kg, rope_a, rope_c, tm):
    n = x2.shape[0]
    per_seq = seq // tm
    return pl.pallas_call(
        _proj_in_body,
        grid=(n // tm,),
        in_specs=[
            pl.BlockSpec((tm, D_MODEL), lambda i: (i, 0)),
            _const_spec((1, D_MODEL)),
            _const_spec((D_MODEL, W_WIDTH)),
            _const_spec((256, 256)),
            _const_spec((1, 256)),
            _const_spec((1, 256)),
            pl.BlockSpec((tm, 3 * LANES), lambda i: (i % per_seq, 0)),
            pl.BlockSpec((tm, 3 * LANES), lambda i: (i % per_seq, 0)),
        ],
        out_specs=[
            pl.BlockSpec((None, A_SLABS, tm, LANES), lambda i: (i // per_seq, 0, i % per_seq, 0)),
            pl.BlockSpec((tm, Z_WIDTH), lambda i: (i, 0)),
        ],
        out_shape=[
            jax.ShapeDtypeStruct((batch, A_SLABS, seq, LANES), BF16),
            jax.ShapeDtypeStruct((n, Z_WIDTH), F32),
        ],
        compiler_params=_cparams(("parallel",)),
        name="proj_in",
    )(x2, ln, w_p, ones256, qg, kg, rope_a, rope_c)


def _attn_body(seq, *refs):
    npat = len(A_PATTERNS)
    ins = [refs[7 * p:7 * p + 7] for p in range(npat)]
    o_ref, m_ref, l_ref, acc_ref, band_ref = refs[7 * npat:]
    t = ATT_TILE
    sub = ATT_SUB
    tile_start = pl.program_id(2) * t

    low_half = _iota((sub, LANES), 1) < HEAD_DIM
    a_idx = _iota((2 * sub, ATT_KEYS), 0) & (sub - 1)
    rel = _iota((2 * sub, ATT_KEYS), 1) - a_idx
    band_ref[...] = jnp.where((rel >= 0) & (rel <= 2 * A_RADIUS), 0.0, NEG)
    c_row = _iota((1, ATT_KEYS), 1)
    ones_v = jnp.ones((ATT_KEYS, LANES), BF16)

    def key_rows(prev_ref, cur_ref, next_ref, j, dil, lanes):
        per_res = t // dil
        lo, hi = sub * j - A_RADIUS, sub * j + sub + A_RADIUS
        parts = []
        if lo < 0:
            parts.append(prev_ref[:, lanes])
        parts.append(cur_ref[max(lo, 0):min(hi, per_res), lanes])
        if hi > per_res:
            parts.append(next_ref[:, lanes])
        return parts[0] if len(parts) == 1 else jnp.concatenate(parts, axis=0)

    def scores_stage(tile):
        pat, r, j, dil = tile
        lanes = slice(LANES * r, LANES * (r + 1))
        q_ref, kp_ref, kc_ref, kn_ref = ins[pat][0:4]
        q_sub = q_ref[sub * j:sub * (j + 1), lanes]
        zero = jnp.zeros_like(q_sub)
        q_st = jnp.concatenate([jnp.where(low_half, q_sub, zero), jnp.where(low_half, zero, q_sub)], axis=0)
        return _dot_nt(q_st, key_rows(kp_ref, kc_ref, kn_ref, j, dil, lanes))

    def values_stage(tile, s):
        pat, r, j, dil = tile
        lanes = slice(LANES * r, LANES * (r + 1))
        key_pos = (tile_start + r + dil * (sub * j - A_RADIUS)) + dil * c_row
        bias = jnp.where((key_pos >= 0) & (key_pos < seq), 0.0, NEG)
        s = (s + band_ref[...]) + bias
        m_row = jnp.broadcast_to(jnp.max(s, axis=1, keepdims=True), (2 * sub, LANES))
        p = jnp.exp2(s - jnp.concatenate([m_row] * (ATT_KEYS // LANES), axis=1)).astype(BF16)
        v_sub = key_rows(*ins[pat][4:7], j, dil, lanes)
        return m_row, _dot(p, jnp.concatenate([v_sub, ones_v], axis=1))

    def store_stage(tile, m_row, pvl):
        pat, r, j, dil = tile
        rw = pl.ds(r + dil * sub * j, sub, stride=dil)
        m_ref[pat, rw, :] = jnp.where(low_half, m_row[0:sub], m_row[sub:])
        l_ref[pat, rw, :] = jnp.where(low_half, pvl[0:sub, LANES:], pvl[sub:, LANES:])
        acc_ref[pat, rw, :] = jnp.where(low_half, pvl[0:sub, 0:LANES], pvl[sub:, 0:LANES])

    tiles = [(pat, r, j, dil)
             for pat, (_, dil) in enumerate(A_PATTERNS)
             for r in range(dil) for j in range(t // (sub * dil))]
    scored, valued = {}, {}
    for step in range(len(tiles) + 2 * ATT_SKEW):
        i_store, i_val = step - 2 * ATT_SKEW, step - ATT_SKEW
        if 0 <= i_store < len(tiles):
            store_stage(tiles[i_store], *valued.pop(i_store))
        if 0 <= i_val < len(tiles):
            valued[i_val] = values_stage(tiles[i_val], scored.pop(i_val))
        if step < len(tiles):
            scored[step] = scores_stage(tiles[step])

    blk = 2 * sub
    for i in range(t // blk):
        rs = pl.ds(i * blk, blk)
        m_g = [m_ref[g, rs, :] for g in range(npat)]
        m_all = functools.reduce(jnp.maximum, m_g)
        w_g = [jnp.exp2(m - m_all) for m in m_g]
        l_all = functools.reduce(lambda a, b: a + b, [w * l_ref[g, rs, :] for g, w in enumerate(w_g)])
        num = functools.reduce(lambda a, b: a + b, [w * acc_ref[g, rs, :] for g, w in enumerate(w_g)])
        o_ref[rs, :] = (num / l_all).astype(o_ref.dtype)


def _attn(a):
    b, _, seq, _ = a.shape
    t = ATT_TILE
    nt = seq // t
    hps = A_HEADS // 2
    views, specs = [], []
    for _, dil in A_PATTERNS:
        width, rows, nrows = LANES * dil, t // dil, seq // dil
        per, last = rows // A_RADIUS, nrows // A_RADIUS - 1

        def cur(c0, rows=rows, width=width):
            return pl.BlockSpec((None, None, rows, width), lambda bi, hp, i: (bi, c0 + hp, i, 0))

        def prev(c0, width=width, per=per):
            return pl.BlockSpec((None, None, A_RADIUS, width),
                                lambda bi, hp, i: (bi, c0 + hp, jnp.maximum(i * per - 1, 0), 0))

        def nxt(c0, width=width, per=per, last=last):
            return pl.BlockSpec((None, None, A_RADIUS, width),
                                lambda bi, hp, i: (bi, c0 + hp, jnp.minimum((i + 1) * per, last), 0))

        specs += [cur(0), prev(hps), cur(hps), nxt(hps), prev(2 * hps), cur(2 * hps), nxt(2 * hps)]
        views += [a.reshape(b, A_SLABS, nrows, width)] * 7

    return pl.pallas_call(
        functools.partial(_attn_body, seq),
        grid=(b, hps, nt),
        in_specs=specs,
        out_specs=pl.BlockSpec((None, t, LANES), lambda bi, hp, i: (bi, i, hp)),
        out_shape=jax.ShapeDtypeStruct((b, seq, A_WIDTH), BF16),
        scratch_shapes=[
            pltpu.VMEM((len(A_PATTERNS), t, LANES), F32),
            pltpu.VMEM((len(A_PATTERNS), t, LANES), F32),
            pltpu.VMEM((len(A_PATTERNS), t, LANES), F32),
            pltpu.VMEM((2 * ATT_SUB, ATT_KEYS), F32),
        ],
        compiler_params=_cparams(("parallel", "parallel", "parallel")),
        name="attn",
    )(*views)


def _split3(x):
    hi = x.astype(BF16)
    r1 = x - hi.astype(F32)
    mid = r1.astype(BF16)
    lo = (r1 - mid.astype(F32)).astype(BF16)
    return hi, mid, lo


def _gla_log_decay(g, gup_ref, gbias_ref):
    logits = _dot(g.astype(BF16), gup_ref[...]) + gbias_ref[...]
    return _log_sigmoid(logits) * (1.0 / B_GATE_TAU)


def _cumsum_rows(la, tri3_ref):
    hi, mid, lo = _split3(la)
    return _dot(tri3_ref[...], jnp.concatenate([hi, mid, lo], axis=0))


def _gla_bwd_body(nit, k_ref, v_ref, g_ref, gup_ref, gbias_ref, tri3_ref, emask_ref, sb_ref, state):
    @pl.when(pl.program_id(1) == 0)
    def _():
        state[...] = jnp.zeros_like(state)

    c = GLA_CHUNK
    grp = GLA_GROUP
    span = grp * c

    def step(i, carry):
        it = nit - 1 - i
        rows = pl.ds(pl.multiple_of(it * span, span), span)
        logits = _dot(g_ref[rows, :].astype(BF16), gup_ref[:, B_QK:]) + gbias_ref[:, B_QK:]
        la = _log_sigmoid(logits) * (1.0 / B_GATE_TAU)
        cums = [_cumsum_rows(la[c * j:c * (j + 1)], tri3_ref) for j in range(grp)]
        cbx = jnp.concatenate(cums, axis=0) - la
        kt = (k_ref[rows, :] * jnp.exp(cbx)).astype(BF16)
        vb = v_ref[rows, :].astype(BF16)
        upd = [_dot_tn(vb[c * j:c * (j + 1)], kt[c * j:c * (j + 1)]) for j in range(grp)]
        st = state[...]
        for j in reversed(range(grp)):
            sb_ref[it * grp + j] = st.astype(BF16)
            st = st * jnp.exp(cums[j][c - 1:c, :]) + upd[j] * emask_ref[...]
        state[...] = st
        return carry

    lax.fori_loop(0, nit, step, 0)


def _gla_main_body(nit, q_ref, k_ref, v_ref, r_ref, g_ref, sb_ref, gup_ref, gbias_ref, tri3_ref,
                   emask_ref, estack_ref, ones_ref, gain_ref, o_ref, state):
    @pl.when(pl.program_id(1) == 0)
    def _():
        state[...] = jnp.zeros_like(state)

    c = GLA_CHUNK
    n = GLA_SUB
    nsub = c // n
    grp = GLA_GROUP
    span = grp * c
    row = _iota((c, B_QK), 0)
    row_8 = _iota((8, B_QK), 0)
    rho = _iota((B_HEADS * n, B_QK), 0)
    head_k_ok = (rho >> 4) == (_iota((B_HEADS * n, B_QK), 1) >> 5)
    rho_v = _iota((B_HEADS * n, B_WIDTH), 0)
    head_v_ok = (rho_v >> 4) == (_iota((B_HEADS * n, B_WIDTH), 1) >> 6)
    own_blk = ((_iota((span, B_HEADS * c), 0) & (c - 1)) >> 4) == (_iota((span, B_HEADS * c), 1) >> 6)
    zeros_k = jnp.zeros((B_HEADS * n, B_QK), F32)

    def rows_of(vals, height):
        return jnp.concatenate([jnp.broadcast_to(x, (height, x.shape[1])) for x in vals], axis=0)

    def by_head(x, ok):
        return jnp.where(ok, jnp.concatenate([x] * B_HEADS, axis=0), 0.0)

    def diag_lhs(q, k, bf2, cb2):
        nblk = span // n
        slabs = []
        for s in range(n):
            ks = rows_of([k[n * i + s:n * i + s + 1] for i in range(nblk)], n)
            pieces = []
            for i in range(nblk):
                ref_f = bf2[n * i + s:n * i + s + 1]
                ref_b = cb2[n * i + s:n * i + s + 1]
                for r0 in range(0, n, 8):
                    rs = slice(n * i + r0, n * i + r0 + 8)
                    if s <= r0:
                        pieces.append(bf2[rs] - ref_f)
                    elif s >= r0 + 8:
                        pieces.append(ref_b - cb2[rs])
                    else:
                        pieces.append(jnp.where(row_8 >= s - r0, bf2[rs] - ref_f, ref_b - cb2[rs]))
            arg = jnp.concatenate(pieces, axis=0)
            slabs.append(((q * ks) * jnp.exp2(arg)).astype(BF16))
        return jnp.concatenate(slabs, axis=1)

    def chunk_local(q, k, v, bf, cbx, kf, kb, att_diag):
        e_rows = [bf[n * j + n - 1:n * j + n, :] for j in range(nsub)]
        f_rows = [cbx[n * j:n * j + 1, :] for j in range(nsub)]
        lhs_parts = []
        for j in range(nsub - 1):
            lhs_parts.append(jnp.where(row >= n * (j + 1),
                                       q * jnp.exp(jnp.minimum(bf - e_rows[j], 0.0)), 0.0))
        for j in range(1, nsub):
            lhs_parts.append(jnp.where(row < n * j, q * jnp.exp(jnp.minimum(f_rows[j] - cbx, 0.0)), 0.0))
        rhs_rows = []
        for j in range(nsub):
            kfj = by_head(kf[n * j:n * (j + 1)], head_k_ok)
            kbj = by_head(kb[n * j:n * (j + 1)], head_k_ok)
            parts = [kfj if (jj == j and j < nsub - 1) else zeros_k for jj in range(nsub - 1)]
            parts += [kbj if (jj == j and j > 0) else zeros_k for jj in range(1, nsub)]
            rhs_rows.append(jnp.concatenate(parts, axis=1))
        att = _dot_nt(jnp.concatenate(lhs_parts, axis=1).astype(BF16),
                      jnp.concatenate(rhs_rows, axis=0).astype(BF16))
        v4 = jnp.concatenate([by_head(v[n * j:n * (j + 1)], head_v_ok) for j in range(nsub)],
                             axis=0).astype(BF16)
        return _dot((att + att_diag).astype(BF16), v4)

    def step(it, carry):
        rows = pl.ds(pl.multiple_of(it * span, span), span)
        q = q_ref[rows, :]
        k = k_ref[rows, :]
        v = v_ref[rows, :]
        la = _gla_log_decay(g_ref[rows, :], gup_ref, gbias_ref)
        cums = [_cumsum_rows(la[c * j:c * (j + 1)], tri3_ref) for j in range(grp)]
        cum = jnp.concatenate(cums, axis=0)
        bf = cum[:, :B_QK]
        cbx = cum[:, B_QK:] - la[:, B_QK:]
        tot_f = [cj[c - 1:c, :B_QK] for cj in cums]
        tot_b = [cj[c - 1:c, B_QK:] for cj in cums]
        bf2 = bf * LOG2E
        cb2 = cbx * LOG2E
        e_blk = rows_of([bf[n * j + n - 1:n * j + n, :] for j in range(span // n)], n)
        f_blk = rows_of([cbx[n * j:n * j + 1, :] for j in range(span // n)], n)
        kf = k * jnp.exp(e_blk - bf)
        kb = k * jnp.exp(cbx - f_blk)
        kt = (k * jnp.exp(rows_of(tot_f, c) - bf)).astype(BF16)
        lhs_inter = jnp.concatenate([q * jnp.exp(bf), q * jnp.exp(rows_of(tot_b, c) - cbx)],
                                    axis=1).astype(BF16)
        vb = v.astype(BF16)
        att_diag = jnp.where(own_blk, _dot(diag_lhs(q, k, bf2, cb2), estack_ref[...]), 0.0)

        outs = []
        upd = []
        for j in range(grp):
            sl = slice(c * j, c * (j + 1))
            outs.append(chunk_local(q[sl], k[sl], v[sl], bf[sl], cbx[sl], kf[sl], kb[sl], att_diag[sl]))
            upd.append(_dot_tn(vb[sl], kt[sl]))

        st = state[...]
        for j in range(grp):
            sl = slice(c * j, c * (j + 1))
            rhs = jnp.concatenate([st.astype(BF16), sb_ref[it * grp + j]], axis=1)
            outs[j] = outs[j] + _dot_nt(lhs_inter[sl], rhs)
            st = st * jnp.exp(tot_f[j]) + upd[j] * emask_ref[...]
        state[...] = st

        gate = r_ref[rows, :]
        o = jnp.concatenate(outs, axis=0)
        out = _head_norm(o, ones_ref, gain_ref[...]) * (gate * _sigmoid(gate))
        o_ref[rows, :] = out.astype(o_ref.dtype)
        return carry

    lax.fori_loop(0, nit, step, 0)


def _gla(z3, gup_bd, gbias, tri3, emask, estack, ones256, gain, tb):
    b, seq, _ = z3.shape
    nblk = seq // tb
    nch = tb // GLA_CHUNK
    ntot = seq // GLA_CHUNK

    def zspec(col, width, rev):
        blk = col // width
        if rev:
            return pl.BlockSpec((None, tb, width), lambda bi, i: (bi, nblk - 1 - i, blk))
        return pl.BlockSpec((None, tb, width), lambda bi, i: (bi, i, blk))

    consts = [gup_bd, gbias, tri3, emask]
    const_specs = [_const_spec(a.shape) for a in consts]
    nit = nch // GLA_GROUP
    sb = pl.pallas_call(
        functools.partial(_gla_bwd_body, nit),
        grid=(b, nblk),
        in_specs=[zspec(Z_BK, B_QK, True), zspec(Z_BV, B_WIDTH, True), zspec(Z_BG, LANES, True)]
        + const_specs,
        out_specs=pl.BlockSpec((None, nch, B_WIDTH, B_QK), lambda bi, i: (bi, nblk - 1 - i, 0, 0)),
        out_shape=jax.ShapeDtypeStruct((b, ntot, B_WIDTH, B_QK), BF16),
        scratch_shapes=[pltpu.VMEM((B_WIDTH, B_QK), F32)],
        compiler_params=_cparams(("parallel", "arbitrary")),
        name="gla_bwd",
    )(z3, z3, z3, *consts)

    consts2 = [gup_bd, gbias, tri3, emask, estack, ones256, gain]
    return pl.pallas_call(
        functools.partial(_gla_main_body, nit),
        grid=(b, nblk),
        in_specs=[zspec(Z_BQ, B_QK, False), zspec(Z_BK, B_QK, False), zspec(Z_BV, B_WIDTH, False),
                  zspec(Z_BR, B_WIDTH, False), zspec(Z_BG, LANES, False),
                  pl.BlockSpec((None, nch, B_WIDTH, B_QK), lambda bi, i: (bi, i, 0, 0))]
        + [_const_spec(a.shape) for a in consts2],
        out_specs=pl.BlockSpec((None, tb, B_WIDTH), lambda bi, i: (bi, i, 0)),
        out_shape=jax.ShapeDtypeStruct((b, seq, B_WIDTH), BF16),
        scratch_shapes=[pltpu.VMEM((B_WIDTH, B_QK), F32)],
        compiler_params=_cparams(("parallel", "arbitrary")),
        name="gla_main",
    )(z3, z3, z3, z3, z3, sb, *consts2)


def _ret_bwd_body(nch, k_ref, v_ref, raw_ref, bmask_ref, rb_ref, state):
    @pl.when(pl.program_id(1) == 0)
    def _():
        state[...] = jnp.zeros_like(state)

    c = RET_CHUNK
    lg1 = _log_sigmoid(raw_ref[1:2, :])
    pos = _iota((c, C_WIDTH), 0).astype(F32)
    kdec = jnp.exp(pos * lg1)
    chunk_decay = jnp.exp(float(c) * lg1)

    upd = []
    for j in range(nch):
        rows = pl.ds(c * j, c)
        kt = (k_ref[rows, :] * kdec).astype(BF16)
        upd.append(_dot_tn(kt, v_ref[rows, :].astype(BF16)))
    st = state[...]
    for j in reversed(range(nch)):
        rb_ref[j] = st.astype(BF16)
        st = chunk_decay * st + upd[j] * bmask_ref[...]
    state[...] = st


def _ret_main_body(nch, q_ref, k_ref, v_ref, g_ref, rb_ref, raw_ref, raw_s_ref, bmask_ref, ones_ref,
                   gain_ref, o_ref, state):
    @pl.when(pl.program_id(1) == 0)
    def _():
        state[...] = jnp.zeros_like(state)

    c = RET_CHUNK
    lg = _log_sigmoid(raw_ref[...])
    lg0, lg1 = lg[0:1, :], lg[1:2, :]
    lgs = _log_sigmoid(raw_s_ref[...])
    pos = _iota((c, C_WIDTH), 0).astype(F32)
    qdec_f = jnp.exp((pos + 1.0) * lg0)
    qdec_b = jnp.exp((float(c) - pos) * lg1)
    kdec_f = jnp.exp((float(c) - 1.0 - pos) * lg0)
    chunk_decay = jnp.exp(float(c) * lg0)
    t_idx = _iota((c, C_HEADS * c), 0)
    s_idx = _iota((c, C_HEADS * c), 1) & (c - 1)
    rel = (t_idx - s_idx).astype(F32)
    dmat = jnp.where(rel >= 0.0, jnp.exp(jnp.maximum(rel, 0.0) * lgs[0:1, :]),
                     jnp.exp(jnp.maximum(-rel, 0.0) * lgs[1:2, :]))
    rho = _iota((C_HEADS * c, C_WIDTH), 0)
    lane = _iota((C_HEADS * c, C_WIDTH), 1)
    head_ok = (rho >> 7) == (lane >> 6)

    outs, lhs_inter, upd = [], [], []
    for j in range(nch):
        rows = pl.ds(c * j, c)
        q = q_ref[rows, :]
        k = k_ref[rows, :]
        v = v_ref[rows, :]
        k4 = jnp.where(head_ok, jnp.concatenate([k] * C_HEADS, axis=0), 0.0).astype(BF16)
        v4 = jnp.where(head_ok, jnp.concatenate([v] * C_HEADS, axis=0), 0.0).astype(BF16)
        scores = _dot_nt(q.astype(BF16), k4) * dmat
        outs.append(_dot(scores.astype(BF16), v4))
        lhs_inter.append(jnp.concatenate([q * qdec_f, q * qdec_b], axis=1).astype(BF16))
        upd.append(_dot_tn((k * kdec_f).astype(BF16), v.astype(BF16)))
    st = state[...]
    for j in range(nch):
        rhs = jnp.concatenate([st.astype(BF16), rb_ref[j]], axis=0)
        outs[j] = outs[j] + _dot(lhs_inter[j], rhs)
        st = chunk_decay * st + upd[j] * bmask_ref[...]
    state[...] = st
    for j in range(nch):
        rows = pl.ds(c * j, c)
        gate = g_ref[rows, :]
        out = _head_norm(outs[j], ones_ref, gain_ref[...]) * (gate * _sigmoid(gate))
        o_ref[rows, :] = out.astype(o_ref.dtype)


def _ret(z3, raw256, raw512, ones256, gain, tb):
    b, seq, _ = z3.shape
    nblk = seq // tb
    nch = tb // RET_CHUNK
    ntot = seq // RET_CHUNK

    def zspec(col, rev):
        blk = col // C_WIDTH
        if rev:
            return pl.BlockSpec((None, tb, C_WIDTH), lambda bi, i: (bi, nblk - 1 - i, blk))
        return pl.BlockSpec((None, tb, C_WIDTH), lambda bi, i: (bi, i, blk))

    rb = pl.pallas_call(
        functools.partial(_ret_bwd_body, nch),
        grid=(b, nblk),
        in_specs=[zspec(Z_CK, True), zspec(Z_CV, True), _const_spec(raw256.shape),
                  _const_spec(ones256.shape)],
        out_specs=pl.BlockSpec((None, nch, C_WIDTH, C_WIDTH), lambda bi, i: (bi, nblk - 1 - i, 0, 0)),
        out_shape=jax.ShapeDtypeStruct((b, ntot, C_WIDTH, C_WIDTH), BF16),
        scratch_shapes=[pltpu.VMEM((C_WIDTH, C_WIDTH), F32)],
        compiler_params=_cparams(("parallel", "arbitrary")),
        name="ret_bwd",
    )(z3, z3, raw256, ones256)

    consts = [raw256, raw512, ones256, ones256, gain]
    return pl.pallas_call(
        functools.partial(_ret_main_body, nch),
        grid=(b, nblk),
        in_specs=[zspec(Z_CQ, False), zspec(Z_CK, False), zspec(Z_CV, False), zspec(Z_CG, False),
                  pl.BlockSpec((None, nch, C_WIDTH, C_WIDTH), lambda bi, i: (bi, i, 0, 0))]
        + [_const_spec(a.shape) for a in consts],
        out_specs=pl.BlockSpec((None, tb, C_WIDTH), lambda bi, i: (bi, i, 0)),
        out_shape=jax.ShapeDtypeStruct((b, seq, C_WIDTH), BF16),
        scratch_shapes=[pltpu.VMEM((C_WIDTH, C_WIDTH), F32)],
        compiler_params=_cparams(("parallel", "arbitrary")),
        name="ret_main",
    )(z3, z3, z3, z3, rb, *consts)


def _post_body(h_ref, oa_ref, ob_ref, oc_ref, p_ref, wo_ref, lm_ref, w1_ref, w2_ref, lp_ref,
               wg_ref, wp_ref, y_ref):
    def rms(x, g_ref):
        ms = jnp.mean(x * x, axis=-1, keepdims=True)
        return (x * lax.rsqrt(ms + EPS) * g_ref[...]).astype(BF16)

    h = h_ref[...]
    h = h + (_dot(oa_ref[...], wo_ref[0:A_WIDTH, :])
             + _dot(ob_ref[...], wo_ref[A_WIDTH:A_WIDTH + B_WIDTH, :])
             + _dot(oc_ref[...], wo_ref[A_WIDTH + B_WIDTH:, :]))
    m = rms(h, lm_ref)
    ff = D_FF // 4
    mlp = None
    for j in range(4):
        hid = _dot(m, w1_ref[:, ff * j:ff * (j + 1)])
        hid = jnp.square(jnp.maximum(hid, 0.0)).astype(BF16)
        part = _dot(hid, w2_ref[ff * j:ff * (j + 1), :])
        mlp = part if mlp is None else mlp + part
    h = h + mlp
    gate = _sigmoid(_dot(rms(h, lp_ref), wg_ref[...]))
    y_ref[...] = h + gate * _dot(p_ref[...].astype(BF16), wp_ref[...])


def _post(h2, oa, ob, oc, ple, wo, lm, w1, w2, lp, wg, wp, tm):
    n = h2.shape[0]

    def tok(width):
        return pl.BlockSpec((tm, width), lambda i: (i, 0))

    consts = [wo, lm, w1, w2, lp, wg, wp]
    return pl.pallas_call(
        _post_body,
        grid=(n // tm,),
        in_specs=[tok(D_MODEL), tok(A_WIDTH), tok(B_WIDTH), tok(C_WIDTH), tok(PLE_DIM)]
        + [_const_spec(a.shape) for a in consts],
        out_specs=tok(D_MODEL),
        out_shape=jax.ShapeDtypeStruct((n, D_MODEL), F32),
        compiler_params=_cparams(("parallel",)),
        name="post",
    )(h2, oa, ob, oc, ple, *consts)


def _rope_tables(seq, rot_dim, theta):
    half = rot_dim // 2
    inv_freq = 1.0 / (theta ** (jnp.arange(half, dtype=F32) * (2.0 / rot_dim)))
    ang = jnp.arange(seq, dtype=F32)[:, None] * inv_freq[None, :]
    cos, sin = jnp.cos(ang), jnp.sin(ang)
    pad = HEAD_DIM - rot_dim
    c = jnp.concatenate([cos, cos, jnp.ones((seq, pad), F32)], axis=1)
    sn = jnp.concatenate([-sin, jnp.zeros((seq, half + pad), F32)], axis=1)
    sp = jnp.concatenate([jnp.zeros((seq, half), F32), sin, jnp.zeros((seq, pad), F32)], axis=1)
    return jnp.concatenate([jnp.tile(t, (1, LANES // HEAD_DIM)) for t in (c, sn, sp)], axis=1)


def _block_ones(rows, cols, rblk, cblk):
    r = jnp.arange(rows)[:, None] // rblk
    c = jnp.arange(cols)[None, :] // cblk
    return r == c


def _layer_consts(i, ln_mix, w_in, attn_q_norm, attn_k_norm, gla_gate_up, gla_gate_bias, gla_out_norm,
                  ret_decay_raw, ret_out_norm, w_out, ln_mlp, w_mlp_in, w_mlp_out, ln_pe, w_pe_gate,
                  w_pe_proj):
    w = w_in[i]
    w_p = jnp.concatenate([w[:, :2304], w[:, 2336:N_IN], w[:, 2304:2336],
                           jnp.zeros((D_MODEL, W_WIDTH - N_IN), F32)], axis=1).astype(BF16)
    gup = gla_gate_up[i].astype(BF16)
    gup_bd = jnp.zeros((LANES, 2 * B_QK), BF16)
    gup_bd = gup_bd.at[0:B_GATE_RANK, 0:B_QK].set(gup[0])
    gup_bd = gup_bd.at[B_GATE_RANK:2 * B_GATE_RANK, B_QK:].set(gup[1])
    return dict(
        ln_mix=ln_mix[i][None, :], w_p=w_p,
        qg=jnp.tile(attn_q_norm[i], 4)[None, :], kg=jnp.tile(attn_k_norm[i], 4)[None, :],
        gup_bd=gup_bd, gbias=gla_gate_bias[i].reshape(1, 2 * B_QK),
        gla_gain=gla_out_norm[i][None, :],
        raw256=jnp.repeat(ret_decay_raw[i], C_DIM, axis=1),
        raw512=jnp.repeat(ret_decay_raw[i], RET_CHUNK, axis=1),
        ret_gain=ret_out_norm[i][None, :],
        wo=w_out[i].astype(BF16), lm=ln_mlp[i][None, :], w1=w_mlp_in[i].astype(BF16),
        w2=w_mlp_out[i].astype(BF16), lp=ln_pe[i][None, :], wg=w_pe_gate[i].astype(BF16),
        wp=w_pe_proj[i].astype(BF16),
    )


def _shared_consts():
    tri = jnp.arange(GLA_CHUNK)[:, None] >= jnp.arange(GLA_CHUNK)[None, :]
    return dict(
        ones256=_block_ones(256, 256, HEAD_DIM, HEAD_DIM).astype(BF16),
        tri3=jnp.tile(tri, (1, 3)).astype(BF16),
        emask=_block_ones(B_WIDTH, B_QK, HEAD_DIM, B_KDIM).astype(F32),
        estack=_gla_diag_selector().astype(BF16),
    )


def _gla_diag_selector():
    r = jnp.arange(GLA_SUB * B_QK)
    c = jnp.arange(B_HEADS * GLA_CHUNK)
    same_s = (r // B_QK)[:, None] == (c % GLA_SUB)[None, :]
    same_head = ((r % B_QK) // B_KDIM)[:, None] == ((c % (B_HEADS * GLA_SUB)) // GLA_SUB)[None, :]
    return same_s & same_head


def _run_trunk(x, p, layers, shared):
    b, seq, _ = x.shape
    n = b * seq
    tm = 512
    tb = 512
    rope_a = _rope_tables(seq, ROPE_DIM, ROPE_THETA)
    rope_c = _rope_tables(seq, C_DIM, RET_THETA)
    h = x.reshape(n, D_MODEL)
    for i, lc in enumerate(layers):
        a, z = _proj_in(h, b, seq, lc["ln_mix"], lc["w_p"], shared["ones256"], lc["qg"], lc["kg"],
                        rope_a, rope_c, tm)
        z3 = z.reshape(b, seq, Z_WIDTH)
        oa = _attn(a)
        ob = _gla(z3, lc["gup_bd"], lc["gbias"], shared["tri3"], shared["emask"], shared["estack"],
                  shared["ones256"], lc["gla_gain"], tb)
        oc = _ret(z3, lc["raw256"], lc["raw512"], shared["ones256"], lc["ret_gain"], tb)
        h = _post(h, oa.reshape(n, A_WIDTH), ob.reshape(n, B_WIDTH), oc.reshape(n, C_WIDTH),
                  p[i].reshape(n, PLE_DIM), lc["wo"], lc["lm"], lc["w1"], lc["w2"], lc["lp"],
                  lc["wg"], lc["wp"], tm)
    return h.reshape(b, seq, D_MODEL)


def kernel(x_prompt, x_sample, p_prompt, p_sample, ln_mix, w_in, attn_q_norm, attn_k_norm, gla_gate_up, gla_gate_bias, gla_out_norm, ret_decay_raw, ret_out_norm, w_out, ln_mlp, w_mlp_in, w_mlp_out, ln_pe, w_pe_gate, w_pe_proj):
    depth = w_in.shape[0]
    layers = [_layer_consts(i, ln_mix, w_in, attn_q_norm, attn_k_norm, gla_gate_up, gla_gate_bias,
                            gla_out_norm, ret_decay_raw, ret_out_norm, w_out, ln_mlp, w_mlp_in,
                            w_mlp_out, ln_pe, w_pe_gate, w_pe_proj) for i in range(depth)]
    shared = _shared_consts()
    y_prompt = _run_trunk(x_prompt, p_prompt, layers, shared)
    y_sample = _run_trunk(x_sample, p_sample, layers, shared)
    return (y_prompt, y_sample)
```

```python
import functools

import jax
import jax.numpy as jnp
from jax import lax
from jax.experimental import pallas as pl
from jax.experimental.pallas import tpu as pltpu

F32 = jnp.float32
BF16 = jnp.bfloat16

D_MODEL = 1024
HEAD_DIM = 64
A_HEADS = 8
A_WIDTH = 512
A_PATTERNS = ((128, 1), (512, 4), (2048, 16))
A_RADIUS = 64
ROPE_THETA = 500000.0
ROPE_DIM = 16
B_HEADS = 4
B_KDIM = 32
B_QK = 128
B_WIDTH = 256
B_GATE_RANK = 16
B_GATE_TAU = 16.0
C_HEADS = 4
C_DIM = 64
C_WIDTH = 256
RET_THETA = 10000.0
N_IN = 3360
D_FF = 4096
PLE_DIM = 256
EPS = 1e-6
NEG = -1e30
LOG2E = 1.4426950408889634

LANES = 128
VMEM_LIMIT = 56 * 1024 * 1024

Z_BQ, Z_BK, Z_BV, Z_BR = 0, 128, 256, 512
Z_CQ, Z_CK, Z_CV, Z_CG = 768, 1024, 1280, 1536
Z_BG = 1792
Z_WIDTH = 1920
A_SLABS = 3 * A_WIDTH // LANES
W_WIDTH = A_SLABS * LANES + Z_WIDTH

GLA_CHUNK = 64
GLA_SUB = 16
GLA_GROUP = 4
RET_CHUNK = 128
ATT_TILE = 2048
ATT_SUB = 128
ATT_KEYS = ATT_SUB + 2 * A_RADIUS
ATT_SKEW = 2


def _cparams(sem):
    return pltpu.CompilerParams(dimension_semantics=sem, vmem_limit_bytes=VMEM_LIMIT)


def _const_spec(shape):
    nd = len(shape)
    return pl.BlockSpec(shape, lambda *_: (0,) * nd, pipeline_mode=pl.Buffered(1))


def _sigmoid(x):
    return 1.0 / (1.0 + jnp.exp(-x))


def _log_sigmoid(x):
    return jnp.minimum(x, 0.0) - jnp.log1p(jnp.exp(-jnp.abs(x)))


def _iota(shape, dim):
    return lax.broadcasted_iota(jnp.int32, shape, dim)


def _dot(a, b):
    return jnp.dot(a, b, preferred_element_type=F32)


def _dot_nt(a, b):
    return lax.dot_general(a, b, (((1,), (1,)), ((), ())), preferred_element_type=F32)


def _dot_tn(a, b):
    return lax.dot_general(a, b, (((0,), (0,)), ((), ())), preferred_element_type=F32)


def _head_norm(o, ones_ref, gain):
    ssum = _dot((o * o).astype(BF16), ones_ref[...])
    return o * lax.rsqrt(ssum * (1.0 / HEAD_DIM) + EPS) * gain


def _rope(y, tab_ref, shift):
    c = tab_ref[:, 0:LANES]
    sn = tab_ref[:, LANES:2 * LANES]
    sp = tab_ref[:, 2 * LANES:3 * LANES]
    outs = []
    for j in range(y.shape[1] // LANES):
        yj = y[:, j * LANES:(j + 1) * LANES]
        outs.append(yj * c + pltpu.roll(yj, LANES - shift, 1) * sn + pltpu.roll(yj, shift, 1) * sp)
    return jnp.concatenate(outs, axis=1)


def _proj_in_body(x_ref, g_ref, w_ref, ones_ref, qg_ref, kg_ref, ra_ref, rc_ref, *out_and_scratch):
    npat = len(A_PATTERNS)
    a_refs = out_and_scratch[:npat]
    z_ref, stage_ref = out_and_scratch[npat:]
    x = x_ref[...]
    ms = jnp.mean(x * x, axis=-1, keepdims=True)
    u = (x * lax.rsqrt(ms + EPS) * g_ref[...]).astype(BF16)
    w0 = A_SLABS * LANES

    def proj(a, b):
        return _dot(u, w_ref[:, a:b])

    def zproj(a, b):
        return proj(w0 + a, w0 + b)

    def qk_norm(y, gain_ref):
        halves = []
        for j in range(2):
            yj = y[:, 256 * j:256 * (j + 1)]
            halves.append(_head_norm(yj, ones_ref, gain_ref[...]))
        return jnp.concatenate(halves, axis=1)

    def put_slabs(first, y):
        rows = y.shape[0]
        for j in range(A_WIDTH // LANES):
            slab = first + j
            stage_ref[slab] = y[:, LANES * j:LANES * (j + 1)]
            for a_ref, (_, dil) in zip(a_refs, A_PATTERNS):
                for r in range(dil):
                    a_ref[slab, :, LANES * r:LANES * (r + 1)] = stage_ref[
                        slab, pl.ds(r, rows // dil, stride=dil), :].astype(BF16)

    def put_z(a, b, y):
        z_ref[:, a:b] = y

    segments = [
        (lambda: proj(0, A_WIDTH),
         lambda y: put_slabs(0, _rope(qk_norm(y, qg_ref), ra_ref, ROPE_DIM // 2)
                             * (HEAD_DIM ** -0.5 * LOG2E))),
        (lambda: proj(A_WIDTH, 2 * A_WIDTH),
         lambda y: put_slabs(4, _rope(qk_norm(y, kg_ref), ra_ref, ROPE_DIM // 2))),
        (lambda: proj(2 * A_WIDTH, 3 * A_WIDTH), lambda y: put_slabs(8, y)),
        (lambda: zproj(Z_BQ, Z_BK), lambda y: put_z(Z_BQ, Z_BK, y * (B_KDIM ** -0.5))),
        (lambda: zproj(Z_BK, Z_CQ), lambda y: put_z(Z_BK, Z_CQ, y)),
        (lambda: zproj(Z_CQ, Z_CK), lambda y: put_z(Z_CQ, Z_CK, _rope(y, rc_ref, C_DIM // 2))),
        (lambda: zproj(Z_CK, Z_CV),
         lambda y: put_z(Z_CK, Z_CV, _rope(y, rc_ref, C_DIM // 2) * (C_DIM ** -0.5))),
        (lambda: zproj(Z_CV, Z_WIDTH), lambda y: put_z(Z_CV, Z_WIDTH, y)),
    ]
    pending = segments[0][0]()
    for i, (_, epilogue) in enumerate(segments):
        nxt = segments[i + 1][0]() if i + 1 < len(segments) else None
        epilogue(pending)
        pending = nxt


def _proj_in(x2, batch, seq, ln, w_p, ones256, qg, kg, rope_a, rope_c, tm):
    n = x2.shape[0]
    per_seq = seq // tm
    return pl.pallas_call(
        _proj_in_body,
        grid=(n // tm,),
        in_specs=[
            pl.BlockSpec((tm, D_MODEL), lambda i: (i, 0)),
            _const_spec((1, D_MODEL)),
            _const_spec((D_MODEL, W_WIDTH)),
            _const_spec((256, 256)),
            _const_spec((1, 256)),
            _const_spec((1, 256)),
            pl.BlockSpec((tm, 3 * LANES), lambda i: (i % per_seq, 0)),
            pl.BlockSpec((tm, 3 * LANES), lambda i: (i % per_seq, 0)),
        ],
        out_specs=[
            pl.BlockSpec((None, A_SLABS, tm // dil, LANES * dil),
                         lambda i: (i // per_seq, 0, i % per_seq, 0)) for _, dil in A_PATTERNS
        ] + [pl.BlockSpec((tm, Z_WIDTH), lambda i: (i, 0))],
        out_shape=[
            jax.ShapeDtypeStruct((batch, A_SLABS, seq // dil, LANES * dil), BF16) for _, dil in A_PATTERNS
        ] + [jax.ShapeDtypeStruct((n, Z_WIDTH), F32)],
        scratch_shapes=[pltpu.VMEM((A_SLABS, tm, LANES), F32)],
        compiler_params=_cparams(("parallel",)),
        name="proj_in",
    )(x2, ln, w_p, ones256, qg, kg, rope_a, rope_c)


def _attn_body(seq, *refs):
    npat = len(A_PATTERNS)
    ins = [refs[7 * p:7 * p + 7] for p in range(npat)]
    o_ref, m_ref, l_ref, acc_ref, band_ref = refs[7 * npat:]
    t = ATT_TILE
    sub = ATT_SUB
    tile_start = pl.program_id(2) * t

    low_half = _iota((sub, LANES), 1) < HEAD_DIM
    a_idx = _iota((2 * sub, ATT_KEYS), 0) & (sub - 1)
    rel = _iota((2 * sub, ATT_KEYS), 1) - a_idx
    band_ref[...] = jnp.where((rel >= 0) & (rel <= 2 * A_RADIUS), 0.0, NEG)
    c_row = _iota((1, ATT_KEYS), 1)
    ones_v = jnp.ones((ATT_KEYS, LANES), BF16)

    def key_rows(prev_ref, cur_ref, next_ref, j, dil, lanes):
        per_res = t // dil
        lo, hi = sub * j - A_RADIUS, sub * j + sub + A_RADIUS
        parts = []
        if lo < 0:
            parts.append(prev_ref[:, lanes])
        parts.append(cur_ref[max(lo, 0):min(hi, per_res), lanes])
        if hi > per_res:
            parts.append(next_ref[:, lanes])
        return parts[0] if len(parts) == 1 else jnp.concatenate(parts, axis=0)

    def scores_stage(tile):
        pat, r, j, dil = tile
        lanes = slice(LANES * r, LANES * (r + 1))
        q_ref, kp_ref, kc_ref, kn_ref = ins[pat][0:4]
        q_sub = q_ref[sub * j:sub * (j + 1), lanes]
        zero = jnp.zeros_like(q_sub)
        q_st = jnp.concatenate([jnp.where(low_half, q_sub, zero), jnp.where(low_half, zero, q_sub)], axis=0)
        return _dot_nt(q_st, key_rows(kp_ref, kc_ref, kn_ref, j, dil, lanes))

    def values_stage(tile, s):
        pat, r, j, dil = tile
        lanes = slice(LANES * r, LANES * (r + 1))
        key_pos = (tile_start + r + dil * (sub * j - A_RADIUS)) + dil * c_row
        bias = jnp.where((key_pos >= 0) & (key_pos < seq), 0.0, NEG)
        s = (s + band_ref[...]) + bias
        m_row = jnp.broadcast_to(jnp.max(s, axis=1, keepdims=True), (2 * sub, LANES))
        p = jnp.exp2(s - jnp.concatenate([m_row] * (ATT_KEYS // LANES), axis=1)).astype(BF16)
        v_sub = key_rows(*ins[pat][4:7], j, dil, lanes)
        return m_row, _dot(p, jnp.concatenate([v_sub, ones_v], axis=1))

    def store_stage(tile, m_row, pvl):
        pat, r, j, dil = tile
        rw = pl.ds(r + dil * sub * j, sub, stride=dil)
        m_ref[pat, rw, :] = jnp.where(low_half, m_row[0:sub], m_row[sub:])
        l_ref[pat, rw, :] = jnp.where(low_half, pvl[0:sub, LANES:], pvl[sub:, LANES:])
        acc_ref[pat, rw, :] = jnp.where(low_half, pvl[0:sub, 0:LANES], pvl[sub:, 0:LANES])

    tiles = [(pat, r, j, dil)
             for pat, (_, dil) in enumerate(A_PATTERNS)
             for r in range(dil) for j in range(t // (sub * dil))]
    scored, valued = {}, {}
    for step in range(len(tiles) + 2 * ATT_SKEW):
        i_store, i_val = step - 2 * ATT_SKEW, step - ATT_SKEW
        if 0 <= i_store < len(tiles):
            store_stage(tiles[i_store], *valued.pop(i_store))
        if 0 <= i_val < len(tiles):
            valued[i_val] = values_stage(tiles[i_val], scored.pop(i_val))
        if step < len(tiles):
            scored[step] = scores_stage(tiles[step])

    blk = 2 * sub
    for i in range(t // blk):
        rs = pl.ds(i * blk, blk)
        m_g = [m_ref[g, rs, :] for g in range(npat)]
        m_all = functools.reduce(jnp.maximum, m_g)
        w_g = [jnp.exp2(m - m_all) for m in m_g]
        l_all = functools.reduce(lambda a, b: a + b, [w * l_ref[g, rs, :] for g, w in enumerate(w_g)])
        num = functools.reduce(lambda a, b: a + b, [w * acc_ref[g, rs, :] for g, w in enumerate(w_g)])
        o_ref[rs, :] = (num / l_all).astype(o_ref.dtype)


def _attn(a_by_pattern):
    b, _, seq, _ = a_by_pattern[0].shape
    t = ATT_TILE
    nt = seq // t
    hps = A_HEADS // 2
    views, specs = [], []
    for a, (_, dil) in zip(a_by_pattern, A_PATTERNS):
        width, rows, nrows = LANES * dil, t // dil, seq // dil
        per, last = rows // A_RADIUS, nrows // A_RADIUS - 1

        def cur(c0, rows=rows, width=width):
            return pl.BlockSpec((None, None, rows, width), lambda bi, hp, i: (bi, c0 + hp, i, 0))

        def prev(c0, width=width, per=per):
            return pl.BlockSpec((None, None, A_RADIUS, width),
                                lambda bi, hp, i: (bi, c0 + hp, jnp.maximum(i * per - 1, 0), 0))

        def nxt(c0, width=width, per=per, last=last):
            return pl.BlockSpec((None, None, A_RADIUS, width),
                                lambda bi, hp, i: (bi, c0 + hp, jnp.minimum((i + 1) * per, last), 0))

        specs += [cur(0), prev(hps), cur(hps), nxt(hps), prev(2 * hps), cur(2 * hps), nxt(2 * hps)]
        views += [a] * 7

    return pl.pallas_call(
        functools.partial(_attn_body, seq),
        grid=(b, hps, nt),
        in_specs=specs,
        out_specs=pl.BlockSpec((None, t, LANES), lambda bi, hp, i: (bi, i, hp)),
        out_shape=jax.ShapeDtypeStruct((b, seq, A_WIDTH), BF16),
        scratch_shapes=[
            pltpu.VMEM((len(A_PATTERNS), t, LANES), F32),
            pltpu.VMEM((len(A_PATTERNS), t, LANES), F32),
            pltpu.VMEM((len(A_PATTERNS), t, LANES), F32),
            pltpu.VMEM((2 * ATT_SUB, ATT_KEYS), F32),
        ],
        compiler_params=_cparams(("parallel", "parallel", "parallel")),
        name="attn",
    )(*views)


def _split3(x):
    hi = x.astype(BF16)
    r1 = x - hi.astype(F32)
    mid = r1.astype(BF16)
    lo = (r1 - mid.astype(F32)).astype(BF16)
    return hi, mid, lo


def _gla_log_decay(g, gup_ref, gbias_ref):
    logits = _dot(g.astype(BF16), gup_ref[...]) + gbias_ref[...]
    return _log_sigmoid(logits) * (1.0 / B_GATE_TAU)


def _cumsum_rows(la, tri3_ref):
    hi, mid, lo = _split3(la)
    return _dot(tri3_ref[...], jnp.concatenate([hi, mid, lo], axis=0))


def _gla_bwd_body(nit, k_ref, v_ref, g_ref, gup_ref, gbias_ref, tri3_ref, emask_ref, sb_ref, state):
    @pl.when(pl.program_id(1) == 0)
    def _():
        state[...] = jnp.zeros_like(state)

    c = GLA_CHUNK
    grp = GLA_GROUP
    span = grp * c

    def step(i, carry):
        it = nit - 1 - i
        rows = pl.ds(pl.multiple_of(it * span, span), span)
        logits = _dot(g_ref[rows, :].astype(BF16), gup_ref[:, B_QK:]) + gbias_ref[:, B_QK:]
        la = _log_sigmoid(logits) * (1.0 / B_GATE_TAU)
        cums = [_cumsum_rows(la[c * j:c * (j + 1)], tri3_ref) for j in range(grp)]
        cbx = jnp.concatenate(cums, axis=0) - la
        kt = (k_ref[rows, :] * jnp.exp(cbx)).astype(BF16)
        vb = v_ref[rows, :].astype(BF16)
        upd = [_dot_tn(vb[c * j:c * (j + 1)], kt[c * j:c * (j + 1)]) for j in range(grp)]
        st = state[...]
        for j in reversed(range(grp)):
            sb_ref[it * grp + j] = st.astype(BF16)
            st = st * jnp.exp(cums[j][c - 1:c, :]) + upd[j] * emask_ref[...]
        state[...] = st
        return carry

    lax.fori_loop(0, nit, step, 0)


def _gla_main_body(nit, q_ref, k_ref, v_ref, r_ref, g_ref, sb_ref, gup_ref, gbias_ref, tri3_ref,
                   emask_ref, estack_ref, ones_ref, gain_ref, o_ref, state):
    @pl.when(pl.program_id(1) == 0)
    def _():
        state[...] = jnp.zeros_like(state)

    c = GLA_CHUNK
    n = GLA_SUB
    nsub = c // n
    grp = GLA_GROUP
    span = grp * c
    row = _iota((c, B_QK), 0)
    row_8 = _iota((8, B_QK), 0)
    rho = _iota((B_HEADS * n, B_QK), 0)
    head_k_ok = (rho >> 4) == (_iota((B_HEADS * n, B_QK), 1) >> 5)
    rho_v = _iota((B_HEADS * n, B_WIDTH), 0)
    head_v_ok = (rho_v >> 4) == (_iota((B_HEADS * n, B_WIDTH), 1) >> 6)
    own_blk = ((_iota((span, B_HEADS * c), 0) & (c - 1)) >> 4) == (_iota((span, B_HEADS * c), 1) >> 6)
    zeros_k = jnp.zeros((B_HEADS * n, B_QK), F32)

    def rows_of(vals, height):
        return jnp.concatenate([jnp.broadcast_to(x, (height, x.shape[1])) for x in vals], axis=0)

    def by_head(x, ok):
        return jnp.where(ok, jnp.concatenate([x] * B_HEADS, axis=0), 0.0)

    def diag_lhs(q, k, bf2, cb2):
        nblk = span // n
        slabs = []
        for s in range(n):
            ks = rows_of([k[n * i + s:n * i + s + 1] for i in range(nblk)], n)
            pieces = []
            for i in range(nblk):
                ref_f = bf2[n * i + s:n * i + s + 1]
                ref_b = cb2[n * i + s:n * i + s + 1]
                for r0 in range(0, n, 8):
                    rs = slice(n * i + r0, n * i + r0 + 8)
                    if s <= r0:
                        pieces.append(bf2[rs] - ref_f)
                    elif s >= r0 + 8:
                        pieces.append(ref_b - cb2[rs])
                    else:
                        pieces.append(jnp.where(row_8 >= s - r0, bf2[rs] - ref_f, ref_b - cb2[rs]))
            arg = jnp.concatenate(pieces, axis=0)
            slabs.append(((q * ks) * jnp.exp2(arg)).astype(BF16))
        return jnp.concatenate(slabs, axis=1)

    def chunk_local(q, k, v, bf, cbx, kf, kb, att_diag):
        e_rows = [bf[n * j + n - 1:n * j + n, :] for j in range(nsub)]
        f_rows = [cbx[n * j:n * j + 1, :] for j in range(nsub)]
        lhs_parts = []
        for j in range(nsub - 1):
            lhs_parts.append(jnp.where(row >= n * (j + 1),
                                       q * jnp.exp(jnp.minimum(bf - e_rows[j], 0.0)), 0.0))
        for j in range(1, nsub):
            lhs_parts.append(jnp.where(row < n * j, q * jnp.exp(jnp.minimum(f_rows[j] - cbx, 0.0)), 0.0))
        rhs_rows = []
        for j in range(nsub):
            kfj = by_head(kf[n * j:n * (j + 1)], head_k_ok)
            kbj = by_head(kb[n * j:n * (j + 1)], head_k_ok)
            parts = [kfj if (jj == j and j < nsub - 1) else zeros_k for jj in range(nsub - 1)]
            parts += [kbj if (jj == j and j > 0) else zeros_k for jj in range(1, nsub)]
            rhs_rows.append(jnp.concatenate(parts, axis=1))
        att = _dot_nt(jnp.concatenate(lhs_parts, axis=1).astype(BF16),
                      jnp.concatenate(rhs_rows, axis=0).astype(BF16))
        v4 = jnp.concatenate([by_head(v[n * j:n * (j + 1)], head_v_ok) for j in range(nsub)],
                             axis=0).astype(BF16)
        return _dot((att + att_diag).astype(BF16), v4)

    def step(it, carry):
        rows = pl.ds(pl.multiple_of(it * span, span), span)
        q = q_ref[rows, :]
        k = k_ref[rows, :]
        v = v_ref[rows, :]
        la = _gla_log_decay(g_ref[rows, :], gup_ref, gbias_ref)
        cums = [_cumsum_rows(la[c * j:c * (j + 1)], tri3_ref) for j in range(grp)]
        cum = jnp.concatenate(cums, axis=0)
        bf = cum[:, :B_QK]
        cbx = cum[:, B_QK:] - la[:, B_QK:]
        tot_f = [cj[c - 1:c, :B_QK] for cj in cums]
        tot_b = [cj[c - 1:c, B_QK:] for cj in cums]
        bf2 = bf * LOG2E
        cb2 = cbx * LOG2E
        e_blk = rows_of([bf[n * j + n - 1:n * j + n, :] for j in range(span // n)], n)
        f_blk = rows_of([cbx[n * j:n * j + 1, :] for j in range(span // n)], n)
        kf = k * jnp.exp(e_blk - bf)
        kb = k * jnp.exp(cbx - f_blk)
        kt = (k * jnp.exp(rows_of(tot_f, c) - bf)).astype(BF16)
        lhs_inter = jnp.concatenate([q * jnp.exp(bf), q * jnp.exp(rows_of(tot_b, c) - cbx)],
                                    axis=1).astype(BF16)
        vb = v.astype(BF16)
        att_diag = jnp.where(own_blk, _dot(diag_lhs(q, k, bf2, cb2), estack_ref[...]), 0.0)

        outs = []
        upd = []
        for j in range(grp):
            sl = slice(c * j, c * (j + 1))
            outs.append(chunk_local(q[sl], k[sl], v[sl], bf[sl], cbx[sl], kf[sl], kb[sl], att_diag[sl]))
            upd.append(_dot_tn(vb[sl], kt[sl]))

        st = state[...]
        for j in range(grp):
            sl = slice(c * j, c * (j + 1))
            rhs = jnp.concatenate([st.astype(BF16), sb_ref[it * grp + j]], axis=1)
            outs[j] = outs[j] + _dot_nt(lhs_inter[sl], rhs)
            st = st * jnp.exp(tot_f[j]) + upd[j] * emask_ref[...]
        state[...] = st

        gate = r_ref[rows, :]
        o = jnp.concatenate(outs, axis=0)
        out = _head_norm(o, ones_ref, gain_ref[...]) * (gate * _sigmoid(gate))
        o_ref[rows, :] = out.astype(o_ref.dtype)
        return carry

    lax.fori_loop(0, nit, step, 0)


def _gla(z3, gup_bd, gbias, tri3, emask, estack, ones256, gain, tb):
    b, seq, _ = z3.shape
    nblk = seq // tb
    nch = tb // GLA_CHUNK
    ntot = seq // GLA_CHUNK

    def zspec(col, width, rev):
        blk = col // width
        if rev:
            return pl.BlockSpec((None, tb, width), lambda bi, i: (bi, nblk - 1 - i, blk))
        return pl.BlockSpec((None, tb, width), lambda bi, i: (bi, i, blk))

    consts = [gup_bd, gbias, tri3, emask]
    const_specs = [_const_spec(a.shape) for a in consts]
    nit = nch // GLA_GROUP
    sb = pl.pallas_call(
        functools.partial(_gla_bwd_body, nit),
        grid=(b, nblk),
        in_specs=[zspec(Z_BK, B_QK, True), zspec(Z_BV, B_WIDTH, True), zspec(Z_BG, LANES, True)]
        + const_specs,
        out_specs=pl.BlockSpec((None, nch, B_WIDTH, B_QK), lambda bi, i: (bi, nblk - 1 - i, 0, 0)),
        out_shape=jax.ShapeDtypeStruct((b, ntot, B_WIDTH, B_QK), BF16),
        scratch_shapes=[pltpu.VMEM((B_WIDTH, B_QK), F32)],
        compiler_params=_cparams(("parallel", "arbitrary")),
        name="gla_bwd",
    )(z3, z3, z3, *consts)

    consts2 = [gup_bd, gbias, tri3, emask, estack, ones256, gain]
    return pl.pallas_call(
        functools.partial(_gla_main_body, nit),
        grid=(b, nblk),
        in_specs=[zspec(Z_BQ, B_QK, False), zspec(Z_BK, B_QK, False), zspec(Z_BV, B_WIDTH, False),
                  zspec(Z_BR, B_WIDTH, False), zspec(Z_BG, LANES, False),
                  pl.BlockSpec((None, nch, B_WIDTH, B_QK), lambda bi, i: (bi, i, 0, 0))]
        + [_const_spec(a.shape) for a in consts2],
        out_specs=pl.BlockSpec((None, tb, B_WIDTH), lambda bi, i: (bi, i, 0)),
        out_shape=jax.ShapeDtypeStruct((b, seq, B_WIDTH), BF16),
        scratch_shapes=[pltpu.VMEM((B_WIDTH, B_QK), F32)],
        compiler_params=_cparams(("parallel", "arbitrary")),
        name="gla_main",
    )(z3, z3, z3, z3, z3, sb, *consts2)


def _ret_bwd_body(nch, k_ref, v_ref, raw_ref, bmask_ref, rb_ref, state):
    @pl.when(pl.program_id(1) == 0)
    def _():
        state[...] = jnp.zeros_like(state)

    c = RET_CHUNK
    lg1 = _log_sigmoid(raw_ref[1:2, :])
    pos = _iota((c, C_WIDTH), 0).astype(F32)
    kdec = jnp.exp(pos * lg1)
    chunk_decay = jnp.exp(float(c) * lg1)

    upd = []
    for j in range(nch):
        rows = pl.ds(c * j, c)
        kt = (k_ref[rows, :] * kdec).astype(BF16)
        upd.append(_dot_tn(kt, v_ref[rows, :].astype(BF16)))
    st = state[...]
    for j in reversed(range(nch)):
        rb_ref[j] = st.astype(BF16)
        st = chunk_decay * st + upd[j] * bmask_ref[...]
    state[...] = st


def _ret_main_body(nch, q_ref, k_ref, v_ref, g_ref, rb_ref, raw_ref, raw_s_ref, bmask_ref, ones_ref,
                   gain_ref, o_ref, state):
    @pl.when(pl.program_id(1) == 0)
    def _():
        state[...] = jnp.zeros_like(state)

    c = RET_CHUNK
    lg = _log_sigmoid(raw_ref[...])
    lg0, lg1 = lg[0:1, :], lg[1:2, :]
    lgs = _log_sigmoid(raw_s_ref[...])
    pos = _iota((c, C_WIDTH), 0).astype(F32)
    qdec_f = jnp.exp((pos + 1.0) * lg0)
    qdec_b = jnp.exp((float(c) - pos) * lg1)
    kdec_f = jnp.exp((float(c) - 1.0 - pos) * lg0)
    chunk_decay = jnp.exp(float(c) * lg0)
    t_idx = _iota((c, C_HEADS * c), 0)
    s_idx = _iota((c, C_HEADS * c), 1) & (c - 1)
    rel = (t_idx - s_idx).astype(F32)
    dmat = jnp.where(rel >= 0.0, jnp.exp(jnp.maximum(rel, 0.0) * lgs[0:1, :]),
                     jnp.exp(jnp.maximum(-rel, 0.0) * lgs[1:2, :]))
    rho = _iota((C_HEADS * c, C_WIDTH), 0)
    lane = _iota((C_HEADS * c, C_WIDTH), 1)
    head_ok = (rho >> 7) == (lane >> 6)

    outs, lhs_inter, upd = [], [], []
    for j in range(nch):
        rows = pl.ds(c * j, c)
        q = q_ref[rows, :]
        k = k_ref[rows, :]
        v = v_ref[rows, :]
        k4 = jnp.where(head_ok, jnp.concatenate([k] * C_HEADS, axis=0), 0.0).astype(BF16)
        v4 = jnp.where(head_ok, jnp.concatenate([v] * C_HEADS, axis=0), 0.0).astype(BF16)
        scores = _dot_nt(q.astype(BF16), k4) * dmat
        outs.append(_dot(scores.astype(BF16), v4))
        lhs_inter.append(jnp.concatenate([q * qdec_f, q * qdec_b], axis=1).astype(BF16))
        upd.append(_dot_tn((k * kdec_f).astype(BF16), v.astype(BF16)))
    st = state[...]
    for j in range(nch):
        rhs = jnp.concatenate([st.astype(BF16), rb_ref[j]], axis=0)
        outs[j] = outs[j] + _dot(lhs_inter[j], rhs)
        st = chunk_decay * st + upd[j] * bmask_ref[...]
    state[...] = st
    for j in range(nch):
        rows = pl.ds(c * j, c)
        gate = g_ref[rows, :]
        out = _head_norm(outs[j], ones_ref, gain_ref[...]) * (gate * _sigmoid(gate))
        o_ref[rows, :] = out.astype(o_ref.dtype)


def _ret(z3, raw256, raw512, ones256, gain, tb):
    b, seq, _ = z3.shape
    nblk = seq // tb
    nch = tb // RET_CHUNK
    ntot = seq // RET_CHUNK

    def zspec(col, rev):
        blk = col // C_WIDTH
        if rev:
            return pl.BlockSpec((None, tb, C_WIDTH), lambda bi, i: (bi, nblk - 1 - i, blk))
        return pl.BlockSpec((None, tb, C_WIDTH), lambda bi, i: (bi, i, blk))

    rb = pl.pallas_call(
        functools.partial(_ret_bwd_body, nch),
        grid=(b, nblk),
        in_specs=[zspec(Z_CK, True), zspec(Z_CV, True), _const_spec(raw256.shape),
                  _const_spec(ones256.shape)],
        out_specs=pl.BlockSpec((None, nch, C_WIDTH, C_WIDTH), lambda bi, i: (bi, nblk - 1 - i, 0, 0)),
        out_shape=jax.ShapeDtypeStruct((b, ntot, C_WIDTH, C_WIDTH), BF16),
        scratch_shapes=[pltpu.VMEM((C_WIDTH, C_WIDTH), F32)],
        compiler_params=_cparams(("parallel", "arbitrary")),
        name="ret_bwd",
    )(z3, z3, raw256, ones256)

    consts = [raw256, raw512, ones256, ones256, gain]
    return pl.pallas_call(
        functools.partial(_ret_main_body, nch),
        grid=(b, nblk),
        in_specs=[zspec(Z_CQ, False), zspec(Z_CK, False), zspec(Z_CV, False), zspec(Z_CG, False),
                  pl.BlockSpec((None, nch, C_WIDTH, C_WIDTH), lambda bi, i: (bi, i, 0, 0))]
        + [_const_spec(a.shape) for a in consts],
        out_specs=pl.BlockSpec((None, tb, C_WIDTH), lambda bi, i: (bi, i, 0)),
        out_shape=jax.ShapeDtypeStruct((b, seq, C_WIDTH), BF16),
        scratch_shapes=[pltpu.VMEM((C_WIDTH, C_WIDTH), F32)],
        compiler_params=_cparams(("parallel", "arbitrary")),
        name="ret_main",
    )(z3, z3, z3, z3, rb, *consts)


def _post_body(h_ref, oa_ref, ob_ref, oc_ref, p_ref, wo_ref, lm_ref, w1_ref, w2_ref, lp_ref,
               wg_ref, wp_ref, y_ref):
    def rms(x, g_ref):
        ms = jnp.mean(x * x, axis=-1, keepdims=True)
        return (x * lax.rsqrt(ms + EPS) * g_ref[...]).astype(BF16)

    h = h_ref[...]
    h = h + (_dot(oa_ref[...], wo_ref[0:A_WIDTH, :])
             + _dot(ob_ref[...], wo_ref[A_WIDTH:A_WIDTH + B_WIDTH, :])
             + _dot(oc_ref[...], wo_ref[A_WIDTH + B_WIDTH:, :]))
    m = rms(h, lm_ref)
    ff = D_FF // 4
    mlp = None
    for j in range(4):
        hid = _dot(m, w1_ref[:, ff * j:ff * (j + 1)])
        hid = jnp.square(jnp.maximum(hid, 0.0)).astype(BF16)
        part = _dot(hid, w2_ref[ff * j:ff * (j + 1), :])
        mlp = part if mlp is None else mlp + part
    h = h + mlp
    gate = _sigmoid(_dot(rms(h, lp_ref), wg_ref[...]))
    y_ref[...] = h + gate * _dot(p_ref[...].astype(BF16), wp_ref[...])


def _post(h2, oa, ob, oc, ple, wo, lm, w1, w2, lp, wg, wp, tm):
    n = h2.shape[0]

    def tok(width):
        return pl.BlockSpec((tm, width), lambda i: (i, 0))

    consts = [wo, lm, w1, w2, lp, wg, wp]
    return pl.pallas_call(
        _post_body,
        grid=(n // tm,),
        in_specs=[tok(D_MODEL), tok(A_WIDTH), tok(B_WIDTH), tok(C_WIDTH), tok(PLE_DIM)]
        + [_const_spec(a.shape) for a in consts],
        out_specs=tok(D_MODEL),
        out_shape=jax.ShapeDtypeStruct((n, D_MODEL), F32),
        compiler_params=_cparams(("parallel",)),
        name="post",
    )(h2, oa, ob, oc, ple, *consts)


def _rope_tables(seq, rot_dim, theta):
    half = rot_dim // 2
    inv_freq = 1.0 / (theta ** (jnp.arange(half, dtype=F32) * (2.0 / rot_dim)))
    ang = jnp.arange(seq, dtype=F32)[:, None] * inv_freq[None, :]
    cos, sin = jnp.cos(ang), jnp.sin(ang)
    pad = HEAD_DIM - rot_dim
    c = jnp.concatenate([cos, cos, jnp.ones((seq, pad), F32)], axis=1)
    sn = jnp.concatenate([-sin, jnp.zeros((seq, half + pad), F32)], axis=1)
    sp = jnp.concatenate([jnp.zeros((seq, half), F32), sin, jnp.zeros((seq, pad), F32)], axis=1)
    return jnp.concatenate([jnp.tile(t, (1, LANES // HEAD_DIM)) for t in (c, sn, sp)], axis=1)


def _block_ones(rows, cols, rblk, cblk):
    r = jnp.arange(rows)[:, None] // rblk
    c = jnp.arange(cols)[None, :] // cblk
    return r == c


def _layer_consts(i, ln_mix, w_in, attn_q_norm, attn_k_norm, gla_gate_up, gla_gate_bias, gla_out_norm,
                  ret_decay_raw, ret_out_norm, w_out, ln_mlp, w_mlp_in, w_mlp_out, ln_pe, w_pe_gate,
                  w_pe_proj):
    w = w_in[i]
    w_p = jnp.concatenate([w[:, :2304], w[:, 2336:N_IN], w[:, 2304:2336],
                           jnp.zeros((D_MODEL, W_WIDTH - N_IN), F32)], axis=1).astype(BF16)
    gup = gla_gate_up[i].astype(BF16)
    gup_bd = jnp.zeros((LANES, 2 * B_QK), BF16)
    gup_bd = gup_bd.at[0:B_GATE_RANK, 0:B_QK].set(gup[0])
    gup_bd = gup_bd.at[B_GATE_RANK:2 * B_GATE_RANK, B_QK:].set(gup[1])
    return dict(
        ln_mix=ln_mix[i][None, :], w_p=w_p,
        qg=jnp.tile(attn_q_norm[i], 4)[None, :], kg=jnp.tile(attn_k_norm[i], 4)[None, :],
        gup_bd=gup_bd, gbias=gla_gate_bias[i].reshape(1, 2 * B_QK),
        gla_gain=gla_out_norm[i][None, :],
        raw256=jnp.repeat(ret_decay_raw[i], C_DIM, axis=1),
        raw512=jnp.repeat(ret_decay_raw[i], RET_CHUNK, axis=1),
        ret_gain=ret_out_norm[i][None, :],
        wo=w_out[i].astype(BF16), lm=ln_mlp[i][None, :], w1=w_mlp_in[i].astype(BF16),
        w2=w_mlp_out[i].astype(BF16), lp=ln_pe[i][None, :], wg=w_pe_gate[i].astype(BF16),
        wp=w_pe_proj[i].astype(BF16),
    )


def _shared_consts():
    tri = jnp.arange(GLA_CHUNK)[:, None] >= jnp.arange(GLA_CHUNK)[None, :]
    return dict(
        ones256=_block_ones(256, 256, HEAD_DIM, HEAD_DIM).astype(BF16),
        tri3=jnp.tile(tri, (1, 3)).astype(BF16),
        emask=_block_ones(B_WIDTH, B_QK, HEAD_DIM, B_KDIM).astype(F32),
        estack=_gla_diag_selector().astype(BF16),
    )


def _gla_diag_selector():
    r = jnp.arange(GLA_SUB * B_QK)
    c = jnp.arange(B_HEADS * GLA_CHUNK)
    same_s = (r // B_QK)[:, None] == (c % GLA_SUB)[None, :]
    same_head = ((r % B_QK) // B_KDIM)[:, None] == ((c % (B_HEADS * GLA_SUB)) // GLA_SUB)[None, :]
    return same_s & same_head


def _run_trunk(x, p, layers, shared):
    b, seq, _ = x.shape
    n = b * seq
    tm = 512
    tb = 512
    rope_a = _rope_tables(seq, ROPE_DIM, ROPE_THETA)
    rope_c = _rope_tables(seq, C_DIM, RET_THETA)
    h = x.reshape(n, D_MODEL)
    for i, lc in enumerate(layers):
        *a_by_pattern, z = _proj_in(h, b, seq, lc["ln_mix"], lc["w_p"], shared["ones256"], lc["qg"],
                                    lc["kg"], rope_a, rope_c, tm)
        z3 = z.reshape(b, seq, Z_WIDTH)
        oa = _attn(a_by_pattern)
        ob = _gla(z3, lc["gup_bd"], lc["gbias"], shared["tri3"], shared["emask"], shared["estack"],
                  shared["ones256"], lc["gla_gain"], tb)
        oc = _ret(z3, lc["raw256"], lc["raw512"], shared["ones256"], lc["ret_gain"], tb)
        h = _post(h, oa.reshape(n, A_WIDTH), ob.reshape(n, B_WIDTH), oc.reshape(n, C_WIDTH),
                  p[i].reshape(n, PLE_DIM), lc["wo"], lc["lm"], lc["w1"], lc["w2"], lc["lp"],
                  lc["wg"], lc["wp"], tm)
    return h.reshape(b, seq, D_MODEL)


def kernel(x_prompt, x_sample, p_prompt, p_sample, ln_mix, w_in, attn_q_norm, attn_k_norm, gla_gate_up, gla_gate_bias, gla_out_norm, ret_decay_raw, ret_out_norm, w_out, ln_mlp, w_mlp_in, w_mlp_out, ln_pe, w_pe_gate, w_pe_proj):
    depth = w_in.shape[0]
    layers = [_layer_consts(i, ln_mix, w_in, attn_q_norm, attn_k_norm, gla_gate_up, gla_gate_bias,
                            gla_out_norm, ret_decay_raw, ret_out_norm, w_out, ln_mlp, w_mlp_in,
                            w_mlp_out, ln_pe, w_pe_gate, w_pe_proj) for i in range(depth)]
    shared = _shared_consts()
    y_prompt = _run_trunk(x_prompt, p_prompt, layers, shared)
    y_sample = _run_trunk(x_sample, p_sample, layers, shared)
    return (y_prompt, y_sample)
```

```python
import functools

import jax
import jax.numpy as jnp
from jax import lax
from jax.experimental import pallas as pl
from jax.experimental.pallas import tpu as pltpu

F32 = jnp.float32
BF16 = jnp.bfloat16

D_MODEL = 1024
HEAD_DIM = 64
A_HEADS = 8
A_WIDTH = 512
A_PATTERNS = ((128, 1), (512, 4), (2048, 16))
A_RADIUS = 64
A_MAX_REACH = 1024
ROPE_THETA = 500000.0
ROPE_DIM = 16
B_HEADS = 4
B_KDIM = 32
B_QK = 128
B_WIDTH = 256
B_GATE_RANK = 16
B_GATE_TAU = 16.0
C_HEADS = 4
C_DIM = 64
C_WIDTH = 256
RET_THETA = 10000.0
N_IN = 3360
D_FF = 4096
PLE_DIM = 256
EPS = 1e-6
NEG = -1e30
LOG2E = 1.4426950408889634

LANES = 128
VMEM_LIMIT = 56 * 1024 * 1024

Z_AQ, Z_AK, Z_AV = 0, 512, 1024
Z_BQ, Z_BK, Z_BV, Z_BR = 1536, 1664, 1792, 2048
Z_CQ, Z_CK, Z_CV, Z_CG = 2304, 2560, 2816, 3072
Z_BG = 3328
Z_WIDTH = 3456

GLA_CHUNK = 64
GLA_SUB = 16
GLA_GROUP = 8
RET_CHUNK = 128
ATT_TILE = 2048
ATT_SUB = 128
ATT_KEYS = ATT_SUB + 2 * A_RADIUS
ATT_SKEW = 2


def _cparams(sem):
    return pltpu.CompilerParams(dimension_semantics=sem, vmem_limit_bytes=VMEM_LIMIT)


def _const_spec(shape):
    nd = len(shape)
    return pl.BlockSpec(shape, lambda *_: (0,) * nd, pipeline_mode=pl.Buffered(1))


def _sigmoid(x):
    return 1.0 / (1.0 + jnp.exp(-x))


def _log_sigmoid(x):
    return jnp.minimum(x, 0.0) - jnp.log1p(jnp.exp(-jnp.abs(x)))


def _iota(shape, dim):
    return lax.broadcasted_iota(jnp.int32, shape, dim)


def _dot(a, b):
    return jnp.dot(a, b, preferred_element_type=F32)


def _dot_nt(a, b):
    return lax.dot_general(a, b, (((1,), (1,)), ((), ())), preferred_element_type=F32)


def _dot_tn(a, b):
    return lax.dot_general(a, b, (((0,), (0,)), ((), ())), preferred_element_type=F32)


def _head_norm(o, ones_ref, gain):
    ssum = _dot((o * o).astype(BF16), ones_ref[...])
    return o * lax.rsqrt(ssum * (1.0 / HEAD_DIM) + EPS) * gain


def _rope(y, tab_ref, shift):
    c = tab_ref[:, 0:LANES]
    sn = tab_ref[:, LANES:2 * LANES]
    sp = tab_ref[:, 2 * LANES:3 * LANES]
    outs = []
    for j in range(y.shape[1] // LANES):
        yj = y[:, j * LANES:(j + 1) * LANES]
        outs.append(yj * c + pltpu.roll(yj, LANES - shift, 1) * sn + pltpu.roll(yj, shift, 1) * sp)
    return jnp.concatenate(outs, axis=1)


def _proj_in_body(x_ref, g_ref, w_ref, ones_ref, qg_ref, kg_ref, ra_ref, rc_ref, z_ref):
    x = x_ref[...]
    ms = jnp.mean(x * x, axis=-1, keepdims=True)
    u = (x * lax.rsqrt(ms + EPS) * g_ref[...]).astype(BF16)

    def proj(a, b):
        return _dot(u, w_ref[:, a:b])

    def qk_norm(y, gain_ref):
        halves = []
        for j in range(2):
            yj = y[:, 256 * j:256 * (j + 1)]
            halves.append(_head_norm(yj, ones_ref, gain_ref[...]))
        return jnp.concatenate(halves, axis=1)

    aq = _rope(qk_norm(proj(Z_AQ, Z_AK), qg_ref), ra_ref, ROPE_DIM // 2)
    z_ref[:, Z_AQ:Z_AK] = aq * (HEAD_DIM ** -0.5 * LOG2E)
    z_ref[:, Z_AK:Z_AV] = _rope(qk_norm(proj(Z_AK, Z_AV), kg_ref), ra_ref, ROPE_DIM // 2)
    z_ref[:, Z_AV:Z_BQ] = proj(Z_AV, Z_BQ)
    z_ref[:, Z_BQ:Z_BK] = proj(Z_BQ, Z_BK) * (B_KDIM ** -0.5)
    z_ref[:, Z_BK:Z_CQ] = proj(Z_BK, Z_CQ)
    z_ref[:, Z_CQ:Z_CK] = _rope(proj(Z_CQ, Z_CK), rc_ref, C_DIM // 2)
    z_ref[:, Z_CK:Z_CV] = _rope(proj(Z_CK, Z_CV), rc_ref, C_DIM // 2) * (C_DIM ** -0.5)
    z_ref[:, Z_CV:Z_WIDTH] = proj(Z_CV, Z_WIDTH)


def _proj_in(x2, seq, ln, w_p, ones256, qg, kg, rope_a, rope_c, tm):
    n = x2.shape[0]
    per_seq = seq // tm
    return pl.pallas_call(
        _proj_in_body,
        grid=(n // tm,),
        in_specs=[
            pl.BlockSpec((tm, D_MODEL), lambda i: (i, 0)),
            _const_spec((1, D_MODEL)),
            _const_spec((D_MODEL, Z_WIDTH)),
            _const_spec((256, 256)),
            _const_spec((1, 256)),
            _const_spec((1, 256)),
            pl.BlockSpec((tm, 3 * LANES), lambda i: (i % per_seq, 0)),
            pl.BlockSpec((tm, 3 * LANES), lambda i: (i % per_seq, 0)),
        ],
        out_specs=pl.BlockSpec((tm, Z_WIDTH), lambda i: (i, 0)),
        out_shape=jax.ShapeDtypeStruct((n, Z_WIDTH), F32),
        compiler_params=_cparams(("parallel",)),
        name="proj_in",
    )(x2, ln, w_p, ones256, qg, kg, rope_a, rope_c)


def _attn_body(seq, q_ref, kp_ref, kc_ref, kn_ref, vp_ref, vc_ref, vn_ref, o_ref,
               kbuf, vbuf, m_ref, l_ref, acc_ref, band_ref):
    t = ATT_TILE
    halo = A_MAX_REACH
    sub = ATT_SUB
    tile_start = pl.program_id(2) * t
    kbuf[0:halo, :] = kp_ref[...]
    kbuf[halo:halo + t, :] = kc_ref[...]
    kbuf[halo + t:, :] = kn_ref[...]
    vbuf[0:halo, :] = vp_ref[...]
    vbuf[halo:halo + t, :] = vc_ref[...]
    vbuf[halo + t:, :] = vn_ref[...]

    low_half = _iota((sub, LANES), 1) < HEAD_DIM
    a_idx = _iota((2 * sub, ATT_KEYS), 0) & (sub - 1)
    rel = _iota((2 * sub, ATT_KEYS), 1) - a_idx
    band_ref[...] = jnp.where((rel >= 0) & (rel <= 2 * A_RADIUS), 0.0, NEG)
    c_row = _iota((1, ATT_KEYS), 1)
    ones_v = jnp.ones((ATT_KEYS, LANES), BF16)

    def scores_stage(tile):
        pat, qs, dil = tile
        span = A_RADIUS * dil
        q_sub = q_ref[pl.ds(qs, sub, stride=dil), :]
        q_st = jnp.concatenate([jnp.where(low_half, q_sub, 0.0),
                                jnp.where(low_half, 0.0, q_sub)], axis=0).astype(BF16)
        k_sub = kbuf[pl.ds(halo + qs - span, ATT_KEYS, stride=dil), :].astype(BF16)
        s = _dot_nt(q_st, k_sub)
        key_pos = (tile_start + qs - span) + dil * c_row
        bias = jnp.where((key_pos >= 0) & (key_pos < seq), 0.0, NEG)
        s = (s + band_ref[...]) + bias
        return s, jnp.broadcast_to(jnp.max(s, axis=1, keepdims=True), (2 * sub, LANES))

    def values_stage(tile, s, m_row):
        pat, qs, dil = tile
        p = jnp.exp2(s - jnp.concatenate([m_row] * (ATT_KEYS // LANES), axis=1)).astype(BF16)
        v_sub = vbuf[pl.ds(halo + qs - A_RADIUS * dil, ATT_KEYS, stride=dil), :].astype(BF16)
        return _dot(p, jnp.concatenate([v_sub, ones_v], axis=1))

    def store_stage(tile, m_row, pvl):
        pat, qs, dil = tile
        rw = pl.ds(qs, sub, stride=dil)
        m_ref[pat, rw, :] = jnp.where(low_half, m_row[0:sub], m_row[sub:])
        l_ref[pat, rw, :] = jnp.where(low_half, pvl[0:sub, LANES:], pvl[sub:, LANES:])
        acc_ref[pat, rw, :] = jnp.where(low_half, pvl[0:sub, 0:LANES], pvl[sub:, 0:LANES])

    tiles = [(pat, r + j * sub * dil, dil)
             for pat, (_, dil) in enumerate(A_PATTERNS)
             for r in range(dil) for j in range(t // (sub * dil))]
    scored, valued = {}, {}
    for step in range(len(tiles) + 2 * ATT_SKEW):
        i_store, i_val = step - 2 * ATT_SKEW, step - ATT_SKEW
        if 0 <= i_store < len(tiles):
            store_stage(tiles[i_store], *valued.pop(i_store))
        if 0 <= i_val < len(tiles):
            s, m_row = scored.pop(i_val)
            valued[i_val] = (m_row, values_stage(tiles[i_val], s, m_row))
        if step < len(tiles):
            scored[step] = scores_stage(tiles[step])

    npat = len(A_PATTERNS)
    blk = 2 * sub
    for i in range(t // blk):
        rs = pl.ds(i * blk, blk)
        m_g = [m_ref[g, rs, :] for g in range(npat)]
        m_all = functools.reduce(jnp.maximum, m_g)
        w_g = [jnp.exp2(m - m_all) for m in m_g]
        l_all = functools.reduce(lambda a, b: a + b, [w * l_ref[g, rs, :] for g, w in enumerate(w_g)])
        num = functools.reduce(lambda a, b: a + b, [w * acc_ref[g, rs, :] for g, w in enumerate(w_g)])
        o_ref[rs, :] = (num / l_all).astype(o_ref.dtype)


def _attn(z3):
    b, seq, _ = z3.shape
    t = ATT_TILE
    halo = A_MAX_REACH
    nt = seq // t
    per_tile = t // halo
    n_halo = seq // halo
    qc, kc, vc = Z_AQ // LANES, Z_AK // LANES, Z_AV // LANES

    def cur(c0):
        return pl.BlockSpec((None, t, LANES), lambda bi, hp, i: (bi, i, c0 + hp))

    def prev(c0):
        return pl.BlockSpec((None, halo, LANES),
                            lambda bi, hp, i: (bi, jnp.maximum(i * per_tile - 1, 0), c0 + hp))

    def nxt(c0):
        return pl.BlockSpec((None, halo, LANES),
                            lambda bi, hp, i: (bi, jnp.minimum((i + 1) * per_tile, n_halo - 1), c0 + hp))

    return pl.pallas_call(
        functools.partial(_attn_body, seq),
        grid=(b, A_HEADS // 2, nt),
        in_specs=[cur(qc), prev(kc), cur(kc), nxt(kc), prev(vc), cur(vc), nxt(vc)],
        out_specs=pl.BlockSpec((None, t, LANES), lambda bi, hp, i: (bi, i, hp)),
        out_shape=jax.ShapeDtypeStruct((b, seq, A_WIDTH), BF16),
        scratch_shapes=[
            pltpu.VMEM((t + 2 * halo, LANES), F32),
            pltpu.VMEM((t + 2 * halo, LANES), F32),
            pltpu.VMEM((len(A_PATTERNS), t, LANES), F32),
            pltpu.VMEM((len(A_PATTERNS), t, LANES), F32),
            pltpu.VMEM((len(A_PATTERNS), t, LANES), F32),
            pltpu.VMEM((2 * ATT_SUB, ATT_KEYS), F32),
        ],
        compiler_params=_cparams(("parallel", "parallel", "parallel")),
        name="attn",
    )(z3, z3, z3, z3, z3, z3, z3)


def _split3(x):
    hi = x.astype(BF16)
    r1 = x - hi.astype(F32)
    mid = r1.astype(BF16)
    lo = (r1 - mid.astype(F32)).astype(BF16)
    return hi, mid, lo


def _gla_log_decay(g, gup_ref, gbias_ref):
    logits = _dot(g.astype(BF16), gup_ref[...]) + gbias_ref[...]
    return _log_sigmoid(logits) * (1.0 / B_GATE_TAU)


def _cumsum_rows(la, tri3_ref):
    hi, mid, lo = _split3(la)
    return _dot(tri3_ref[...], jnp.concatenate([hi, mid, lo], axis=0))


def _gla_bwd_body(nit, k_ref, v_ref, g_ref, gup_ref, gbias_ref, tri3_ref, emask_ref, sb_ref, state):
    @pl.when(pl.program_id(1) == 0)
    def _():
        state[...] = jnp.zeros_like(state)

    c = GLA_CHUNK
    grp = GLA_GROUP
    span = grp * c

    def step(i, carry):
        it = nit - 1 - i
        rows = pl.ds(pl.multiple_of(it * span, span), span)
        logits = _dot(g_ref[rows, :].astype(BF16), gup_ref[:, B_QK:]) + gbias_ref[:, B_QK:]
        la = _log_sigmoid(logits) * (1.0 / B_GATE_TAU)
        cums = [_cumsum_rows(la[c * j:c * (j + 1)], tri3_ref) for j in range(grp)]
        cbx = jnp.concatenate(cums, axis=0) - la
        kt = (k_ref[rows, :] * jnp.exp(cbx)).astype(BF16)
        vb = v_ref[rows, :].astype(BF16)
        upd = [_dot_tn(vb[c * j:c * (j + 1)], kt[c * j:c * (j + 1)]) for j in range(grp)]
        st = state[...]
        for j in reversed(range(grp)):
            sb_ref[it * grp + j] = st.astype(BF16)
            st = st * jnp.exp(cums[j][c - 1:c, :]) + upd[j] * emask_ref[...]
        state[...] = st
        return carry

    lax.fori_loop(0, nit, step, 0)


def _gla_main_body(nit, q_ref, k_ref, v_ref, r_ref, g_ref, sb_ref, gup_ref, gbias_ref, tri3_ref,
                   emask_ref, estack_ref, ones_ref, gain_ref, o_ref, state):
    @pl.when(pl.program_id(1) == 0)
    def _():
        state[...] = jnp.zeros_like(state)

    c = GLA_CHUNK
    n = GLA_SUB
    nsub = c // n
    grp = GLA_GROUP
    span = grp * c
    row = _iota((c, B_QK), 0)
    row_8 = _iota((8, B_QK), 0)
    rho = _iota((B_HEADS * n, B_QK), 0)
    head_k_ok = (rho >> 4) == (_iota((B_HEADS * n, B_QK), 1) >> 5)
    rho_v = _iota((B_HEADS * n, B_WIDTH), 0)
    head_v_ok = (rho_v >> 4) == (_iota((B_HEADS * n, B_WIDTH), 1) >> 6)
    own_blk = ((_iota((span, B_HEADS * c), 0) & (c - 1)) >> 4) == (_iota((span, B_HEADS * c), 1) >> 6)
    zeros_k = jnp.zeros((B_HEADS * n, B_QK), F32)

    def rows_of(vals, height):
        return jnp.concatenate([jnp.broadcast_to(x, (height, x.shape[1])) for x in vals], axis=0)

    def by_head(x, ok):
        return jnp.where(ok, jnp.concatenate([x] * B_HEADS, axis=0), 0.0)

    def diag_lhs(q, k, bf2, cb2):
        nblk = span // n
        slabs = []
        for s in range(n):
            ks = rows_of([k[n * i + s:n * i + s + 1] for i in range(nblk)], n)
            pieces = []
            for i in range(nblk):
                ref_f = bf2[n * i + s:n * i + s + 1]
                ref_b = cb2[n * i + s:n * i + s + 1]
                for r0 in range(0, n, 8):
                    rs = slice(n * i + r0, n * i + r0 + 8)
                    if s <= r0:
                        pieces.append(bf2[rs] - ref_f)
                    elif s >= r0 + 8:
                        pieces.append(ref_b - cb2[rs])
                    else:
                        pieces.append(jnp.where(row_8 >= s - r0, bf2[rs] - ref_f, ref_b - cb2[rs]))
            arg = jnp.concatenate(pieces, axis=0)
            slabs.append(((q * ks) * jnp.exp2(arg)).astype(BF16))
        return jnp.concatenate(slabs, axis=1)

    def chunk_local(q, k, v, bf, cbx, kf, kb, att_diag):
        e_rows = [bf[n * j + n - 1:n * j + n, :] for j in range(nsub)]
        f_rows = [cbx[n * j:n * j + 1, :] for j in range(nsub)]
        lhs_parts = []
        for j in range(nsub - 1):
            lhs_parts.append(jnp.where(row >= n * (j + 1),
                                       q * jnp.exp(jnp.minimum(bf - e_rows[j], 0.0)), 0.0))
        for j in range(1, nsub):
            lhs_parts.append(jnp.where(row < n * j, q * jnp.exp(jnp.minimum(f_rows[j] - cbx, 0.0)), 0.0))
        rhs_rows = []
        for j in range(nsub):
            kfj = by_head(kf[n * j:n * (j + 1)], head_k_ok)
            kbj = by_head(kb[n * j:n * (j + 1)], head_k_ok)
            parts = [kfj if (jj == j and j < nsub - 1) else zeros_k for jj in range(nsub - 1)]
            parts += [kbj if (jj == j and j > 0) else zeros_k for jj in range(1, nsub)]
            rhs_rows.append(jnp.concatenate(parts, axis=1))
        att = _dot_nt(jnp.concatenate(lhs_parts, axis=1).astype(BF16),
                      jnp.concatenate(rhs_rows, axis=0).astype(BF16))
        v4 = jnp.concatenate([by_head(v[n * j:n * (j + 1)], head_v_ok) for j in range(nsub)],
                             axis=0).astype(BF16)
        return _dot((att + att_diag).astype(BF16), v4)

    def step(it, carry):
        rows = pl.ds(pl.multiple_of(it * span, span), span)
        q = q_ref[rows, :]
        k = k_ref[rows, :]
        v = v_ref[rows, :]
        la = _gla_log_decay(g_ref[rows, :], gup_ref, gbias_ref)
        cums = [_cumsum_rows(la[c * j:c * (j + 1)], tri3_ref) for j in range(grp)]
        cum = jnp.concatenate(cums, axis=0)
        bf = cum[:, :B_QK]
        cbx = cum[:, B_QK:] - la[:, B_QK:]
        tot_f = [cj[c - 1:c, :B_QK] for cj in cums]
        tot_b = [cj[c - 1:c, B_QK:] for cj in cums]
        bf2 = bf * LOG2E
        cb2 = cbx * LOG2E
        e_blk = rows_of([bf[n * j + n - 1:n * j + n, :] for j in range(span // n)], n)
        f_blk = rows_of([cbx[n * j:n * j + 1, :] for j in range(span // n)], n)
        kf = k * jnp.exp(e_blk - bf)
        kb = k * jnp.exp(cbx - f_blk)
        kt = (k * jnp.exp(rows_of(tot_f, c) - bf)).astype(BF16)
        lhs_inter = jnp.concatenate([q * jnp.exp(bf), q * jnp.exp(rows_of(tot_b, c) - cbx)],
                                    axis=1).astype(BF16)
        vb = v.astype(BF16)
        att_diag = jnp.where(own_blk, _dot(diag_lhs(q, k, bf2, cb2), estack_ref[...]), 0.0)

        outs = []
        upd = []
        for j in range(grp):
            sl = slice(c * j, c * (j + 1))
            outs.append(chunk_local(q[sl], k[sl], v[sl], bf[sl], cbx[sl], kf[sl], kb[sl], att_diag[sl]))
            upd.append(_dot_tn(vb[sl], kt[sl]))

        st = state[...]
        for j in range(grp):
            sl = slice(c * j, c * (j + 1))
            rhs = jnp.concatenate([st.astype(BF16), sb_ref[it * grp + j]], axis=1)
            outs[j] = outs[j] + _dot_nt(lhs_inter[sl], rhs)
            st = st * jnp.exp(tot_f[j]) + upd[j] * emask_ref[...]
        state[...] = st

        gate = r_ref[rows, :]
        o = jnp.concatenate(outs, axis=0)
        out = _head_norm(o, ones_ref, gain_ref[...]) * (gate * _sigmoid(gate))
        o_ref[rows, :] = out.astype(o_ref.dtype)
        return carry

    lax.fori_loop(0, nit, step, 0)


def _gla(z3, gup_bd, gbias, tri3, emask, estack, ones256, gain, tb):
    b, seq, _ = z3.shape
    nblk = seq // tb
    nch = tb // GLA_CHUNK
    ntot = seq // GLA_CHUNK

    def zspec(col, width, rev):
        blk = col // width
        if rev:
            return pl.BlockSpec((None, tb, width), lambda bi, i: (bi, nblk - 1 - i, blk))
        return pl.BlockSpec((None, tb, width), lambda bi, i: (bi, i, blk))

    consts = [gup_bd, gbias, tri3, emask]
    const_specs = [_const_spec(a.shape) for a in consts]
    nit = nch // GLA_GROUP
    sb = pl.pallas_call(
        functools.partial(_gla_bwd_body, nit),
        grid=(b, nblk),
        in_specs=[zspec(Z_BK, B_QK, True), zspec(Z_BV, B_WIDTH, True), zspec(Z_BG, LANES, True)]
        + const_specs,
        out_specs=pl.BlockSpec((None, nch, B_WIDTH, B_QK), lambda bi, i: (bi, nblk - 1 - i, 0, 0)),
        out_shape=jax.ShapeDtypeStruct((b, ntot, B_WIDTH, B_QK), BF16),
        scratch_shapes=[pltpu.VMEM((B_WIDTH, B_QK), F32)],
        compiler_params=_cparams(("parallel", "arbitrary")),
        name="gla_bwd",
    )(z3, z3, z3, *consts)

    consts2 = [gup_bd, gbias, tri3, emask, estack, ones256, gain]
    return pl.pallas_call(
        functools.partial(_gla_main_body, nit),
        grid=(b, nblk),
        in_specs=[zspec(Z_BQ, B_QK, False), zspec(Z_BK, B_QK, False), zspec(Z_BV, B_WIDTH, False),
                  zspec(Z_BR, B_WIDTH, False), zspec(Z_BG, LANES, False),
                  pl.BlockSpec((None, nch, B_WIDTH, B_QK), lambda bi, i: (bi, i, 0, 0))]
        + [_const_spec(a.shape) for a in consts2],
        out_specs=pl.BlockSpec((None, tb, B_WIDTH), lambda bi, i: (bi, i, 0)),
        out_shape=jax.ShapeDtypeStruct((b, seq, B_WIDTH), BF16),
        scratch_shapes=[pltpu.VMEM((B_WIDTH, B_QK), F32)],
        compiler_params=_cparams(("parallel", "arbitrary")),
        name="gla_main",
    )(z3, z3, z3, z3, z3, sb, *consts2)


def _ret_bwd_body(nch, k_ref, v_ref, raw_ref, bmask_ref, rb_ref, state):
    @pl.when(pl.program_id(1) == 0)
    def _():
        state[...] = jnp.zeros_like(state)

    c = RET_CHUNK
    lg1 = _log_sigmoid(raw_ref[1:2, :])
    pos = _iota((c, C_WIDTH), 0).astype(F32)
    kdec = jnp.exp(pos * lg1)
    chunk_decay = jnp.exp(float(c) * lg1)

    upd = []
    for j in range(nch):
        rows = pl.ds(c * j, c)
        kt = (k_ref[rows, :] * kdec).astype(BF16)
        upd.append(_dot_tn(kt, v_ref[rows, :].astype(BF16)))
    st = state[...]
    for j in reversed(range(nch)):
        rb_ref[j] = st.astype(BF16)
        st = chunk_decay * st + upd[j] * bmask_ref[...]
    state[...] = st


def _ret_main_body(nch, q_ref, k_ref, v_ref, g_ref, rb_ref, raw_ref, raw_s_ref, bmask_ref, ones_ref,
                   gain_ref, o_ref, state):
    @pl.when(pl.program_id(1) == 0)
    def _():
        state[...] = jnp.zeros_like(state)

    c = RET_CHUNK
    lg = _log_sigmoid(raw_ref[...])
    lg0, lg1 = lg[0:1, :], lg[1:2, :]
    lgs = _log_sigmoid(raw_s_ref[...])
    pos = _iota((c, C_WIDTH), 0).astype(F32)
    qdec_f = jnp.exp((pos + 1.0) * lg0)
    qdec_b = jnp.exp((float(c) - pos) * lg1)
    kdec_f = jnp.exp((float(c) - 1.0 - pos) * lg0)
    chunk_decay = jnp.exp(float(c) * lg0)
    t_idx = _iota((c, C_HEADS * c), 0)
    s_idx = _iota((c, C_HEADS * c), 1) & (c - 1)
    rel = (t_idx - s_idx).astype(F32)
    dmat = jnp.where(rel >= 0.0, jnp.exp(jnp.maximum(rel, 0.0) * lgs[0:1, :]),
                     jnp.exp(jnp.maximum(-rel, 0.0) * lgs[1:2, :]))
    rho = _iota((C_HEADS * c, C_WIDTH), 0)
    lane = _iota((C_HEADS * c, C_WIDTH), 1)
    head_ok = (rho >> 7) == (lane >> 6)

    outs, lhs_inter, upd = [], [], []
    for j in range(nch):
        rows = pl.ds(c * j, c)
        q = q_ref[rows, :]
        k = k_ref[rows, :]
        v = v_ref[rows, :]
        k4 = jnp.where(head_ok, jnp.concatenate([k] * C_HEADS, axis=0), 0.0).astype(BF16)
        v4 = jnp.where(head_ok, jnp.concatenate([v] * C_HEADS, axis=0), 0.0).astype(BF16)
        scores = _dot_nt(q.astype(BF16), k4) * dmat
        outs.append(_dot(scores.astype(BF16), v4))
        lhs_inter.append(jnp.concatenate([q * qdec_f, q * qdec_b], axis=1).astype(BF16))
        upd.append(_dot_tn((k * kdec_f).astype(BF16), v.astype(BF16)))
    st = state[...]
    for j in range(nch):
        rhs = jnp.concatenate([st.astype(BF16), rb_ref[j]], axis=0)
        outs[j] = outs[j] + _dot(lhs_inter[j], rhs)
        st = chunk_decay * st + upd[j] * bmask_ref[...]
    state[...] = st
    for j in range(nch):
        rows = pl.ds(c * j, c)
        gate = g_ref[rows, :]
        out = _head_norm(outs[j], ones_ref, gain_ref[...]) * (gate * _sigmoid(gate))
        o_ref[rows, :] = out.astype(o_ref.dtype)


def _ret(z3, raw256, raw512, ones256, gain, tb):
    b, seq, _ = z3.shape
    nblk = seq // tb
    nch = tb // RET_CHUNK
    ntot = seq // RET_CHUNK

    def zspec(col, rev):
        blk = col // C_WIDTH
        if rev:
            return pl.BlockSpec((None, tb, C_WIDTH), lambda bi, i: (bi, nblk - 1 - i, blk))
        return pl.BlockSpec((None, tb, C_WIDTH), lambda bi, i: (bi, i, blk))

    rb = pl.pallas_call(
        functools.partial(_ret_bwd_body, nch),
        grid=(b, nblk),
        in_specs=[zspec(Z_CK, True), zspec(Z_CV, True), _const_spec(raw256.shape),
                  _const_spec(ones256.shape)],
        out_specs=pl.BlockSpec((None, nch, C_WIDTH, C_WIDTH), lambda bi, i: (bi, nblk - 1 - i, 0, 0)),
        out_shape=jax.ShapeDtypeStruct((b, ntot, C_WIDTH, C_WIDTH), BF16),
        scratch_shapes=[pltpu.VMEM((C_WIDTH, C_WIDTH), F32)],
        compiler_params=_cparams(("parallel", "arbitrary")),
        name="ret_bwd",
    )(z3, z3, raw256, ones256)

    consts = [raw256, raw512, ones256, ones256, gain]
    return pl.pallas_call(
        functools.partial(_ret_main_body, nch),
        grid=(b, nblk),
        in_specs=[zspec(Z_CQ, False), zspec(Z_CK, False), zspec(Z_CV, False), zspec(Z_CG, False),
                  pl.BlockSpec((None, nch, C_WIDTH, C_WIDTH), lambda bi, i: (bi, i, 0, 0))]
        + [_const_spec(a.shape) for a in consts],
        out_specs=pl.BlockSpec((None, tb, C_WIDTH), lambda bi, i: (bi, i, 0)),
        out_shape=jax.ShapeDtypeStruct((b, seq, C_WIDTH), BF16),
        scratch_shapes=[pltpu.VMEM((C_WIDTH, C_WIDTH), F32)],
        compiler_params=_cparams(("parallel", "arbitrary")),
        name="ret_main",
    )(z3, z3, z3, z3, rb, *consts)


def _post_body(h_ref, oa_ref, ob_ref, oc_ref, p_ref, wo_ref, lm_ref, w1_ref, w2_ref, lp_ref,
               wg_ref, wp_ref, y_ref):
    def rms(x, g_ref):
        ms = jnp.mean(x * x, axis=-1, keepdims=True)
        return (x * lax.rsqrt(ms + EPS) * g_ref[...]).astype(BF16)

    h = h_ref[...]
    h = h + (_dot(oa_ref[...], wo_ref[0:A_WIDTH, :])
             + _dot(ob_ref[...], wo_ref[A_WIDTH:A_WIDTH + B_WIDTH, :])
             + _dot(oc_ref[...], wo_ref[A_WIDTH + B_WIDTH:, :]))
    m = rms(h, lm_ref)
    ff = D_FF // 4
    mlp = None
    for j in range(4):
        hid = _dot(m, w1_ref[:, ff * j:ff * (j + 1)])
        hid = jnp.square(jnp.maximum(hid, 0.0)).astype(BF16)
        part = _dot(hid, w2_ref[ff * j:ff * (j + 1), :])
        mlp = part if mlp is None else mlp + part
    h = h + mlp
    gate = _sigmoid(_dot(rms(h, lp_ref), wg_ref[...]))
    y_ref[...] = h + gate * _dot(p_ref[...].astype(BF16), wp_ref[...])


def _post(h2, oa, ob, oc, ple, wo, lm, w1, w2, lp, wg, wp, tm):
    n = h2.shape[0]

    def tok(width):
        return pl.BlockSpec((tm, width), lambda i: (i, 0))

    consts = [wo, lm, w1, w2, lp, wg, wp]
    return pl.pallas_call(
        _post_body,
        grid=(n // tm,),
        in_specs=[tok(D_MODEL), tok(A_WIDTH), tok(B_WIDTH), tok(C_WIDTH), tok(PLE_DIM)]
        + [_const_spec(a.shape) for a in consts],
        out_specs=tok(D_MODEL),
        out_shape=jax.ShapeDtypeStruct((n, D_MODEL), F32),
        compiler_params=_cparams(("parallel",)),
        name="post",
    )(h2, oa, ob, oc, ple, *consts)


def _rope_tables(seq, rot_dim, theta):
    half = rot_dim // 2
    inv_freq = 1.0 / (theta ** (jnp.arange(half, dtype=F32) * (2.0 / rot_dim)))
    ang = jnp.arange(seq, dtype=F32)[:, None] * inv_freq[None, :]
    cos, sin = jnp.cos(ang), jnp.sin(ang)
    pad = HEAD_DIM - rot_dim
    c = jnp.concatenate([cos, cos, jnp.ones((seq, pad), F32)], axis=1)
    sn = jnp.concatenate([-sin, jnp.zeros((seq, half + pad), F32)], axis=1)
    sp = jnp.concatenate([jnp.zeros((seq, half), F32), sin, jnp.zeros((seq, pad), F32)], axis=1)
    return jnp.concatenate([jnp.tile(t, (1, LANES // HEAD_DIM)) for t in (c, sn, sp)], axis=1)


def _block_ones(rows, cols, rblk, cblk):
    r = jnp.arange(rows)[:, None] // rblk
    c = jnp.arange(cols)[None, :] // cblk
    return r == c


def _layer_consts(i, ln_mix, w_in, attn_q_norm, attn_k_norm, gla_gate_up, gla_gate_bias, gla_out_norm,
                  ret_decay_raw, ret_out_norm, w_out, ln_mlp, w_mlp_in, w_mlp_out, ln_pe, w_pe_gate,
                  w_pe_proj):
    w = w_in[i]
    w_p = jnp.concatenate([w[:, :2304], w[:, 2336:N_IN], w[:, 2304:2336],
                           jnp.zeros((D_MODEL, Z_WIDTH - N_IN), F32)], axis=1).astype(BF16)
    gup = gla_gate_up[i].astype(BF16)
    gup_bd = jnp.zeros((LANES, 2 * B_QK), BF16)
    gup_bd = gup_bd.at[0:B_GATE_RANK, 0:B_QK].set(gup[0])
    gup_bd = gup_bd.at[B_GATE_RANK:2 * B_GATE_RANK, B_QK:].set(gup[1])
    return dict(
        ln_mix=ln_mix[i][None, :], w_p=w_p,
        qg=jnp.tile(attn_q_norm[i], 4)[None, :], kg=jnp.tile(attn_k_norm[i], 4)[None, :],
        gup_bd=gup_bd, gbias=gla_gate_bias[i].reshape(1, 2 * B_QK),
        gla_gain=gla_out_norm[i][None, :],
        raw256=jnp.repeat(ret_decay_raw[i], C_DIM, axis=1),
        raw512=jnp.repeat(ret_decay_raw[i], RET_CHUNK, axis=1),
        ret_gain=ret_out_norm[i][None, :],
        wo=w_out[i].astype(BF16), lm=ln_mlp[i][None, :], w1=w_mlp_in[i].astype(BF16),
        w2=w_mlp_out[i].astype(BF16), lp=ln_pe[i][None, :], wg=w_pe_gate[i].astype(BF16),
        wp=w_pe_proj[i].astype(BF16),
    )


def _shared_consts():
    tri = jnp.arange(GLA_CHUNK)[:, None] >= jnp.arange(GLA_CHUNK)[None, :]
    return dict(
        ones256=_block_ones(256, 256, HEAD_DIM, HEAD_DIM).astype(BF16),
        tri3=jnp.tile(tri, (1, 3)).astype(BF16),
        emask=_block_ones(B_WIDTH, B_QK, HEAD_DIM, B_KDIM).astype(F32),
        estack=_gla_diag_selector().astype(BF16),
    )


def _gla_diag_selector():
    r = jnp.arange(GLA_SUB * B_QK)
    c = jnp.arange(B_HEADS * GLA_CHUNK)
    same_s = (r // B_QK)[:, None] == (c % GLA_SUB)[None, :]
    same_head = ((r % B_QK) // B_KDIM)[:, None] == ((c % (B_HEADS * GLA_SUB)) // GLA_SUB)[None, :]
    return same_s & same_head


def _run_trunk(x, p, layers, shared):
    b, seq, _ = x.shape
    n = b * seq
    tm = 512
    tb = 512
    rope_a = _rope_tables(seq, ROPE_DIM, ROPE_THETA)
    rope_c = _rope_tables(seq, C_DIM, RET_THETA)
    h = x.reshape(n, D_MODEL)
    for i, lc in enumerate(layers):
        z = _proj_in(h, seq, lc["ln_mix"], lc["w_p"], shared["ones256"], lc["qg"], lc["kg"],
                     rope_a, rope_c, tm)
        z3 = z.reshape(b, seq, Z_WIDTH)
        oa = _attn(z3)
        ob = _gla(z3, lc["gup_bd"], lc["gbias"], shared["tri3"], shared["emask"], shared["estack"],
                  shared["ones256"], lc["gla_gain"], tb)
        oc = _ret(z3, lc["raw256"], lc["raw512"], shared["ones256"], lc["ret_gain"], 2 * tb)
        h = _post(h, oa.reshape(n, A_WIDTH), ob.reshape(n, B_WIDTH), oc.reshape(n, C_WIDTH),
                  p[i].reshape(n, PLE_DIM), lc["wo"], lc["lm"], lc["w1"], lc["w2"], lc["lp"],
                  lc["wg"], lc["wp"], tm)
    return h.reshape(b, seq, D_MODEL)


def kernel(x_prompt, x_sample, p_prompt, p_sample, ln_mix, w_in, attn_q_norm, attn_k_norm, gla_gate_up, gla_gate_bias, gla_out_norm, ret_decay_raw, ret_out_norm, w_out, ln_mlp, w_mlp_in, w_mlp_out, ln_pe, w_pe_gate, w_pe_proj):
    depth = w_in.shape[0]
    layers = [_layer_consts(i, ln_mix, w_in, attn_q_norm, attn_k_norm, gla_gate_up, gla_gate_bias,
                            gla_out_norm, ret_decay_raw, ret_out_norm, w_out, ln_mlp, w_mlp_in,
                            w_mlp_out, ln_pe, w_pe_gate, w_pe_proj) for i in range(depth)]
    shared = _shared_consts()
    y_prompt = _run_trunk(x_prompt, p_prompt, layers, shared)
    y_sample = _run_trunk(x_sample, p_sample, layers, shared)
    return (y_prompt, y_sample)
```

```python
import functools

import jax
import jax.numpy as jnp
from jax import lax
from jax.experimental import pallas as pl
from jax.experimental.pallas import tpu as pltpu

F32 = jnp.float32
BF16 = jnp.bfloat16

D_MODEL = 1024
HEAD_DIM = 64
A_HEADS = 8
A_WIDTH = 512
A_PATTERNS = ((128, 1), (512, 4), (2048, 16))
A_RADIUS = 64
A_MAX_REACH = 1024
ROPE_THETA = 500000.0
ROPE_DIM = 16
B_HEADS = 4
B_KDIM = 32
B_QK = 128
B_WIDTH = 256
B_GATE_RANK = 16
B_GATE_TAU = 16.0
C_HEADS = 4
C_DIM = 64
C_WIDTH = 256
RET_THETA = 10000.0
N_IN = 3360
D_FF = 4096
PLE_DIM = 256
EPS = 1e-6
NEG = -1e30
LOG2E = 1.4426950408889634

LANES = 128
VMEM_LIMIT = 56 * 1024 * 1024

Z_AQ, Z_AK, Z_AV = 0, 512, 1024
Z_BQ, Z_BK, Z_BV, Z_BR = 1536, 1664, 1792, 2048
Z_CQ, Z_CK, Z_CV, Z_CG = 2304, 2560, 2816, 3072
Z_BG = 3328
Z_WIDTH = 3456

GLA_CHUNK = 64
GLA_SUB = 16
GLA_GROUP = 8
RET_CHUNK = 128
ATT_TILE = 2048
ATT_SUB = 128
ATT_KEYS = ATT_SUB + 2 * A_RADIUS
ATT_SKEW = 2


def _cparams(sem):
    return pltpu.CompilerParams(dimension_semantics=sem, vmem_limit_bytes=VMEM_LIMIT)


def _const_spec(shape):
    nd = len(shape)
    return pl.BlockSpec(shape, lambda *_: (0,) * nd, pipeline_mode=pl.Buffered(1))


def _sigmoid(x):
    return 1.0 / (1.0 + jnp.exp(-x))


def _log_sigmoid(x):
    return jnp.minimum(x, 0.0) - jnp.log1p(jnp.exp(-jnp.abs(x)))


def _iota(shape, dim):
    return lax.broadcasted_iota(jnp.int32, shape, dim)


def _dot(a, b):
    return jnp.dot(a, b, preferred_element_type=F32)


def _dot_nt(a, b):
    return lax.dot_general(a, b, (((1,), (1,)), ((), ())), preferred_element_type=F32)


def _dot_tn(a, b):
    return lax.dot_general(a, b, (((0,), (0,)), ((), ())), preferred_element_type=F32)


def _head_norm(o, ones_ref, gain):
    ssum = _dot((o * o).astype(BF16), ones_ref[...])
    return o * lax.rsqrt(ssum * (1.0 / HEAD_DIM) + EPS) * gain


def _rope(y, tab_ref, shift):
    c = tab_ref[:, 0:LANES]
    sn = tab_ref[:, LANES:2 * LANES]
    sp = tab_ref[:, 2 * LANES:3 * LANES]
    outs = []
    for j in range(y.shape[1] // LANES):
        yj = y[:, j * LANES:(j + 1) * LANES]
        outs.append(yj * c + pltpu.roll(yj, LANES - shift, 1) * sn + pltpu.roll(yj, shift, 1) * sp)
    return jnp.concatenate(outs, axis=1)


def _proj_in_body(x_ref, g_ref, w_ref, ones_ref, qg_ref, kg_ref, ra_ref, rc_ref, z_ref):
    x = x_ref[...]
    ms = jnp.mean(x * x, axis=-1, keepdims=True)
    u = (x * lax.rsqrt(ms + EPS) * g_ref[...]).astype(BF16)

    def proj(a, b):
        return _dot(u, w_ref[:, a:b])

    def qk_norm(y, gain_ref):
        halves = []
        for j in range(2):
            yj = y[:, 256 * j:256 * (j + 1)]
            halves.append(_head_norm(yj, ones_ref, gain_ref[...]))
        return jnp.concatenate(halves, axis=1)

    aq = _rope(qk_norm(proj(Z_AQ, Z_AK), qg_ref), ra_ref, ROPE_DIM // 2)
    z_ref[:, Z_AQ:Z_AK] = aq * (HEAD_DIM ** -0.5 * LOG2E)
    z_ref[:, Z_AK:Z_AV] = _rope(qk_norm(proj(Z_AK, Z_AV), kg_ref), ra_ref, ROPE_DIM // 2)
    z_ref[:, Z_AV:Z_BQ] = proj(Z_AV, Z_BQ)
    z_ref[:, Z_BQ:Z_BK] = proj(Z_BQ, Z_BK) * (B_KDIM ** -0.5)
    z_ref[:, Z_BK:Z_CQ] = proj(Z_BK, Z_CQ)
    z_ref[:, Z_CQ:Z_CK] = _rope(proj(Z_CQ, Z_CK), rc_ref, C_DIM // 2)
    z_ref[:, Z_CK:Z_CV] = _rope(proj(Z_CK, Z_CV), rc_ref, C_DIM // 2) * (C_DIM ** -0.5)
    z_ref[:, Z_CV:Z_WIDTH] = proj(Z_CV, Z_WIDTH)


def _proj_in(x2, seq, ln, w_p, ones256, qg, kg, rope_a, rope_c, tm):
    n = x2.shape[0]
    per_seq = seq // tm
    return pl.pallas_call(
        _proj_in_body,
        grid=(n // tm,),
        in_specs=[
            pl.BlockSpec((tm, D_MODEL), lambda i: (i, 0)),
            _const_spec((1, D_MODEL)),
            _const_spec((D_MODEL, Z_WIDTH)),
            _const_spec((256, 256)),
            _const_spec((1, 256)),
            _const_spec((1, 256)),
            pl.BlockSpec((tm, 3 * LANES), lambda i: (i % per_seq, 0)),
            pl.BlockSpec((tm, 3 * LANES), lambda i: (i % per_seq, 0)),
        ],
        out_specs=pl.BlockSpec((tm, Z_WIDTH), lambda i: (i, 0)),
        out_shape=jax.ShapeDtypeStruct((n, Z_WIDTH), F32),
        compiler_params=_cparams(("parallel",)),
        name="proj_in",
    )(x2, ln, w_p, ones256, qg, kg, rope_a, rope_c)


def _attn_body(seq, q_ref, kp_ref, kc_ref, kn_ref, vp_ref, vc_ref, vn_ref, o_ref,
               kbuf, vbuf, m_ref, l_ref, acc_ref, band_ref):
    t = ATT_TILE
    halo = A_MAX_REACH
    sub = ATT_SUB
    tile_start = pl.program_id(2) * t
    kbuf[0:halo, :] = kp_ref[...]
    kbuf[halo:halo + t, :] = kc_ref[...]
    kbuf[halo + t:, :] = kn_ref[...]
    vbuf[0:halo, :] = vp_ref[...]
    vbuf[halo:halo + t, :] = vc_ref[...]
    vbuf[halo + t:, :] = vn_ref[...]

    low_half = _iota((sub, LANES), 1) < HEAD_DIM
    a_idx = _iota((2 * sub, ATT_KEYS), 0) & (sub - 1)
    rel = _iota((2 * sub, ATT_KEYS), 1) - a_idx
    band_ref[...] = jnp.where((rel >= 0) & (rel <= 2 * A_RADIUS), 0.0, NEG)
    c_row = _iota((1, ATT_KEYS), 1)
    ones_v = jnp.ones((ATT_KEYS, LANES), BF16)

    def scores_stage(tile):
        pat, qs, dil = tile
        span = A_RADIUS * dil
        q_sub = q_ref[pl.ds(qs, sub, stride=dil), :]
        q_st = jnp.concatenate([jnp.where(low_half, q_sub, 0.0),
                                jnp.where(low_half, 0.0, q_sub)], axis=0).astype(BF16)
        k_sub = kbuf[pl.ds(halo + qs - span, ATT_KEYS, stride=dil), :].astype(BF16)
        s = _dot_nt(q_st, k_sub)
        key_pos = (tile_start + qs - span) + dil * c_row
        bias = jnp.where((key_pos >= 0) & (key_pos < seq), 0.0, NEG)
        s = (s + band_ref[...]) + bias
        return s, jnp.broadcast_to(jnp.max(s, axis=1, keepdims=True), (2 * sub, LANES))

    def values_stage(tile, s, m_row):
        pat, qs, dil = tile
        p = jnp.exp2(s - jnp.concatenate([m_row] * (ATT_KEYS // LANES), axis=1)).astype(BF16)
        v_sub = vbuf[pl.ds(halo + qs - A_RADIUS * dil, ATT_KEYS, stride=dil), :].astype(BF16)
        return _dot(p, jnp.concatenate([v_sub, ones_v], axis=1))

    def store_stage(tile, m_row, pvl):
        pat, qs, dil = tile
        rw = pl.ds(qs, sub, stride=dil)
        m_ref[pat, rw, :] = jnp.where(low_half, m_row[0:sub], m_row[sub:])
        l_ref[pat, rw, :] = jnp.where(low_half, pvl[0:sub, LANES:], pvl[sub:, LANES:])
        acc_ref[pat, rw, :] = jnp.where(low_half, pvl[0:sub, 0:LANES], pvl[sub:, 0:LANES])

    tiles = [(pat, r + j * sub * dil, dil)
             for pat, (_, dil) in enumerate(A_PATTERNS)
             for r in range(dil) for j in range(t // (sub * dil))]
    scored, valued = {}, {}
    for step in range(len(tiles) + 2 * ATT_SKEW):
        i_store, i_val = step - 2 * ATT_SKEW, step - ATT_SKEW
        if 0 <= i_store < len(tiles):
            store_stage(tiles[i_store], *valued.pop(i_store))
        if 0 <= i_val < len(tiles):
            s, m_row = scored.pop(i_val)
            valued[i_val] = (m_row, values_stage(tiles[i_val], s, m_row))
        if step < len(tiles):
            scored[step] = scores_stage(tiles[step])

    npat = len(A_PATTERNS)
    blk = 2 * sub
    for i in range(t // blk):
        rs = pl.ds(i * blk, blk)
        m_g = [m_ref[g, rs, :] for g in range(npat)]
        m_all = functools.reduce(jnp.maximum, m_g)
        w_g = [jnp.exp2(m - m_all) for m in m_g]
        l_all = functools.reduce(lambda a, b: a + b, [w * l_ref[g, rs, :] for g, w in enumerate(w_g)])
        num = functools.reduce(lambda a, b: a + b, [w * acc_ref[g, rs, :] for g, w in enumerate(w_g)])
        o_ref[rs, :] = (num / l_all).astype(o_ref.dtype)


def _attn(z3):
    b, seq, _ = z3.shape
    t = ATT_TILE
    halo = A_MAX_REACH
    nt = seq // t
    per_tile = t // halo
    n_halo = seq // halo
    qc, kc, vc = Z_AQ // LANES, Z_AK // LANES, Z_AV // LANES

    def cur(c0):
        return pl.BlockSpec((None, t, LANES), lambda bi, hp, i: (bi, i, c0 + hp))

    def prev(c0):
        return pl.BlockSpec((None, halo, LANES),
                            lambda bi, hp, i: (bi, jnp.maximum(i * per_tile - 1, 0), c0 + hp))

    def nxt(c0):
        return pl.BlockSpec((None, halo, LANES),
                            lambda bi, hp, i: (bi, jnp.minimum((i + 1) * per_tile, n_halo - 1), c0 + hp))

    return pl.pallas_call(
        functools.partial(_attn_body, seq),
        grid=(b, A_HEADS // 2, nt),
        in_specs=[cur(qc), prev(kc), cur(kc), nxt(kc), prev(vc), cur(vc), nxt(vc)],
        out_specs=pl.BlockSpec((None, t, LANES), lambda bi, hp, i: (bi, i, hp)),
        out_shape=jax.ShapeDtypeStruct((b, seq, A_WIDTH), BF16),
        scratch_shapes=[
            pltpu.VMEM((t + 2 * halo, LANES), F32),
            pltpu.VMEM((t + 2 * halo, LANES), F32),
            pltpu.VMEM((len(A_PATTERNS), t, LANES), F32),
            pltpu.VMEM((len(A_PATTERNS), t, LANES), F32),
            pltpu.VMEM((len(A_PATTERNS), t, LANES), F32),
            pltpu.VMEM((2 * ATT_SUB, ATT_KEYS), F32),
        ],
        compiler_params=_cparams(("parallel", "parallel", "parallel")),
        name="attn",
    )(z3, z3, z3, z3, z3, z3, z3)


def _split3(x):
    hi = x.astype(BF16)
    r1 = x - hi.astype(F32)
    mid = r1.astype(BF16)
    lo = (r1 - mid.astype(F32)).astype(BF16)
    return hi, mid, lo


def _gla_log_decay(g, gup_ref, gbias_ref):
    logits = _dot(g.astype(BF16), gup_ref[...]) + gbias_ref[...]
    return _log_sigmoid(logits) * (1.0 / B_GATE_TAU)


def _cumsum_rows(la, tri3_ref):
    hi, mid, lo = _split3(la)
    return _dot(tri3_ref[...], jnp.concatenate([hi, mid, lo], axis=0))


def _gla_bwd_body(nit, k_ref, v_ref, g_ref, gup_ref, gbias_ref, tri3_ref, emask_ref, sb_ref, state):
    @pl.when(pl.program_id(1) == 0)
    def _():
        state[...] = jnp.zeros_like(state)

    c = GLA_CHUNK
    grp = GLA_GROUP
    span = grp * c

    def step(i, carry):
        it = nit - 1 - i
        rows = pl.ds(pl.multiple_of(it * span, span), span)
        logits = _dot(g_ref[rows, :].astype(BF16), gup_ref[:, B_QK:]) + gbias_ref[:, B_QK:]
        la = _log_sigmoid(logits) * (1.0 / B_GATE_TAU)
        cums = [_cumsum_rows(la[c * j:c * (j + 1)], tri3_ref) for j in range(grp)]
        cbx = jnp.concatenate(cums, axis=0) - la
        kt = (k_ref[rows, :] * jnp.exp(cbx)).astype(BF16)
        vb = v_ref[rows, :].astype(BF16)
        upd = [_dot_tn(vb[c * j:c * (j + 1)], kt[c * j:c * (j + 1)]) for j in range(grp)]
        st = state[...]
        for j in reversed(range(grp)):
            sb_ref[it * grp + j] = st.astype(BF16)
            st = st * jnp.exp(cums[j][c - 1:c, :]) + upd[j] * emask_ref[...]
        state[...] = st
        return carry

    lax.fori_loop(0, nit, step, 0)


def _gla_main_body(nit, q_ref, k_ref, v_ref, r_ref, g_ref, sb_ref, gup_ref, gbias_ref, tri3_ref,
                   emask_ref, estack_ref, ones_ref, gain_ref, o_ref, state):
    @pl.when(pl.program_id(1) == 0)
    def _():
        state[...] = jnp.zeros_like(state)

    c = GLA_CHUNK
    n = GLA_SUB
    nsub = c // n
    grp = GLA_GROUP
    span = grp * c
    row = _iota((c, B_QK), 0)
    row_8 = _iota((8, B_QK), 0)
    rho = _iota((B_HEADS * n, B_QK), 0)
    head_k_ok = (rho >> 4) == (_iota((B_HEADS * n, B_QK), 1) >> 5)
    rho_v = _iota((B_HEADS * n, B_WIDTH), 0)
    head_v_ok = (rho_v >> 4) == (_iota((B_HEADS * n, B_WIDTH), 1) >> 6)
    own_blk = ((_iota((span, B_HEADS * c), 0) & (c - 1)) >> 4) == (_iota((span, B_HEADS * c), 1) >> 6)
    zeros_k = jnp.zeros((B_HEADS * n, B_QK), F32)

    def rows_of(vals, height):
        return jnp.concatenate([jnp.broadcast_to(x, (height, x.shape[1])) for x in vals], axis=0)

    def by_head(x, ok):
        return jnp.where(ok, jnp.concatenate([x] * B_HEADS, axis=0), 0.0)

    def diag_lhs(q, k, bf2, cb2):
        nblk = span // n
        slabs = []
        for s in range(n):
            ks = rows_of([k[n * i + s:n * i + s + 1] for i in range(nblk)], n)
            pieces = []
            for i in range(nblk):
                ref_f = bf2[n * i + s:n * i + s + 1]
                ref_b = cb2[n * i + s:n * i + s + 1]
                for r0 in range(0, n, 8):
                    rs = slice(n * i + r0, n * i + r0 + 8)
                    if s <= r0:
                        pieces.append(bf2[rs] - ref_f)
                    elif s >= r0 + 8:
                        pieces.append(ref_b - cb2[rs])
                    else:
                        pieces.append(jnp.where(row_8 >= s - r0, bf2[rs] - ref_f, ref_b - cb2[rs]))
            arg = jnp.concatenate(pieces, axis=0)
            slabs.append(((q * ks) * jnp.exp2(arg)).astype(BF16))
        return jnp.concatenate(slabs, axis=1)

    def chunk_local(q, k, v, bf, cbx, kf, kb, att_diag):
        e_rows = [bf[n * j + n - 1:n * j + n, :] for j in range(nsub)]
        f_rows = [cbx[n * j:n * j + 1, :] for j in range(nsub)]
        lhs_parts = []
        for j in range(nsub - 1):
            lhs_parts.append(jnp.where(row >= n * (j + 1),
                                       q * jnp.exp(jnp.minimum(bf - e_rows[j], 0.0)), 0.0))
        for j in range(1, nsub):
            lhs_parts.append(jnp.where(row < n * j, q * jnp.exp(jnp.minimum(f_rows[j] - cbx, 0.0)), 0.0))
        rhs_rows = []
        for j in range(nsub):
            kfj = by_head(kf[n * j:n * (j + 1)], head_k_ok)
            kbj = by_head(kb[n * j:n * (j + 1)], head_k_ok)
            parts = [kfj if (jj == j and j < nsub - 1) else zeros_k for jj in range(nsub - 1)]
            parts += [kbj if (jj == j and j > 0) else zeros_k for jj in range(1, nsub)]
            rhs_rows.append(jnp.concatenate(parts, axis=1))
        att = _dot_nt(jnp.concatenate(lhs_parts, axis=1).astype(BF16),
                      jnp.concatenate(rhs_rows, axis=0).astype(BF16))
        v4 = jnp.concatenate([by_head(v[n * j:n * (j + 1)], head_v_ok) for j in range(nsub)],
                             axis=0).astype(BF16)
        return _dot((att + att_diag).astype(BF16), v4)

    def step(it, carry):
        rows = pl.ds(pl.multiple_of(it * span, span), span)
        q = q_ref[rows, :]
        k = k_ref[rows, :]
        v = v_ref[rows, :]
        la = _gla_log_decay(g_ref[rows, :], gup_ref, gbias_ref)
        cums = [_cumsum_rows(la[c * j:c * (j + 1)], tri3_ref) for j in range(grp)]
        cum = jnp.concatenate(cums, axis=0)
        bf = cum[:, :B_QK]
        cbx = cum[:, B_QK:] - la[:, B_QK:]
        tot_f = [cj[c - 1:c, :B_QK] for cj in cums]
        tot_b = [cj[c - 1:c, B_QK:] for cj in cums]
        bf2 = bf * LOG2E
        cb2 = cbx * LOG2E
        e_blk = rows_of([bf[n * j + n - 1:n * j + n, :] for j in range(span // n)], n)
        f_blk = rows_of([cbx[n * j:n * j + 1, :] for j in range(span // n)], n)
        kf = k * jnp.exp(e_blk - bf)
        kb = k * jnp.exp(cbx - f_blk)
        kt = (k * jnp.exp(rows_of(tot_f, c) - bf)).astype(BF16)
        lhs_inter = jnp.concatenate([q * jnp.exp(bf), q * jnp.exp(rows_of(tot_b, c) - cbx)],
                                    axis=1).astype(BF16)
        vb = v.astype(BF16)
        att_diag = jnp.where(own_blk, _dot(diag_lhs(q, k, bf2, cb2), estack_ref[...]), 0.0)

        outs = []
        upd = []
        for j in range(grp):
            sl = slice(c * j, c * (j + 1))
            outs.append(chunk_local(q[sl], k[sl], v[sl], bf[sl], cbx[sl], kf[sl], kb[sl], att_diag[sl]))
            upd.append(_dot_tn(vb[sl], kt[sl]))

        st = state[...]
        for j in range(grp):
            sl = slice(c * j, c * (j + 1))
            rhs = jnp.concatenate([st.astype(BF16), sb_ref[it * grp + j]], axis=1)
            outs[j] = outs[j] + _dot_nt(lhs_inter[sl], rhs)
            st = st * jnp.exp(tot_f[j]) + upd[j] * emask_ref[...]
        state[...] = st

        gate = r_ref[rows, :]
        o = jnp.concatenate(outs, axis=0)
        out = _head_norm(o, ones_ref, gain_ref[...]) * (gate * _sigmoid(gate))
        o_ref[rows, :] = out.astype(o_ref.dtype)
        return carry

    lax.fori_loop(0, nit, step, 0)


def _gla(z3, gup_bd, gbias, tri3, emask, estack, ones256, gain, tb):
    b, seq, _ = z3.shape
    nblk = seq // tb
    nch = tb // GLA_CHUNK
    ntot = seq // GLA_CHUNK

    def zspec(col, width, rev):
        blk = col // width
        if rev:
            return pl.BlockSpec((None, tb, width), lambda bi, i: (bi, nblk - 1 - i, blk))
        return pl.BlockSpec((None, tb, width), lambda bi, i: (bi, i, blk))

    consts = [gup_bd, gbias, tri3, emask]
    const_specs = [_const_spec(a.shape) for a in consts]
    nit = nch // GLA_GROUP
    sb = pl.pallas_call(
        functools.partial(_gla_bwd_body, nit),
        grid=(b, nblk),
        in_specs=[zspec(Z_BK, B_QK, True), zspec(Z_BV, B_WIDTH, True), zspec(Z_BG, LANES, True)]
        + const_specs,
        out_specs=pl.BlockSpec((None, nch, B_WIDTH, B_QK), lambda bi, i: (bi, nblk - 1 - i, 0, 0)),
        out_shape=jax.ShapeDtypeStruct((b, ntot, B_WIDTH, B_QK), BF16),
        scratch_shapes=[pltpu.VMEM((B_WIDTH, B_QK), F32)],
        compiler_params=_cparams(("parallel", "arbitrary")),
        name="gla_bwd",
    )(z3, z3, z3, *consts)

    consts2 = [gup_bd, gbias, tri3, emask, estack, ones256, gain]
    return pl.pallas_call(
        functools.partial(_gla_main_body, nit),
        grid=(b, nblk),
        in_specs=[zspec(Z_BQ, B_QK, False), zspec(Z_BK, B_QK, False), zspec(Z_BV, B_WIDTH, False),
                  zspec(Z_BR, B_WIDTH, False), zspec(Z_BG, LANES, False),
                  pl.BlockSpec((None, nch, B_WIDTH, B_QK), lambda bi, i: (bi, i, 0, 0))]
        + [_const_spec(a.shape) for a in consts2],
        out_specs=pl.BlockSpec((None, tb, B_WIDTH), lambda bi, i: (bi, i, 0)),
        out_shape=jax.ShapeDtypeStruct((b, seq, B_WIDTH), BF16),
        scratch_shapes=[pltpu.VMEM((B_WIDTH, B_QK), F32)],
        compiler_params=_cparams(("parallel", "arbitrary")),
        name="gla_main",
    )(z3, z3, z3, z3, z3, sb, *consts2)


def _ret_bwd_body(nch, k_ref, v_ref, raw_ref, bmask_ref, rb_ref, state):
    @pl.when(pl.program_id(1) == 0)
    def _():
        state[...] = jnp.zeros_like(state)

    c = RET_CHUNK
    lg1 = _log_sigmoid(raw_ref[1:2, :])
    pos = _iota((c, C_WIDTH), 0).astype(F32)
    kdec = jnp.exp(pos * lg1)
    chunk_decay = jnp.exp(float(c) * lg1)

    upd = []
    for j in range(nch):
        rows = pl.ds(c * j, c)
        kt = (k_ref[rows, :] * kdec).astype(BF16)
        upd.append(_dot_tn(kt, v_ref[rows, :].astype(BF16)))
    st = state[...]
    for j in reversed(range(nch)):
        rb_ref[j] = st.astype(BF16)
        st = chunk_decay * st + upd[j] * bmask_ref[...]
    state[...] = st


def _ret_main_body(nch, q_ref, k_ref, v_ref, g_ref, rb_ref, raw_ref, raw_s_ref, bmask_ref, ones_ref,
                   gain_ref, o_ref, state):
    @pl.when(pl.program_id(1) == 0)
    def _():
        state[...] = jnp.zeros_like(state)

    c = RET_CHUNK
    lg = _log_sigmoid(raw_ref[...])
    lg0, lg1 = lg[0:1, :], lg[1:2, :]
    lgs = _log_sigmoid(raw_s_ref[...])
    pos = _iota((c, C_WIDTH), 0).astype(F32)
    qdec_f = jnp.exp((pos + 1.0) * lg0)
    qdec_b = jnp.exp((float(c) - pos) * lg1)
    kdec_f = jnp.exp((float(c) - 1.0 - pos) * lg0)
    chunk_decay = jnp.exp(float(c) * lg0)
    t_idx = _iota((c, C_HEADS * c), 0)
    s_idx = _iota((c, C_HEADS * c), 1) & (c - 1)
    rel = (t_idx - s_idx).astype(F32)
    dmat = jnp.where(rel >= 0.0, jnp.exp(jnp.maximum(rel, 0.0) * lgs[0:1, :]),
                     jnp.exp(jnp.maximum(-rel, 0.0) * lgs[1:2, :]))
    rho = _iota((C_HEADS * c, C_WIDTH), 0)
    lane = _iota((C_HEADS * c, C_WIDTH), 1)
    head_ok = (rho >> 7) == (lane >> 6)

    outs, lhs_inter, upd = [], [], []
    for j in range(nch):
        rows = pl.ds(c * j, c)
        q = q_ref[rows, :]
        k = k_ref[rows, :]
        v = v_ref[rows, :]
        k4 = jnp.where(head_ok, jnp.concatenate([k] * C_HEADS, axis=0), 0.0).astype(BF16)
        v4 = jnp.where(head_ok, jnp.concatenate([v] * C_HEADS, axis=0), 0.0).astype(BF16)
        scores = _dot_nt(q.astype(BF16), k4) * dmat
        outs.append(_dot(scores.astype(BF16), v4))
        lhs_inter.append(jnp.concatenate([q * qdec_f, q * qdec_b], axis=1).astype(BF16))
        upd.append(_dot_tn((k * kdec_f).astype(BF16), v.astype(BF16)))
    st = state[...]
    for j in range(nch):
        rhs = jnp.concatenate([st.astype(BF16), rb_ref[j]], axis=0)
        outs[j] = outs[j] + _dot(lhs_inter[j], rhs)
        st = chunk_decay * st + upd[j] * bmask_ref[...]
    state[...] = st
    for j in range(nch):
        rows = pl.ds(c * j, c)
        gate = g_ref[rows, :]
        out = _head_norm(outs[j], ones_ref, gain_ref[...]) * (gate * _sigmoid(gate))
        o_ref[rows, :] = out.astype(o_ref.dtype)


def _ret(z3, raw256, raw512, ones256, gain, tb):
    b, seq, _ = z3.shape
    nblk = seq // tb
    nch = tb // RET_CHUNK
    ntot = seq // RET_CHUNK

    def zspec(col, rev):
        blk = col // C_WIDTH
        if rev:
            return pl.BlockSpec((None, tb, C_WIDTH), lambda bi, i: (bi, nblk - 1 - i, blk))
        return pl.BlockSpec((None, tb, C_WIDTH), lambda bi, i: (bi, i, blk))

    rb = pl.pallas_call(
        functools.partial(_ret_bwd_body, nch),
        grid=(b, nblk),
        in_specs=[zspec(Z_CK, True), zspec(Z_CV, True), _const_spec(raw256.shape),
                  _const_spec(ones256.shape)],
        out_specs=pl.BlockSpec((None, nch, C_WIDTH, C_WIDTH), lambda bi, i: (bi, nblk - 1 - i, 0, 0)),
        out_shape=jax.ShapeDtypeStruct((b, ntot, C_WIDTH, C_WIDTH), BF16),
        scratch_shapes=[pltpu.VMEM((C_WIDTH, C_WIDTH), F32)],
        compiler_params=_cparams(("parallel", "arbitrary")),
        name="ret_bwd",
    )(z3, z3, raw256, ones256)

    consts = [raw256, raw512, ones256, ones256, gain]
    return pl.pallas_call(
        functools.partial(_ret_main_body, nch),
        grid=(b, nblk),
        in_specs=[zspec(Z_CQ, False), zspec(Z_CK, False), zspec(Z_CV, False), zspec(Z_CG, False),
                  pl.BlockSpec((None, nch, C_WIDTH, C_WIDTH), lambda bi, i: (bi, i, 0, 0))]
        + [_const_spec(a.shape) for a in consts],
        out_specs=pl.BlockSpec((None, tb, C_WIDTH), lambda bi, i: (bi, i, 0)),
        out_shape=jax.ShapeDtypeStruct((b, seq, C_WIDTH), BF16),
        scratch_shapes=[pltpu.VMEM((C_WIDTH, C_WIDTH), F32)],
        compiler_params=_cparams(("parallel", "arbitrary")),
        name="ret_main",
    )(z3, z3, z3, z3, rb, *consts)


def _post_body(h_ref, oa_ref, ob_ref, oc_ref, p_ref, wo_ref, lm_ref, w1_ref, w2_ref, lp_ref,
               wg_ref, wp_ref, y_ref):
    def rms(x, g_ref):
        ms = jnp.mean(x * x, axis=-1, keepdims=True)
        return (x * lax.rsqrt(ms + EPS) * g_ref[...]).astype(BF16)

    h = h_ref[...]
    h = h + (_dot(oa_ref[...], wo_ref[0:A_WIDTH, :])
             + _dot(ob_ref[...], wo_ref[A_WIDTH:A_WIDTH + B_WIDTH, :])
             + _dot(oc_ref[...], wo_ref[A_WIDTH + B_WIDTH:, :]))
    m = rms(h, lm_ref)
    ff = D_FF // 4
    mlp = None
    for j in range(4):
        hid = _dot(m, w1_ref[:, ff * j:ff * (j + 1)])
        hid = jnp.square(jnp.maximum(hid, 0.0)).astype(BF16)
        part = _dot(hid, w2_ref[ff * j:ff * (j + 1), :])
        mlp = part if mlp is None else mlp + part
    h = h + mlp
    gate = _sigmoid(_dot(rms(h, lp_ref), wg_ref[...]))
    y_ref[...] = h + gate * _dot(p_ref[...].astype(BF16), wp_ref[...])


def _post(h2, oa, ob, oc, ple, wo, lm, w1, w2, lp, wg, wp, tm):
    n = h2.shape[0]

    def tok(width):
        return pl.BlockSpec((tm, width), lambda i: (i, 0))

    consts = [wo, lm, w1, w2, lp, wg, wp]
    return pl.pallas_call(
        _post_body,
        grid=(n // tm,),
        in_specs=[tok(D_MODEL), tok(A_WIDTH), tok(B_WIDTH), tok(C_WIDTH), tok(PLE_DIM)]
        + [_const_spec(a.shape) for a in consts],
        out_specs=tok(D_MODEL),
        out_shape=jax.ShapeDtypeStruct((n, D_MODEL), F32),
        compiler_params=_cparams(("parallel",)),
        name="post",
    )(h2, oa, ob, oc, ple, *consts)


def _rope_tables(seq, rot_dim, theta):
    half = rot_dim // 2
    inv_freq = 1.0 / (theta ** (jnp.arange(half, dtype=F32) * (2.0 / rot_dim)))
    ang = jnp.arange(seq, dtype=F32)[:, None] * inv_freq[None, :]
    cos, sin = jnp.cos(ang), jnp.sin(ang)
    pad = HEAD_DIM - rot_dim
    c = jnp.concatenate([cos, cos, jnp.ones((seq, pad), F32)], axis=1)
    sn = jnp.concatenate([-sin, jnp.zeros((seq, half + pad), F32)], axis=1)
    sp = jnp.concatenate([jnp.zeros((seq, half), F32), sin, jnp.zeros((seq, pad), F32)], axis=1)
    return jnp.concatenate([jnp.tile(t, (1, LANES // HEAD_DIM)) for t in (c, sn, sp)], axis=1)


def _block_ones(rows, cols, rblk, cblk):
    r = jnp.arange(rows)[:, None] // rblk
    c = jnp.arange(cols)[None, :] // cblk
    return r == c


def _layer_consts(i, ln_mix, w_in, attn_q_norm, attn_k_norm, gla_gate_up, gla_gate_bias, gla_out_norm,
                  ret_decay_raw, ret_out_norm, w_out, ln_mlp, w_mlp_in, w_mlp_out, ln_pe, w_pe_gate,
                  w_pe_proj):
    w = w_in[i]
    w_p = jnp.concatenate([w[:, :2304], w[:, 2336:N_IN], w[:, 2304:2336],
                           jnp.zeros((D_MODEL, Z_WIDTH - N_IN), F32)], axis=1).astype(BF16)
    gup = gla_gate_up[i].astype(BF16)
    gup_bd = jnp.zeros((LANES, 2 * B_QK), BF16)
    gup_bd = gup_bd.at[0:B_GATE_RANK, 0:B_QK].set(gup[0])
    gup_bd = gup_bd.at[B_GATE_RANK:2 * B_GATE_RANK, B_QK:].set(gup[1])
    return dict(
        ln_mix=ln_mix[i][None, :], w_p=w_p,
        qg=jnp.tile(attn_q_norm[i], 4)[None, :], kg=jnp.tile(attn_k_norm[i], 4)[None, :],
        gup_bd=gup_bd, gbias=gla_gate_bias[i].reshape(1, 2 * B_QK),
        gla_gain=gla_out_norm[i][None, :],
        raw256=jnp.repeat(ret_decay_raw[i], C_DIM, axis=1),
        raw512=jnp.repeat(ret_decay_raw[i], RET_CHUNK, axis=1),
        ret_gain=ret_out_norm[i][None, :],
        wo=w_out[i].astype(BF16), lm=ln_mlp[i][None, :], w1=w_mlp_in[i].astype(BF16),
        w2=w_mlp_out[i].astype(BF16), lp=ln_pe[i][None, :], wg=w_pe_gate[i].astype(BF16),
        wp=w_pe_proj[i].astype(BF16),
    )


def _shared_consts():
    tri = jnp.arange(GLA_CHUNK)[:, None] >= jnp.arange(GLA_CHUNK)[None, :]
    return dict(
        ones256=_block_ones(256, 256, HEAD_DIM, HEAD_DIM).astype(BF16),
        tri3=jnp.tile(tri, (1, 3)).astype(BF16),
        emask=_block_ones(B_WIDTH, B_QK, HEAD_DIM, B_KDIM).astype(F32),
        estack=_gla_diag_selector().astype(BF16),
    )


def _gla_diag_selector():
    r = jnp.arange(GLA_SUB * B_QK)
    c = jnp.arange(B_HEADS * GLA_CHUNK)
    same_s = (r // B_QK)[:, None] == (c % GLA_SUB)[None, :]
    same_head = ((r % B_QK) // B_KDIM)[:, None] == ((c % (B_HEADS * GLA_SUB)) // GLA_SUB)[None, :]
    return same_s & same_head


def _run_trunk(x, p, layers, shared):
    b, seq, _ = x.shape
    n = b * seq
    tm = 512
    tb_gla, tb_ret = 1024, 2048
    rope_a = _rope_tables(seq, ROPE_DIM, ROPE_THETA)
    rope_c = _rope_tables(seq, C_DIM, RET_THETA)
    h = x.reshape(n, D_MODEL)
    for i, lc in enumerate(layers):
        z = _proj_in(h, seq, lc["ln_mix"], lc["w_p"], shared["ones256"], lc["qg"], lc["kg"],
                     rope_a, rope_c, tm)
        z3 = z.reshape(b, seq, Z_WIDTH)
        oa = _attn(z3)
        ob = _gla(z3, lc["gup_bd"], lc["gbias"], shared["tri3"], shared["emask"], shared["estack"],
                  shared["ones256"], lc["gla_gain"], tb_gla)
        oc = _ret(z3, lc["raw256"], lc["raw512"], shared["ones256"], lc["ret_gain"], tb_ret)
        h = _post(h, oa.reshape(n, A_WIDTH), ob.reshape(n, B_WIDTH), oc.reshape(n, C_WIDTH),
                  p[i].reshape(n, PLE_DIM), lc["wo"], lc["lm"], lc["w1"], lc["w2"], lc["lp"],
                  lc["wg"], lc["wp"], tm)
    return h.reshape(b, seq, D_MODEL)


def kernel(x_prompt, x_sample, p_prompt, p_sample, ln_mix, w_in, attn_q_norm, attn_k_norm, gla_gate_up, gla_gate_bias, gla_out_norm, ret_decay_raw, ret_out_norm, w_out, ln_mlp, w_mlp_in, w_mlp_out, ln_pe, w_pe_gate, w_pe_proj):
    depth = w_in.shape[0]
    layers = [_layer_consts(i, ln_mix, w_in, attn_q_norm, attn_k_norm, gla_gate_up, gla_gate_bias,
                            gla_out_norm, ret_decay_raw, ret_out_norm, w_out, ln_mlp, w_mlp_in,
                            w_mlp_out, ln_pe, w_pe_gate, w_pe_proj) for i in range(depth)]
    shared = _shared_consts()
    y_prompt = _run_trunk(x_prompt, p_prompt, layers, shared)
    y_sample = _run_trunk(x_sample, p_sample, layers, shared)
    return (y_prompt, y_sample)
```

```python
import functools

import jax
import jax.numpy as jnp
from jax import lax
from jax.experimental import pallas as pl
from jax.experimental.pallas import tpu as pltpu

F32 = jnp.float32
BF16 = jnp.bfloat16

D_MODEL = 1024
HEAD_DIM = 64
A_HEADS = 8
A_WIDTH = 512
A_PATTERNS = ((128, 1), (512, 4), (2048, 16))
A_RADIUS = 64
A_MAX_REACH = 1024
ROPE_THETA = 500000.0
ROPE_DIM = 16
B_HEADS = 4
B_KDIM = 32
B_QK = 128
B_WIDTH = 256
B_GATE_RANK = 16
B_GATE_TAU = 16.0
C_HEADS = 4
C_DIM = 64
C_WIDTH = 256
RET_THETA = 10000.0
N_IN = 3360
D_FF = 4096
PLE_DIM = 256
EPS = 1e-6
NEG = -1e30
LOG2E = 1.4426950408889634

LANES = 128
VMEM_LIMIT = 56 * 1024 * 1024

Z_AQ, Z_AK, Z_AV = 0, 512, 1024
Z_BQ, Z_BK, Z_BV, Z_BR = 1536, 1664, 1792, 2048
Z_CQ, Z_CK, Z_CV, Z_CG = 2304, 2560, 2816, 3072
Z_BG = 3328
Z_WIDTH = 3456

GLA_CHUNK = 64
GLA_SUB = 16
GLA_GROUP = 8
RET_CHUNK = 128
ATT_TILE = 2048
ATT_SUB = 128
ATT_KEYS = ATT_SUB + 2 * A_RADIUS
ATT_SKEW = 2


def _cparams(sem):
    return pltpu.CompilerParams(dimension_semantics=sem, vmem_limit_bytes=VMEM_LIMIT)


def _const_spec(shape):
    nd = len(shape)
    return pl.BlockSpec(shape, lambda *_: (0,) * nd, pipeline_mode=pl.Buffered(1))


def _sigmoid(x):
    return 1.0 / (1.0 + jnp.exp(-x))


def _log_sigmoid(x):
    return jnp.minimum(x, 0.0) - jnp.log1p(jnp.exp(-jnp.abs(x)))


def _iota(shape, dim):
    return lax.broadcasted_iota(jnp.int32, shape, dim)


def _dot(a, b):
    return jnp.dot(a, b, preferred_element_type=F32)


def _dot_nt(a, b):
    return lax.dot_general(a, b, (((1,), (1,)), ((), ())), preferred_element_type=F32)


def _dot_tn(a, b):
    return lax.dot_general(a, b, (((0,), (0,)), ((), ())), preferred_element_type=F32)


def _head_norm(o, ones_ref, gain):
    ssum = _dot((o * o).astype(BF16), ones_ref[...])
    return o * lax.rsqrt(ssum * (1.0 / HEAD_DIM) + EPS) * gain


def _rope(y, tab_ref, shift):
    c = tab_ref[:, 0:LANES]
    sn = tab_ref[:, LANES:2 * LANES]
    sp = tab_ref[:, 2 * LANES:3 * LANES]
    outs = []
    for j in range(y.shape[1] // LANES):
        yj = y[:, j * LANES:(j + 1) * LANES]
        outs.append(yj * c + pltpu.roll(yj, LANES - shift, 1) * sn + pltpu.roll(yj, shift, 1) * sp)
    return jnp.concatenate(outs, axis=1)


def _proj_in_body(x_ref, g_ref, w_ref, ones_ref, qg_ref, kg_ref, ra_ref, rc_ref, z_ref):
    x = x_ref[...]
    ms = jnp.mean(x * x, axis=-1, keepdims=True)
    u = (x * lax.rsqrt(ms + EPS) * g_ref[...]).astype(BF16)

    def proj(a, b):
        return _dot(u, w_ref[:, a:b])

    def qk_norm(y, gain_ref):
        halves = []
        for j in range(2):
            yj = y[:, 256 * j:256 * (j + 1)]
            halves.append(_head_norm(yj, ones_ref, gain_ref[...]))
        return jnp.concatenate(halves, axis=1)

    aq = _rope(qk_norm(proj(Z_AQ, Z_AK), qg_ref), ra_ref, ROPE_DIM // 2)
    z_ref[:, Z_AQ:Z_AK] = aq * (HEAD_DIM ** -0.5 * LOG2E)
    z_ref[:, Z_AK:Z_AV] = _rope(qk_norm(proj(Z_AK, Z_AV), kg_ref), ra_ref, ROPE_DIM // 2)
    z_ref[:, Z_AV:Z_BQ] = proj(Z_AV, Z_BQ)
    z_ref[:, Z_BQ:Z_BK] = proj(Z_BQ, Z_BK) * (B_KDIM ** -0.5)
    z_ref[:, Z_BK:Z_CQ] = proj(Z_BK, Z_CQ)
    z_ref[:, Z_CQ:Z_CK] = _rope(proj(Z_CQ, Z_CK), rc_ref, C_DIM // 2)
    z_ref[:, Z_CK:Z_CV] = _rope(proj(Z_CK, Z_CV), rc_ref, C_DIM // 2) * (C_DIM ** -0.5)
    z_ref[:, Z_CV:Z_WIDTH] = proj(Z_CV, Z_WIDTH)


def _proj_in(x2, seq, ln, w_p, ones256, qg, kg, rope_a, rope_c, tm):
    n = x2.shape[0]
    per_seq = seq // tm
    return pl.pallas_call(
        _proj_in_body,
        grid=(n // tm,),
        in_specs=[
            pl.BlockSpec((tm, D_MODEL), lambda i: (i, 0)),
            _const_spec((1, D_MODEL)),
            _const_spec((D_MODEL, Z_WIDTH)),
            _const_spec((256, 256)),
            _const_spec((1, 256)),
            _const_spec((1, 256)),
            pl.BlockSpec((tm, 3 * LANES), lambda i: (i % per_seq, 0)),
            pl.BlockSpec((tm, 3 * LANES), lambda i: (i % per_seq, 0)),
        ],
        out_specs=pl.BlockSpec((tm, Z_WIDTH), lambda i: (i, 0)),
        out_shape=jax.ShapeDtypeStruct((n, Z_WIDTH), F32),
        compiler_params=_cparams(("parallel",)),
        name="proj_in",
    )(x2, ln, w_p, ones256, qg, kg, rope_a, rope_c)


def _attn_body(seq, q_ref, kp_ref, kc_ref, kn_ref, vp_ref, vc_ref, vn_ref, o_ref,
               kbuf, vbuf, m_ref, l_ref, acc_ref, band_ref):
    t = ATT_TILE
    halo = A_MAX_REACH
    sub = ATT_SUB
    tile_start = pl.program_id(2) * t
    kbuf[0:halo, :] = kp_ref[...]
    kbuf[halo:halo + t, :] = kc_ref[...]
    kbuf[halo + t:, :] = kn_ref[...]
    vbuf[0:halo, :] = vp_ref[...]
    vbuf[halo:halo + t, :] = vc_ref[...]
    vbuf[halo + t:, :] = vn_ref[...]

    low_half = _iota((sub, LANES), 1) < HEAD_DIM
    a_idx = _iota((2 * sub, ATT_KEYS), 0) & (sub - 1)
    rel = _iota((2 * sub, ATT_KEYS), 1) - a_idx
    band_ref[...] = jnp.where((rel >= 0) & (rel <= 2 * A_RADIUS), 0.0, NEG)
    c_row = _iota((1, ATT_KEYS), 1)
    ones_v = jnp.ones((ATT_KEYS, LANES), BF16)

    def scores_stage(tile):
        pat, qs, dil = tile
        span = A_RADIUS * dil
        q_sub = q_ref[pl.ds(qs, sub, stride=dil), :]
        q_st = jnp.concatenate([jnp.where(low_half, q_sub, 0.0),
                                jnp.where(low_half, 0.0, q_sub)], axis=0).astype(BF16)
        k_sub = kbuf[pl.ds(halo + qs - span, ATT_KEYS, stride=dil), :].astype(BF16)
        s = _dot_nt(q_st, k_sub)
        key_pos = (tile_start + qs - span) + dil * c_row
        bias = jnp.where((key_pos >= 0) & (key_pos < seq), 0.0, NEG)
        s = (s + band_ref[...]) + bias
        return s, jnp.broadcast_to(jnp.max(s, axis=1, keepdims=True), (2 * sub, LANES))

    def values_stage(tile, s, m_row):
        pat, qs, dil = tile
        p = jnp.exp2(s - jnp.concatenate([m_row] * (ATT_KEYS // LANES), axis=1)).astype(BF16)
        v_sub = vbuf[pl.ds(halo + qs - A_RADIUS * dil, ATT_KEYS, stride=dil), :].astype(BF16)
        return _dot(p, jnp.concatenate([v_sub, ones_v], axis=1))

    def store_stage(tile, m_row, pvl):
        pat, qs, dil = tile
        rw = pl.ds(qs, sub, stride=dil)
        m_ref[pat, rw, :] = jnp.where(low_half, m_row[0:sub], m_row[sub:])
        l_ref[pat, rw, :] = jnp.where(low_half, pvl[0:sub, LANES:], pvl[sub:, LANES:])
        acc_ref[pat, rw, :] = jnp.where(low_half, pvl[0:sub, 0:LANES], pvl[sub:, 0:LANES])

    tiles = [(pat, r + j * sub * dil, dil)
             for pat, (_, dil) in enumerate(A_PATTERNS)
             for r in range(dil) for j in range(t // (sub * dil))]
    scored, valued = {}, {}
    for step in range(len(tiles) + 2 * ATT_SKEW):
        i_store, i_val = step - 2 * ATT_SKEW, step - ATT_SKEW
        if 0 <= i_store < len(tiles):
            store_stage(tiles[i_store], *valued.pop(i_store))
        if 0 <= i_val < len(tiles):
            s, m_row = scored.pop(i_val)
            valued[i_val] = (m_row, values_stage(tiles[i_val], s, m_row))
        if step < len(tiles):
            scored[step] = scores_stage(tiles[step])

    npat = len(A_PATTERNS)
    blk = 2 * sub
    for i in range(t // blk):
        rs = pl.ds(i * blk, blk)
        m_g = [m_ref[g, rs, :] for g in range(npat)]
        m_all = functools.reduce(jnp.maximum, m_g)
        w_g = [jnp.exp2(m - m_all) for m in m_g]
        l_all = functools.reduce(lambda a, b: a + b, [w * l_ref[g, rs, :] for g, w in enumerate(w_g)])
        num = functools.reduce(lambda a, b: a + b, [w * acc_ref[g, rs, :] for g, w in enumerate(w_g)])
        o_ref[rs, :] = (num / l_all).astype(o_ref.dtype)


def _attn(z3):
    b, seq, _ = z3.shape
    t = ATT_TILE
    halo = A_MAX_REACH
    nt = seq // t
    per_tile = t // halo
    n_halo = seq // halo
    qc, kc, vc = Z_AQ // LANES, Z_AK // LANES, Z_AV // LANES

    def cur(c0):
        return pl.BlockSpec((None, t, LANES), lambda bi, hp, i: (bi, i, c0 + hp))

    def prev(c0):
        return pl.BlockSpec((None, halo, LANES),
                            lambda bi, hp, i: (bi, jnp.maximum(i * per_tile - 1, 0), c0 + hp))

    def nxt(c0):
        return pl.BlockSpec((None, halo, LANES),
                            lambda bi, hp, i: (bi, jnp.minimum((i + 1) * per_tile, n_halo - 1), c0 + hp))

    return pl.pallas_call(
        functools.partial(_attn_body, seq),
        grid=(b, A_HEADS // 2, nt),
        in_specs=[cur(qc), prev(kc), cur(kc), nxt(kc), prev(vc), cur(vc), nxt(vc)],
        out_specs=pl.BlockSpec((None, t, LANES), lambda bi, hp, i: (bi, i, hp)),
        out_shape=jax.ShapeDtypeStruct((b, seq, A_WIDTH), BF16),
        scratch_shapes=[
            pltpu.VMEM((t + 2 * halo, LANES), F32),
            pltpu.VMEM((t + 2 * halo, LANES), F32),
            pltpu.VMEM((len(A_PATTERNS), t, LANES), F32),
            pltpu.VMEM((len(A_PATTERNS), t, LANES), F32),
            pltpu.VMEM((len(A_PATTERNS), t, LANES), F32),
            pltpu.VMEM((2 * ATT_SUB, ATT_KEYS), F32),
        ],
        compiler_params=_cparams(("parallel", "parallel", "parallel")),
        name="attn",
    )(z3, z3, z3, z3, z3, z3, z3)


def _split3(x):
    hi = x.astype(BF16)
    r1 = x - hi.astype(F32)
    mid = r1.astype(BF16)
    lo = (r1 - mid.astype(F32)).astype(BF16)
    return hi, mid, lo


def _gla_log_decay(g, gup_ref, gbias_ref):
    logits = _dot(g.astype(BF16), gup_ref[...]) + gbias_ref[...]
    return _log_sigmoid(logits) * (1.0 / B_GATE_TAU)


def _cumsum_rows(la, tri3_ref):
    hi, mid, lo = _split3(la)
    return _dot(tri3_ref[...], jnp.concatenate([hi, mid, lo], axis=0))


def _gla_bwd_body(nit, k_ref, v_ref, g_ref, gup_ref, gbias_ref, tri3_ref, emask_ref, sb_ref, state):
    @pl.when(pl.program_id(1) == 0)
    def _():
        state[...] = jnp.zeros_like(state)

    c = GLA_CHUNK
    grp = GLA_GROUP
    span = grp * c

    def step(i, carry):
        it = nit - 1 - i
        rows = pl.ds(pl.multiple_of(it * span, span), span)
        logits = _dot(g_ref[rows, :].astype(BF16), gup_ref[:, B_QK:]) + gbias_ref[:, B_QK:]
        la = _log_sigmoid(logits) * (1.0 / B_GATE_TAU)
        cums = [_cumsum_rows(la[c * j:c * (j + 1)], tri3_ref) for j in range(grp)]
        cbx = jnp.concatenate(cums, axis=0) - la
        kt = (k_ref[rows, :] * jnp.exp(cbx)).astype(BF16)
        vb = v_ref[rows, :].astype(BF16)
        upd = [_dot_tn(vb[c * j:c * (j + 1)], kt[c * j:c * (j + 1)]) for j in range(grp)]
        st = state[...]
        for j in reversed(range(grp)):
            sb_ref[it * grp + j] = functools.reduce(
                lambda a, b: a + b, [st[HEAD_DIM * h:HEAD_DIM * (h + 1)] for h in range(B_HEADS)]).astype(BF16)
            st = st * jnp.exp(cums[j][c - 1:c, :]) + upd[j] * emask_ref[...]
        state[...] = st
        return carry

    lax.fori_loop(0, nit, step, 0)


def _gla_main_body(nit, q_ref, k_ref, v_ref, r_ref, g_ref, sb_ref, gup_ref, gbias_ref, tri3_ref,
                   emask_ref, estack_ref, ones_ref, gain_ref, o_ref, state):
    @pl.when(pl.program_id(1) == 0)
    def _():
        state[...] = jnp.zeros_like(state)

    c = GLA_CHUNK
    n = GLA_SUB
    nsub = c // n
    grp = GLA_GROUP
    span = grp * c
    row = _iota((c, B_QK), 0)
    row_8 = _iota((8, B_QK), 0)
    rho = _iota((B_HEADS * n, B_QK), 0)
    head_k_ok = (rho >> 4) == (_iota((B_HEADS * n, B_QK), 1) >> 5)
    rho_v = _iota((B_HEADS * n, B_WIDTH), 0)
    head_v_ok = (rho_v >> 4) == (_iota((B_HEADS * n, B_WIDTH), 1) >> 6)
    own_blk = ((_iota((span, B_HEADS * c), 0) & (c - 1)) >> 4) == (_iota((span, B_HEADS * c), 1) >> 6)
    zeros_k = jnp.zeros((B_HEADS * n, B_QK), F32)

    def rows_of(vals, height):
        return jnp.concatenate([jnp.broadcast_to(x, (height, x.shape[1])) for x in vals], axis=0)

    def by_head(x, ok):
        return jnp.where(ok, jnp.concatenate([x] * B_HEADS, axis=0), 0.0)

    def diag_lhs(q, k, bf2, cb2):
        nblk = span // n
        slabs = []
        for s in range(n):
            ks = rows_of([k[n * i + s:n * i + s + 1] for i in range(nblk)], n)
            pieces = []
            for i in range(nblk):
                ref_f = bf2[n * i + s:n * i + s + 1]
                ref_b = cb2[n * i + s:n * i + s + 1]
                for r0 in range(0, n, 8):
                    rs = slice(n * i + r0, n * i + r0 + 8)
                    if s <= r0:
                        pieces.append(bf2[rs] - ref_f)
                    elif s >= r0 + 8:
                        pieces.append(ref_b - cb2[rs])
                    else:
                        pieces.append(jnp.where(row_8 >= s - r0, bf2[rs] - ref_f, ref_b - cb2[rs]))
            arg = jnp.concatenate(pieces, axis=0)
            slabs.append(((q * ks) * jnp.exp2(arg)).astype(BF16))
        return jnp.concatenate(slabs, axis=1)

    def chunk_local(q, k, v, bf, cbx, kf, kb, att_diag):
        e_rows = [bf[n * j + n - 1:n * j + n, :] for j in range(nsub)]
        f_rows = [cbx[n * j:n * j + 1, :] for j in range(nsub)]
        lhs_parts = []
        for j in range(nsub - 1):
            lhs_parts.append(jnp.where(row >= n * (j + 1),
                                       q * jnp.exp(jnp.minimum(bf - e_rows[j], 0.0)), 0.0))
        for j in range(1, nsub):
            lhs_parts.append(jnp.where(row < n * j, q * jnp.exp(jnp.minimum(f_rows[j] - cbx, 0.0)), 0.0))
        rhs_rows = []
        for j in range(nsub):
            kfj = by_head(kf[n * j:n * (j + 1)], head_k_ok)
            kbj = by_head(kb[n * j:n * (j + 1)], head_k_ok)
            parts = [kfj if (jj == j and j < nsub - 1) else zeros_k for jj in range(nsub - 1)]
            parts += [kbj if (jj == j and j > 0) else zeros_k for jj in range(1, nsub)]
            rhs_rows.append(jnp.concatenate(parts, axis=1))
        att = _dot_nt(jnp.concatenate(lhs_parts, axis=1).astype(BF16),
                      jnp.concatenate(rhs_rows, axis=0).astype(BF16))
        v4 = jnp.concatenate([by_head(v[n * j:n * (j + 1)], head_v_ok) for j in range(nsub)],
                             axis=0).astype(BF16)
        return _dot((att + att_diag).astype(BF16), v4)

    def step(it, carry):
        rows = pl.ds(pl.multiple_of(it * span, span), span)
        q = q_ref[rows, :]
        k = k_ref[rows, :]
        v = v_ref[rows, :]
        la = _gla_log_decay(g_ref[rows, :], gup_ref, gbias_ref)
        cums = [_cumsum_rows(la[c * j:c * (j + 1)], tri3_ref) for j in range(grp)]
        cum = jnp.concatenate(cums, axis=0)
        bf = cum[:, :B_QK]
        cbx = cum[:, B_QK:] - la[:, B_QK:]
        tot_f = [cj[c - 1:c, :B_QK] for cj in cums]
        tot_b = [cj[c - 1:c, B_QK:] for cj in cums]
        bf2 = bf * LOG2E
        cb2 = cbx * LOG2E
        e_blk = rows_of([bf[n * j + n - 1:n * j + n, :] for j in range(span // n)], n)
        f_blk = rows_of([cbx[n * j:n * j + 1, :] for j in range(span // n)], n)
        kf = k * jnp.exp(e_blk - bf)
        kb = k * jnp.exp(cbx - f_blk)
        kt = (k * jnp.exp(rows_of(tot_f, c) - bf)).astype(BF16)
        lhs_inter = jnp.concatenate([q * jnp.exp(bf), q * jnp.exp(rows_of(tot_b, c) - cbx)],
                                    axis=1).astype(BF16)
        vb = v.astype(BF16)
        att_diag = jnp.where(own_blk, _dot(diag_lhs(q, k, bf2, cb2), estack_ref[...]), 0.0)

        outs = []
        upd = []
        for j in range(grp):
            sl = slice(c * j, c * (j + 1))
            outs.append(chunk_local(q[sl], k[sl], v[sl], bf[sl], cbx[sl], kf[sl], kb[sl], att_diag[sl]))
            upd.append(_dot_tn(vb[sl], kt[sl]))

        st = state[...]
        for j in range(grp):
            sl = slice(c * j, c * (j + 1))
            sb_full = jnp.where(emask_ref[...] > 0.0,
                                jnp.concatenate([sb_ref[it * grp + j]] * B_HEADS, axis=0), 0.0)
            rhs = jnp.concatenate([st.astype(BF16), sb_full.astype(BF16)], axis=1)
            outs[j] = outs[j] + _dot_nt(lhs_inter[sl], rhs)
            st = st * jnp.exp(tot_f[j]) + upd[j] * emask_ref[...]
        state[...] = st

        gate = r_ref[rows, :]
        o = jnp.concatenate(outs, axis=0)
        out = _head_norm(o, ones_ref, gain_ref[...]) * (gate * _sigmoid(gate))
        o_ref[rows, :] = out.astype(o_ref.dtype)
        return carry

    lax.fori_loop(0, nit, step, 0)


def _gla(z3, gup_bd, gbias, tri3, emask, estack, ones256, gain, tb):
    b, seq, _ = z3.shape
    nblk = seq // tb
    nch = tb // GLA_CHUNK
    ntot = seq // GLA_CHUNK

    def zspec(col, width, rev):
        blk = col // width
        if rev:
            return pl.BlockSpec((None, tb, width), lambda bi, i: (bi, nblk - 1 - i, blk))
        return pl.BlockSpec((None, tb, width), lambda bi, i: (bi, i, blk))

    consts = [gup_bd, gbias, tri3, emask]
    const_specs = [_const_spec(a.shape) for a in consts]
    nit = nch // GLA_GROUP
    sb = pl.pallas_call(
        functools.partial(_gla_bwd_body, nit),
        grid=(b, nblk),
        in_specs=[zspec(Z_BK, B_QK, True), zspec(Z_BV, B_WIDTH, True), zspec(Z_BG, LANES, True)]
        + const_specs,
        out_specs=pl.BlockSpec((None, nch, HEAD_DIM, B_QK), lambda bi, i: (bi, nblk - 1 - i, 0, 0)),
        out_shape=jax.ShapeDtypeStruct((b, ntot, HEAD_DIM, B_QK), BF16),
        scratch_shapes=[pltpu.VMEM((B_WIDTH, B_QK), F32)],
        compiler_params=_cparams(("parallel", "arbitrary")),
        name="gla_bwd",
    )(z3, z3, z3, *consts)

    consts2 = [gup_bd, gbias, tri3, emask, estack, ones256, gain]
    return pl.pallas_call(
        functools.partial(_gla_main_body, nit),
        grid=(b, nblk),
        in_specs=[zspec(Z_BQ, B_QK, False), zspec(Z_BK, B_QK, False), zspec(Z_BV, B_WIDTH, False),
                  zspec(Z_BR, B_WIDTH, False), zspec(Z_BG, LANES, False),
                  pl.BlockSpec((None, nch, HEAD_DIM, B_QK), lambda bi, i: (bi, i, 0, 0))]
        + [_const_spec(a.shape) for a in consts2],
        out_specs=pl.BlockSpec((None, tb, B_WIDTH), lambda bi, i: (bi, i, 0)),
        out_shape=jax.ShapeDtypeStruct((b, seq, B_WIDTH), BF16),
        scratch_shapes=[pltpu.VMEM((B_WIDTH, B_QK), F32)],
        compiler_params=_cparams(("parallel", "arbitrary")),
        name="gla_main",
    )(z3, z3, z3, z3, z3, sb, *consts2)


def _ret_bwd_body(nch, k_ref, v_ref, raw_ref, bmask_ref, rb_ref, state):
    @pl.when(pl.program_id(1) == 0)
    def _():
        state[...] = jnp.zeros_like(state)

    c = RET_CHUNK
    lg1 = _log_sigmoid(raw_ref[1:2, :])
    pos = _iota((c, C_WIDTH), 0).astype(F32)
    kdec = jnp.exp(pos * lg1)
    chunk_decay = jnp.exp(float(c) * lg1)

    upd = []
    for j in range(nch):
        rows = pl.ds(c * j, c)
        kt = (k_ref[rows, :] * kdec).astype(BF16)
        upd.append(_dot_tn(kt, v_ref[rows, :].astype(BF16)))
    st = state[...]
    half = C_WIDTH // 2
    for j in reversed(range(nch)):
        rb_ref[j] = jnp.concatenate([st[:half, :half], st[half:, half:]], axis=0).astype(BF16)
        st = chunk_decay * st + upd[j] * bmask_ref[...]
    state[...] = st


def _ret_main_body(nch, q_ref, k_ref, v_ref, g_ref, rb_ref, raw_ref, raw_s_ref, bmask_ref, ones_ref,
                   gain_ref, o_ref, state):
    @pl.when(pl.program_id(1) == 0)
    def _():
        state[...] = jnp.zeros_like(state)

    c = RET_CHUNK
    lg = _log_sigmoid(raw_ref[...])
    lg0, lg1 = lg[0:1, :], lg[1:2, :]
    lgs = _log_sigmoid(raw_s_ref[...])
    pos = _iota((c, C_WIDTH), 0).astype(F32)
    qdec_f = jnp.exp((pos + 1.0) * lg0)
    qdec_b = jnp.exp((float(c) - pos) * lg1)
    kdec_f = jnp.exp((float(c) - 1.0 - pos) * lg0)
    chunk_decay = jnp.exp(float(c) * lg0)
    t_idx = _iota((c, C_HEADS * c), 0)
    s_idx = _iota((c, C_HEADS * c), 1) & (c - 1)
    rel = (t_idx - s_idx).astype(F32)
    dmat = jnp.where(rel >= 0.0, jnp.exp(jnp.maximum(rel, 0.0) * lgs[0:1, :]),
                     jnp.exp(jnp.maximum(-rel, 0.0) * lgs[1:2, :]))
    rho = _iota((C_HEADS * c, C_WIDTH), 0)
    lane = _iota((C_HEADS * c, C_WIDTH), 1)
    head_ok = (rho >> 7) == (lane >> 6)

    outs, lhs_inter, upd = [], [], []
    for j in range(nch):
        rows = pl.ds(c * j, c)
        q = q_ref[rows, :]
        k = k_ref[rows, :]
        v = v_ref[rows, :]
        k4 = jnp.where(head_ok, jnp.concatenate([k] * C_HEADS, axis=0), 0.0).astype(BF16)
        v4 = jnp.where(head_ok, jnp.concatenate([v] * C_HEADS, axis=0), 0.0).astype(BF16)
        scores = _dot_nt(q.astype(BF16), k4) * dmat
        outs.append(_dot(scores.astype(BF16), v4))
        lhs_inter.append(jnp.concatenate([q * qdec_f, q * qdec_b], axis=1).astype(BF16))
        upd.append(_dot_tn((k * kdec_f).astype(BF16), v.astype(BF16)))
    st = state[...]
    half = C_WIDTH // 2
    zero_q = jnp.zeros((half, half), BF16)
    for j in range(nch):
        rbj = rb_ref[j]
        rb_full = jnp.concatenate([jnp.concatenate([rbj[:half], zero_q], axis=1),
                                   jnp.concatenate([zero_q, rbj[half:]], axis=1)], axis=0)
        rhs = jnp.concatenate([st.astype(BF16), rb_full], axis=0)
        outs[j] = outs[j] + _dot(lhs_inter[j], rhs)
        st = chunk_decay * st + upd[j] * bmask_ref[...]
    state[...] = st
    for j in range(nch):
        rows = pl.ds(c * j, c)
        gate = g_ref[rows, :]
        out = _head_norm(outs[j], ones_ref, gain_ref[...]) * (gate * _sigmoid(gate))
        o_ref[rows, :] = out.astype(o_ref.dtype)


def _ret(z3, raw256, raw512, ones256, gain, tb):
    b, seq, _ = z3.shape
    nblk = seq // tb
    nch = tb // RET_CHUNK
    ntot = seq // RET_CHUNK

    def zspec(col, rev):
        blk = col // C_WIDTH
        if rev:
            return pl.BlockSpec((None, tb, C_WIDTH), lambda bi, i: (bi, nblk - 1 - i, blk))
        return pl.BlockSpec((None, tb, C_WIDTH), lambda bi, i: (bi, i, blk))

    rb = pl.pallas_call(
        functools.partial(_ret_bwd_body, nch),
        grid=(b, nblk),
        in_specs=[zspec(Z_CK, True), zspec(Z_CV, True), _const_spec(raw256.shape),
                  _const_spec(ones256.shape)],
        out_specs=pl.BlockSpec((None, nch, C_WIDTH, C_WIDTH // 2), lambda bi, i: (bi, nblk - 1 - i, 0, 0)),
        out_shape=jax.ShapeDtypeStruct((b, ntot, C_WIDTH, C_WIDTH // 2), BF16),
        scratch_shapes=[pltpu.VMEM((C_WIDTH, C_WIDTH), F32)],
        compiler_params=_cparams(("parallel", "arbitrary")),
        name="ret_bwd",
    )(z3, z3, raw256, ones256)

    consts = [raw256, raw512, ones256, ones256, gain]
    return pl.pallas_call(
        functools.partial(_ret_main_body, nch),
        grid=(b, nblk),
        in_specs=[zspec(Z_CQ, False), zspec(Z_CK, False), zspec(Z_CV, False), zspec(Z_CG, False),
                  pl.BlockSpec((None, nch, C_WIDTH, C_WIDTH // 2), lambda bi, i: (bi, i, 0, 0))]
        + [_const_spec(a.shape) for a in consts],
        out_specs=pl.BlockSpec((None, tb, C_WIDTH), lambda bi, i: (bi, i, 0)),
        out_shape=jax.ShapeDtypeStruct((b, seq, C_WIDTH), BF16),
        scratch_shapes=[pltpu.VMEM((C_WIDTH, C_WIDTH), F32)],
        compiler_params=_cparams(("parallel", "arbitrary")),
        name="ret_main",
    )(z3, z3, z3, z3, rb, *consts)


def _post_body(h_ref, oa_ref, ob_ref, oc_ref, p_ref, wo_ref, lm_ref, w1_ref, w2_ref, lp_ref,
               wg_ref, wp_ref, y_ref):
    def rms(x, g_ref):
        ms = jnp.mean(x * x, axis=-1, keepdims=True)
        return (x * lax.rsqrt(ms + EPS) * g_ref[...]).astype(BF16)

    h = h_ref[...]
    h = h + (_dot(oa_ref[...], wo_ref[0:A_WIDTH, :])
             + _dot(ob_ref[...], wo_ref[A_WIDTH:A_WIDTH + B_WIDTH, :])
             + _dot(oc_ref[...], wo_ref[A_WIDTH + B_WIDTH:, :]))
    m = rms(h, lm_ref)
    ff = D_FF // 4
    mlp = None
    for j in range(4):
        hid = _dot(m, w1_ref[:, ff * j:ff * (j + 1)])
        hid = jnp.square(jnp.maximum(hid, 0.0)).astype(BF16)
        part = _dot(hid, w2_ref[ff * j:ff * (j + 1), :])
        mlp = part if mlp is None else mlp + part
    h = h + mlp
    gate = _sigmoid(_dot(rms(h, lp_ref), wg_ref[...]))
    y_ref[...] = h + gate * _dot(p_ref[...].astype(BF16), wp_ref[...])


def _post(h2, oa, ob, oc, ple, wo, lm, w1, w2, lp, wg, wp, tm):
    n = h2.shape[0]

    def tok(width):
        return pl.BlockSpec((tm, width), lambda i: (i, 0))

    consts = [wo, lm, w1, w2, lp, wg, wp]
    return pl.pallas_call(
        _post_body,
        grid=(n // tm,),
        in_specs=[tok(D_MODEL), tok(A_WIDTH), tok(B_WIDTH), tok(C_WIDTH), tok(PLE_DIM)]
        + [_const_spec(a.shape) for a in consts],
        out_specs=tok(D_MODEL),
        out_shape=jax.ShapeDtypeStruct((n, D_MODEL), F32),
        compiler_params=_cparams(("parallel",)),
        name="post",
    )(h2, oa, ob, oc, ple, *consts)


def _rope_tables(seq, rot_dim, theta):
    half = rot_dim // 2
    inv_freq = 1.0 / (theta ** (jnp.arange(half, dtype=F32) * (2.0 / rot_dim)))
    ang = jnp.arange(seq, dtype=F32)[:, None] * inv_freq[None, :]
    cos, sin = jnp.cos(ang), jnp.sin(ang)
    pad = HEAD_DIM - rot_dim
    c = jnp.concatenate([cos, cos, jnp.ones((seq, pad), F32)], axis=1)
    sn = jnp.concatenate([-sin, jnp.zeros((seq, half + pad), F32)], axis=1)
    sp = jnp.concatenate([jnp.zeros((seq, half), F32), sin, jnp.zeros((seq, pad), F32)], axis=1)
    return jnp.concatenate([jnp.tile(t, (1, LANES // HEAD_DIM)) for t in (c, sn, sp)], axis=1)


def _block_ones(rows, cols, rblk, cblk):
    r = jnp.arange(rows)[:, None] // rblk
    c = jnp.arange(cols)[None, :] // cblk
    return r == c


def _layer_consts(i, ln_mix, w_in, attn_q_norm, attn_k_norm, gla_gate_up, gla_gate_bias, gla_out_norm,
                  ret_decay_raw, ret_out_norm, w_out, ln_mlp, w_mlp_in, w_mlp_out, ln_pe, w_pe_gate,
                  w_pe_proj):
    w = w_in[i]
    w_p = jnp.concatenate([w[:, :2304], w[:, 2336:N_IN], w[:, 2304:2336],
                           jnp.zeros((D_MODEL, Z_WIDTH - N_IN), F32)], axis=1).astype(BF16)
    gup = gla_gate_up[i].astype(BF16)
    gup_bd = jnp.zeros((LANES, 2 * B_QK), BF16)
    gup_bd = gup_bd.at[0:B_GATE_RANK, 0:B_QK].set(gup[0])
    gup_bd = gup_bd.at[B_GATE_RANK:2 * B_GATE_RANK, B_QK:].set(gup[1])
    return dict(
        ln_mix=ln_mix[i][None, :], w_p=w_p,
        qg=jnp.tile(attn_q_norm[i], 4)[None, :], kg=jnp.tile(attn_k_norm[i], 4)[None, :],
        gup_bd=gup_bd, gbias=gla_gate_bias[i].reshape(1, 2 * B_QK),
        gla_gain=gla_out_norm[i][None, :],
        raw256=jnp.repeat(ret_decay_raw[i], C_DIM, axis=1),
        raw512=jnp.repeat(ret_decay_raw[i], RET_CHUNK, axis=1),
        ret_gain=ret_out_norm[i][None, :],
        wo=w_out[i].astype(BF16), lm=ln_mlp[i][None, :], w1=w_mlp_in[i].astype(BF16),
        w2=w_mlp_out[i].astype(BF16), lp=ln_pe[i][None, :], wg=w_pe_gate[i].astype(BF16),
        wp=w_pe_proj[i].astype(BF16),
    )


def _shared_consts():
    tri = jnp.arange(GLA_CHUNK)[:, None] >= jnp.arange(GLA_CHUNK)[None, :]
    return dict(
        ones256=_block_ones(256, 256, HEAD_DIM, HEAD_DIM).astype(BF16),
        tri3=jnp.tile(tri, (1, 3)).astype(BF16),
        emask=_block_ones(B_WIDTH, B_QK, HEAD_DIM, B_KDIM).astype(F32),
        estack=_gla_diag_selector().astype(BF16),
    )


def _gla_diag_selector():
    r = jnp.arange(GLA_SUB * B_QK)
    c = jnp.arange(B_HEADS * GLA_CHUNK)
    same_s = (r // B_QK)[:, None] == (c % GLA_SUB)[None, :]
    same_head = ((r % B_QK) // B_KDIM)[:, None] == ((c % (B_HEADS * GLA_SUB)) // GLA_SUB)[None, :]
    return same_s & same_head


def _run_trunk(x, p, layers, shared):
    b, seq, _ = x.shape
    n = b * seq
    tm = 512
    tb_gla, tb_ret = 2048, 2048
    rope_a = _rope_tables(seq, ROPE_DIM, ROPE_THETA)
    rope_c = _rope_tables(seq, C_DIM, RET_THETA)
    h = x.reshape(n, D_MODEL)
    for i, lc in enumerate(layers):
        z = _proj_in(h, seq, lc["ln_mix"], lc["w_p"], shared["ones256"], lc["qg"], lc["kg"],
                     rope_a, rope_c, tm)
        z3 = z.reshape(b, seq, Z_WIDTH)
        oa = _attn(z3)
        ob = _gla(z3, lc["gup_bd"], lc["gbias"], shared["tri3"], shared["emask"], shared["estack"],
                  shared["ones256"], lc["gla_gain"], tb_gla)
        oc = _ret(z3, lc["raw256"], lc["raw512"], shared["ones256"], lc["ret_gain"], tb_ret)
        h = _post(h, oa.reshape(n, A_WIDTH), ob.reshape(n, B_WIDTH), oc.reshape(n, C_WIDTH),
                  p[i].reshape(n, PLE_DIM), lc["wo"], lc["lm"], lc["w1"], lc["w2"], lc["lp"],
                  lc["wg"], lc["wp"], tm)
    return h.reshape(b, seq, D_MODEL)


def kernel(x_prompt, x_sample, p_prompt, p_sample, ln_mix, w_in, attn_q_norm, attn_k_norm, gla_gate_up, gla_gate_bias, gla_out_norm, ret_decay_raw, ret_out_norm, w_out, ln_mlp, w_mlp_in, w_mlp_out, ln_pe, w_pe_gate, w_pe_proj):
    depth = w_in.shape[0]
    layers = [_layer_consts(i, ln_mix, w_in, attn_q_norm, attn_k_norm, gla_gate_up, gla_gate_bias,
                            gla_out_norm, ret_decay_raw, ret_out_norm, w_out, ln_mlp, w_mlp_in,
                            w_mlp_out, ln_pe, w_pe_gate, w_pe_proj) for i in range(depth)]
    shared = _shared_consts()
    y_prompt = _run_trunk(x_prompt, p_prompt, layers, shared)
    y_sample = _run_trunk(x_sample, p_sample, layers, shared)
    return (y_prompt, y_sample)
```

```python
import functools

import jax
import jax.numpy as jnp
from jax import lax
from jax.experimental import pallas as pl
from jax.experimental.pallas import tpu as pltpu

F32 = jnp.float32
BF16 = jnp.bfloat16

D_MODEL = 1024
HEAD_DIM = 64
A_HEADS = 8
A_WIDTH = 512
A_PATTERNS = ((128, 1), (512, 4), (2048, 16))
A_RADIUS = 64
A_MAX_REACH = 1024
ROPE_THETA = 500000.0
ROPE_DIM = 16
B_HEADS = 4
B_KDIM = 32
B_QK = 128
B_WIDTH = 256
B_GATE_RANK = 16
B_GATE_TAU = 16.0
C_HEADS = 4
C_DIM = 64
C_WIDTH = 256
RET_THETA = 10000.0
N_IN = 3360
D_FF = 4096
PLE_DIM = 256
EPS = 1e-6
NEG = -1e30
LOG2E = 1.4426950408889634

LANES = 128
VMEM_LIMIT = 56 * 1024 * 1024

Z_AQ, Z_AK, Z_AV = 0, 512, 1024
Z_BQ, Z_BK, Z_BV, Z_BR = 1536, 1664, 1792, 2048
Z_CQ, Z_CK, Z_CV, Z_CG = 2304, 2560, 2816, 3072
Z_BG = 3328
Z_WIDTH = 3456

GLA_CHUNK = 64
GLA_SUB = 16
GLA_GROUP = 8
RET_CHUNK = 128
ATT_TILE = 2048
ATT_SUB = 128
ATT_KEYS = ATT_SUB + 2 * A_RADIUS
ATT_SKEW = 2


def _cparams(sem):
    return pltpu.CompilerParams(dimension_semantics=sem, vmem_limit_bytes=VMEM_LIMIT)


def _const_spec(shape):
    nd = len(shape)
    return pl.BlockSpec(shape, lambda *_: (0,) * nd, pipeline_mode=pl.Buffered(1))


def _sigmoid(x):
    return 1.0 / (1.0 + jnp.exp(-x))


def _log_sigmoid(x):
    return jnp.minimum(x, 0.0) - jnp.log1p(jnp.exp(-jnp.abs(x)))


def _iota(shape, dim):
    return lax.broadcasted_iota(jnp.int32, shape, dim)


def _dot(a, b):
    return jnp.dot(a, b, preferred_element_type=F32)


def _dot_nt(a, b):
    return lax.dot_general(a, b, (((1,), (1,)), ((), ())), preferred_element_type=F32)


def _dot_tn(a, b):
    return lax.dot_general(a, b, (((0,), (0,)), ((), ())), preferred_element_type=F32)


def _head_norm(o, ones_ref, gain):
    ssum = _dot((o * o).astype(BF16), ones_ref[...])
    return o * lax.rsqrt(ssum * (1.0 / HEAD_DIM) + EPS) * gain


def _rope(y, tab_ref, shift):
    c = tab_ref[:, 0:LANES]
    sn = tab_ref[:, LANES:2 * LANES]
    sp = tab_ref[:, 2 * LANES:3 * LANES]
    outs = []
    for j in range(y.shape[1] // LANES):
        yj = y[:, j * LANES:(j + 1) * LANES]
        outs.append(yj * c + pltpu.roll(yj, LANES - shift, 1) * sn + pltpu.roll(yj, shift, 1) * sp)
    return jnp.concatenate(outs, axis=1)


def _proj_in_body(x_ref, g_ref, w_ref, ones_ref, qg_ref, kg_ref, ra_ref, rc_ref, z_ref):
    x = x_ref[...]
    ms = jnp.mean(x * x, axis=-1, keepdims=True)
    u = (x * lax.rsqrt(ms + EPS) * g_ref[...]).astype(BF16)

    def proj(a, b):
        return _dot(u, w_ref[:, a:b])

    def qk_norm(y, gain_ref):
        halves = []
        for j in range(2):
            yj = y[:, 256 * j:256 * (j + 1)]
            halves.append(_head_norm(yj, ones_ref, gain_ref[...]))
        return jnp.concatenate(halves, axis=1)

    aq = _rope(qk_norm(proj(Z_AQ, Z_AK), qg_ref), ra_ref, ROPE_DIM // 2)
    z_ref[:, Z_AQ:Z_AK] = aq * (HEAD_DIM ** -0.5 * LOG2E)
    z_ref[:, Z_AK:Z_AV] = _rope(qk_norm(proj(Z_AK, Z_AV), kg_ref), ra_ref, ROPE_DIM // 2)
    z_ref[:, Z_AV:Z_BQ] = proj(Z_AV, Z_BQ)
    z_ref[:, Z_BQ:Z_BK] = proj(Z_BQ, Z_BK) * (B_KDIM ** -0.5)
    z_ref[:, Z_BK:Z_CQ] = proj(Z_BK, Z_CQ)
    z_ref[:, Z_CQ:Z_CK] = _rope(proj(Z_CQ, Z_CK), rc_ref, C_DIM // 2)
    z_ref[:, Z_CK:Z_CV] = _rope(proj(Z_CK, Z_CV), rc_ref, C_DIM // 2) * (C_DIM ** -0.5)
    z_ref[:, Z_CV:Z_WIDTH] = proj(Z_CV, Z_WIDTH)


def _proj_in(x2, seq, ln, w_p, ones256, qg, kg, rope_a, rope_c, tm):
    n = x2.shape[0]
    per_seq = seq // tm
    return pl.pallas_call(
        _proj_in_body,
        grid=(n // tm,),
        in_specs=[
            pl.BlockSpec((tm, D_MODEL), lambda i: (i, 0)),
            _const_spec((1, D_MODEL)),
            _const_spec((D_MODEL, Z_WIDTH)),
            _const_spec((256, 256)),
            _const_spec((1, 256)),
            _const_spec((1, 256)),
            pl.BlockSpec((tm, 3 * LANES), lambda i: (i % per_seq, 0)),
            pl.BlockSpec((tm, 3 * LANES), lambda i: (i % per_seq, 0)),
        ],
        out_specs=pl.BlockSpec((tm, Z_WIDTH), lambda i: (i, 0)),
        out_shape=jax.ShapeDtypeStruct((n, Z_WIDTH), F32),
        compiler_params=_cparams(("parallel",)),
        name="proj_in",
    )(x2, ln, w_p, ones256, qg, kg, rope_a, rope_c)


def _attn_body(seq, q_ref, kp_ref, kc_ref, kn_ref, vp_ref, vc_ref, vn_ref, o_ref,
               kbuf, vbuf, m_ref, l_ref, acc_ref, band_ref):
    t = ATT_TILE
    halo = A_MAX_REACH
    sub = ATT_SUB
    tile_start = pl.program_id(2) * t
    kbuf[0:halo, :] = kp_ref[...]
    kbuf[halo:halo + t, :] = kc_ref[...]
    kbuf[halo + t:, :] = kn_ref[...]
    vbuf[0:halo, :] = vp_ref[...]
    vbuf[halo:halo + t, :] = vc_ref[...]
    vbuf[halo + t:, :] = vn_ref[...]

    low_half = _iota((sub, LANES), 1) < HEAD_DIM
    a_idx = _iota((2 * sub, ATT_KEYS), 0) & (sub - 1)
    rel = _iota((2 * sub, ATT_KEYS), 1) - a_idx
    band_ref[...] = jnp.where((rel >= 0) & (rel <= 2 * A_RADIUS), 0.0, NEG)
    c_row = _iota((1, ATT_KEYS), 1)
    ones_v = jnp.ones((ATT_KEYS, LANES), BF16)

    def scores_stage(tile):
        pat, qs, dil = tile
        span = A_RADIUS * dil
        q_sub = q_ref[pl.ds(qs, sub, stride=dil), :]
        q_st = jnp.concatenate([jnp.where(low_half, q_sub, 0.0),
                                jnp.where(low_half, 0.0, q_sub)], axis=0).astype(BF16)
        k_sub = kbuf[pl.ds(halo + qs - span, ATT_KEYS, stride=dil), :].astype(BF16)
        s = _dot_nt(q_st, k_sub) + band_ref[...]
        if qs - span < 0 or qs - span + dil * (ATT_KEYS - 1) >= t:
            key_pos = (tile_start + qs - span) + dil * c_row
            s = s + jnp.where((key_pos >= 0) & (key_pos < seq), 0.0, NEG)
        return s, jnp.broadcast_to(jnp.max(s, axis=1, keepdims=True), (2 * sub, LANES))

    def values_stage(tile, s, m_row):
        pat, qs, dil = tile
        p = jnp.exp2(s - jnp.concatenate([m_row] * (ATT_KEYS // LANES), axis=1)).astype(BF16)
        v_sub = vbuf[pl.ds(halo + qs - A_RADIUS * dil, ATT_KEYS, stride=dil), :].astype(BF16)
        return _dot(p, jnp.concatenate([v_sub, ones_v], axis=1))

    def store_stage(tile, m_row, pvl):
        pat, qs, dil = tile
        rw = pl.ds(qs, sub, stride=dil)
        m_ref[pat, rw, :] = jnp.where(low_half, m_row[0:sub], m_row[sub:])
        l_ref[pat, rw, :] = jnp.where(low_half, pvl[0:sub, LANES:], pvl[sub:, LANES:])
        acc_ref[pat, rw, :] = jnp.where(low_half, pvl[0:sub, 0:LANES], pvl[sub:, 0:LANES])

    tiles = [(pat, r + j * sub * dil, dil)
             for pat, (_, dil) in enumerate(A_PATTERNS)
             for r in range(dil) for j in range(t // (sub * dil))]
    scored, valued = {}, {}
    for step in range(len(tiles) + 2 * ATT_SKEW):
        i_store, i_val = step - 2 * ATT_SKEW, step - ATT_SKEW
        if 0 <= i_store < len(tiles):
            store_stage(tiles[i_store], *valued.pop(i_store))
        if 0 <= i_val < len(tiles):
            s, m_row = scored.pop(i_val)
            valued[i_val] = (m_row, values_stage(tiles[i_val], s, m_row))
        if step < len(tiles):
            scored[step] = scores_stage(tiles[step])

    npat = len(A_PATTERNS)
    blk = 2 * sub
    for i in range(t // blk):
        rs = pl.ds(i * blk, blk)
        m_g = [m_ref[g, rs, :] for g in range(npat)]
        m_all = functools.reduce(jnp.maximum, m_g)
        w_g = [jnp.exp2(m - m_all) for m in m_g]
        l_all = functools.reduce(lambda a, b: a + b, [w * l_ref[g, rs, :] for g, w in enumerate(w_g)])
        num = functools.reduce(lambda a, b: a + b, [w * acc_ref[g, rs, :] for g, w in enumerate(w_g)])
        o_ref[rs, :] = (num / l_all).astype(o_ref.dtype)


def _attn(z3):
    b, seq, _ = z3.shape
    t = ATT_TILE
    halo = A_MAX_REACH
    nt = seq // t
    per_tile = t // halo
    n_halo = seq // halo
    qc, kc, vc = Z_AQ // LANES, Z_AK // LANES, Z_AV // LANES

    def cur(c0):
        return pl.BlockSpec((None, t, LANES), lambda bi, hp, i: (bi, i, c0 + hp))

    def prev(c0):
        return pl.BlockSpec((None, halo, LANES),
                            lambda bi, hp, i: (bi, jnp.maximum(i * per_tile - 1, 0), c0 + hp))

    def nxt(c0):
        return pl.BlockSpec((None, halo, LANES),
                            lambda bi, hp, i: (bi, jnp.minimum((i + 1) * per_tile, n_halo - 1), c0 + hp))

    return pl.pallas_call(
        functools.partial(_attn_body, seq),
        grid=(b, A_HEADS // 2, nt),
        in_specs=[cur(qc), prev(kc), cur(kc), nxt(kc), prev(vc), cur(vc), nxt(vc)],
        out_specs=pl.BlockSpec((None, t, LANES), lambda bi, hp, i: (bi, i, hp)),
        out_shape=jax.ShapeDtypeStruct((b, seq, A_WIDTH), BF16),
        scratch_shapes=[
            pltpu.VMEM((t + 2 * halo, LANES), F32),
            pltpu.VMEM((t + 2 * halo, LANES), F32),
            pltpu.VMEM((len(A_PATTERNS), t, LANES), F32),
            pltpu.VMEM((len(A_PATTERNS), t, LANES), F32),
            pltpu.VMEM((len(A_PATTERNS), t, LANES), F32),
            pltpu.VMEM((2 * ATT_SUB, ATT_KEYS), F32),
        ],
        compiler_params=_cparams(("parallel", "parallel", "parallel")),
        name="attn",
    )(z3, z3, z3, z3, z3, z3, z3)


def _split3(x):
    hi = x.astype(BF16)
    r1 = x - hi.astype(F32)
    mid = r1.astype(BF16)
    lo = (r1 - mid.astype(F32)).astype(BF16)
    return hi, mid, lo


def _gla_log_decay(g, gup_ref, gbias_ref):
    logits = _dot(g.astype(BF16), gup_ref[...]) + gbias_ref[...]
    return _log_sigmoid(logits) * (1.0 / B_GATE_TAU)


def _cumsum_rows(la, tri3_ref):
    hi, mid, lo = _split3(la)
    return _dot(tri3_ref[...], jnp.concatenate([hi, mid, lo], axis=0))


def _gla_bwd_body(nit, k_ref, v_ref, g_ref, gup_ref, gbias_ref, tri3_ref, emask_ref, sb_ref, state):
    @pl.when(pl.program_id(1) == 0)
    def _():
        state[...] = jnp.zeros_like(state)

    c = GLA_CHUNK
    grp = GLA_GROUP
    span = grp * c

    def step(i, carry):
        it = nit - 1 - i
        rows = pl.ds(pl.multiple_of(it * span, span), span)
        logits = _dot(g_ref[rows, :].astype(BF16), gup_ref[:, B_QK:]) + gbias_ref[:, B_QK:]
        la = _log_sigmoid(logits) * (1.0 / B_GATE_TAU)
        cums = [_cumsum_rows(la[c * j:c * (j + 1)], tri3_ref) for j in range(grp)]
        cbx = jnp.concatenate(cums, axis=0) - la
        kt = (k_ref[rows, :] * jnp.exp(cbx)).astype(BF16)
        vb = v_ref[rows, :].astype(BF16)
        upd = [_dot_tn(vb[c * j:c * (j + 1)], kt[c * j:c * (j + 1)]) for j in range(grp)]
        st = state[...]
        for j in reversed(range(grp)):
            sb_ref[it * grp + j] = functools.reduce(
                lambda a, b: a + b, [st[HEAD_DIM * h:HEAD_DIM * (h + 1)] for h in range(B_HEADS)]).astype(BF16)
            st = st * jnp.exp(cums[j][c - 1:c, :]) + upd[j] * emask_ref[...]
        state[...] = st
        return carry

    lax.fori_loop(0, nit, step, 0)


def _gla_main_body(nit, q_ref, k_ref, v_ref, r_ref, g_ref, sb_ref, gup_ref, gbias_ref, tri3_ref,
                   emask_ref, estack_ref, ones_ref, gain_ref, o_ref, state):
    @pl.when(pl.program_id(1) == 0)
    def _():
        state[...] = jnp.zeros_like(state)

    c = GLA_CHUNK
    n = GLA_SUB
    nsub = c // n
    grp = GLA_GROUP
    span = grp * c
    row_8 = _iota((8, B_QK), 0)
    rho = _iota((B_HEADS * n, B_QK), 0)
    head_k_ok = (rho >> 4) == (_iota((B_HEADS * n, B_QK), 1) >> 5)
    rho_v = _iota((B_HEADS * n, B_WIDTH), 0)
    head_v_ok = (rho_v >> 4) == (_iota((B_HEADS * n, B_WIDTH), 1) >> 6)
    own_blk = ((_iota((span, B_HEADS * c), 0) & (c - 1)) >> 4) == (_iota((span, B_HEADS * c), 1) >> 6)
    zeros_k = jnp.zeros((B_HEADS * n, B_QK), F32)

    def rows_of(vals, height):
        return jnp.concatenate([jnp.broadcast_to(x, (height, x.shape[1])) for x in vals], axis=0)

    def by_head(x, ok):
        return jnp.where(ok, jnp.concatenate([x] * B_HEADS, axis=0), 0.0)

    def diag_lhs(q, k, bf2, cb2):
        nblk = span // n
        slabs = []
        for s in range(n):
            ks = rows_of([k[n * i + s:n * i + s + 1] for i in range(nblk)], n)
            pieces = []
            for i in range(nblk):
                ref_f = bf2[n * i + s:n * i + s + 1]
                ref_b = cb2[n * i + s:n * i + s + 1]
                for r0 in range(0, n, 8):
                    rs = slice(n * i + r0, n * i + r0 + 8)
                    if s <= r0:
                        pieces.append(bf2[rs] - ref_f)
                    elif s >= r0 + 8:
                        pieces.append(ref_b - cb2[rs])
                    else:
                        pieces.append(jnp.where(row_8 >= s - r0, bf2[rs] - ref_f, ref_b - cb2[rs]))
            arg = jnp.concatenate(pieces, axis=0)
            slabs.append(((q * ks) * jnp.exp2(arg)).astype(BF16))
        return jnp.concatenate(slabs, axis=1)

    def chunk_local(q, k, v, bf, cbx, kf, kb, att_diag):
        e_rows = [bf[n * j + n - 1:n * j + n, :] for j in range(nsub)]
        f_rows = [cbx[n * j:n * j + 1, :] for j in range(nsub)]
        lhs_parts = []
        for j in range(nsub - 1):
            r0 = n * (j + 1)
            part = q[r0:] * jnp.exp(bf[r0:] - e_rows[j])
            lhs_parts.append(jnp.concatenate([jnp.zeros((r0, B_QK), F32), part], axis=0))
        for j in range(1, nsub):
            r1 = n * j
            part = q[:r1] * jnp.exp(f_rows[j] - cbx[:r1])
            lhs_parts.append(jnp.concatenate([part, jnp.zeros((c - r1, B_QK), F32)], axis=0))
        rhs_rows = []
        for j in range(nsub):
            kfj = by_head(kf[n * j:n * (j + 1)], head_k_ok)
            kbj = by_head(kb[n * j:n * (j + 1)], head_k_ok)
            parts = [kfj if (jj == j and j < nsub - 1) else zeros_k for jj in range(nsub - 1)]
            parts += [kbj if (jj == j and j > 0) else zeros_k for jj in range(1, nsub)]
            rhs_rows.append(jnp.concatenate(parts, axis=1))
        att = _dot_nt(jnp.concatenate(lhs_parts, axis=1).astype(BF16),
                      jnp.concatenate(rhs_rows, axis=0).astype(BF16))
        v4 = jnp.concatenate([by_head(v[n * j:n * (j + 1)], head_v_ok) for j in range(nsub)],
                             axis=0).astype(BF16)
        return _dot((att + att_diag).astype(BF16), v4)

    def step(it, carry):
        rows = pl.ds(pl.multiple_of(it * span, span), span)
        q = q_ref[rows, :]
        k = k_ref[rows, :]
        v = v_ref[rows, :]
        la = _gla_log_decay(g_ref[rows, :], gup_ref, gbias_ref)
        cums = [_cumsum_rows(la[c * j:c * (j + 1)], tri3_ref) for j in range(grp)]
        cum = jnp.concatenate(cums, axis=0)
        bf = cum[:, :B_QK]
        cbx = cum[:, B_QK:] - la[:, B_QK:]
        tot_f = [cj[c - 1:c, :B_QK] for cj in cums]
        tot_b = [cj[c - 1:c, B_QK:] for cj in cums]
        bf2 = bf * LOG2E
        cb2 = cbx * LOG2E
        e_blk = rows_of([bf[n * j + n - 1:n * j + n, :] for j in range(span // n)], n)
        f_blk = rows_of([cbx[n * j:n * j + 1, :] for j in range(span // n)], n)
        kf = k * jnp.exp(e_blk - bf)
        kb = k * jnp.exp(cbx - f_blk)
        kt = (k * jnp.exp(rows_of(tot_f, c) - bf)).astype(BF16)
        lhs_inter = jnp.concatenate([q * jnp.exp(bf), q * jnp.exp(rows_of(tot_b, c) - cbx)],
                                    axis=1).astype(BF16)
        vb = v.astype(BF16)
        att_diag = jnp.where(own_blk, _dot(diag_lhs(q, k, bf2, cb2), estack_ref[...]), 0.0)

        outs = []
        upd = []
        for j in range(grp):
            sl = slice(c * j, c * (j + 1))
            outs.append(chunk_local(q[sl], k[sl], v[sl], bf[sl], cbx[sl], kf[sl], kb[sl], att_diag[sl]))
            upd.append(_dot_tn(vb[sl], kt[sl]))

        st = state[...]
        for j in range(grp):
            sl = slice(c * j, c * (j + 1))
            sb_full = jnp.where(emask_ref[...] > 0.0,
                                jnp.concatenate([sb_ref[it * grp + j]] * B_HEADS, axis=0), 0.0)
            rhs = jnp.concatenate([st.astype(BF16), sb_full.astype(BF16)], axis=1)
            outs[j] = outs[j] + _dot_nt(lhs_inter[sl], rhs)
            st = st * jnp.exp(tot_f[j]) + upd[j] * emask_ref[...]
        state[...] = st

        gate = r_ref[rows, :]
        o = jnp.concatenate(outs, axis=0)
        out = _head_norm(o, ones_ref, gain_ref[...]) * (gate * _sigmoid(gate))
        o_ref[rows, :] = out.astype(o_ref.dtype)
        return carry

    lax.fori_loop(0, nit, step, 0)


def _gla(z3, gup_bd, gbias, tri3, emask, estack, ones256, gain, tb):
    b, seq, _ = z3.shape
    nblk = seq // tb
    nch = tb // GLA_CHUNK
    ntot = seq // GLA_CHUNK

    def zspec(col, width, rev):
        blk = col // width
        if rev:
            return pl.BlockSpec((None, tb, width), lambda bi, i: (bi, nblk - 1 - i, blk))
        return pl.BlockSpec((None, tb, width), lambda bi, i: (bi, i, blk))

    consts = [gup_bd, gbias, tri3, emask]
    const_specs = [_const_spec(a.shape) for a in consts]
    nit = nch // GLA_GROUP
    sb = pl.pallas_call(
        functools.partial(_gla_bwd_body, nit),
        grid=(b, nblk),
        in_specs=[zspec(Z_BK, B_QK, True), zspec(Z_BV, B_WIDTH, True), zspec(Z_BG, LANES, True)]
        + const_specs,
        out_specs=pl.BlockSpec((None, nch, HEAD_DIM, B_QK), lambda bi, i: (bi, nblk - 1 - i, 0, 0)),
        out_shape=jax.ShapeDtypeStruct((b, ntot, HEAD_DIM, B_QK), BF16),
        scratch_shapes=[pltpu.VMEM((B_WIDTH, B_QK), F32)],
        compiler_params=_cparams(("parallel", "arbitrary")),
        name="gla_bwd",
    )(z3, z3, z3, *consts)

    consts2 = [gup_bd, gbias, tri3, emask, estack, ones256, gain]
    return pl.pallas_call(
        functools.partial(_gla_main_body, nit),
        grid=(b, nblk),
        in_specs=[zspec(Z_BQ, B_QK, False), zspec(Z_BK, B_QK, False), zspec(Z_BV, B_WIDTH, False),
                  zspec(Z_BR, B_WIDTH, False), zspec(Z_BG, LANES, False),
                  pl.BlockSpec((None, nch, HEAD_DIM, B_QK), lambda bi, i: (bi, i, 0, 0))]
        + [_const_spec(a.shape) for a in consts2],
        out_specs=pl.BlockSpec((None, tb, B_WIDTH), lambda bi, i: (bi, i, 0)),
        out_shape=jax.ShapeDtypeStruct((b, seq, B_WIDTH), BF16),
        scratch_shapes=[pltpu.VMEM((B_WIDTH, B_QK), F32)],
        compiler_params=_cparams(("parallel", "arbitrary")),
        name="gla_main",
    )(z3, z3, z3, z3, z3, sb, *consts2)


def _ret_bwd_body(nch, k_ref, v_ref, raw_ref, bmask_ref, rb_ref, state):
    @pl.when(pl.program_id(1) == 0)
    def _():
        state[...] = jnp.zeros_like(state)

    c = RET_CHUNK
    lg1 = _log_sigmoid(raw_ref[1:2, :])
    pos = _iota((c, C_WIDTH), 0).astype(F32)
    kdec = jnp.exp(pos * lg1)
    chunk_decay = jnp.exp(float(c) * lg1)

    upd = []
    for j in range(nch):
        rows = pl.ds(c * j, c)
        kt = (k_ref[rows, :] * kdec).astype(BF16)
        upd.append(_dot_tn(kt, v_ref[rows, :].astype(BF16)))
    st = state[...]
    half = C_WIDTH // 2
    for j in reversed(range(nch)):
        rb_ref[j] = jnp.concatenate([st[:half, :half], st[half:, half:]], axis=0).astype(BF16)
        st = chunk_decay * st + upd[j] * bmask_ref[...]
    state[...] = st


def _ret_main_body(nch, q_ref, k_ref, v_ref, g_ref, rb_ref, raw_ref, raw_s_ref, bmask_ref, ones_ref,
                   gain_ref, o_ref, state):
    @pl.when(pl.program_id(1) == 0)
    def _():
        state[...] = jnp.zeros_like(state)

    c = RET_CHUNK
    lg = _log_sigmoid(raw_ref[...])
    lg0, lg1 = lg[0:1, :], lg[1:2, :]
    lgs = _log_sigmoid(raw_s_ref[...])
    pos = _iota((c, C_WIDTH), 0).astype(F32)
    qdec_f = jnp.exp((pos + 1.0) * lg0)
    qdec_b = jnp.exp((float(c) - pos) * lg1)
    kdec_f = jnp.exp((float(c) - 1.0 - pos) * lg0)
    chunk_decay = jnp.exp(float(c) * lg0)
    t_idx = _iota((c, C_HEADS * c), 0)
    s_idx = _iota((c, C_HEADS * c), 1) & (c - 1)
    rel = (t_idx - s_idx).astype(F32)
    dmat = jnp.where(rel >= 0.0, jnp.exp(jnp.maximum(rel, 0.0) * lgs[0:1, :]),
                     jnp.exp(jnp.maximum(-rel, 0.0) * lgs[1:2, :]))
    rho = _iota((C_HEADS * c, C_WIDTH), 0)
    lane = _iota((C_HEADS * c, C_WIDTH), 1)
    head_ok = (rho >> 7) == (lane >> 6)

    outs, lhs_inter, upd = [], [], []
    for j in range(nch):
        rows = pl.ds(c * j, c)
        q = q_ref[rows, :]
        k = k_ref[rows, :]
        v = v_ref[rows, :]
        k4 = jnp.where(head_ok, jnp.concatenate([k] * C_HEADS, axis=0), 0.0).astype(BF16)
        v4 = jnp.where(head_ok, jnp.concatenate([v] * C_HEADS, axis=0), 0.0).astype(BF16)
        scores = _dot_nt(q.astype(BF16), k4) * dmat
        outs.append(_dot(scores.astype(BF16), v4))
        lhs_inter.append(jnp.concatenate([q * qdec_f, q * qdec_b], axis=1).astype(BF16))
        upd.append(_dot_tn((k * kdec_f).astype(BF16), v.astype(BF16)))
    st = state[...]
    half = C_WIDTH // 2
    zero_q = jnp.zeros((half, half), BF16)
    for j in range(nch):
        rbj = rb_ref[j]
        rb_full = jnp.concatenate([jnp.concatenate([rbj[:half], zero_q], axis=1),
                                   jnp.concatenate([zero_q, rbj[half:]], axis=1)], axis=0)
        rhs = jnp.concatenate([st.astype(BF16), rb_full], axis=0)
        outs[j] = outs[j] + _dot(lhs_inter[j], rhs)
        st = chunk_decay * st + upd[j] * bmask_ref[...]
    state[...] = st
    for j in range(nch):
        rows = pl.ds(c * j, c)
        gate = g_ref[rows, :]
        out = _head_norm(outs[j], ones_ref, gain_ref[...]) * (gate * _sigmoid(gate))
        o_ref[rows, :] = out.astype(o_ref.dtype)


def _ret(z3, raw256, raw512, ones256, gain, tb):
    b, seq, _ = z3.shape
    nblk = seq // tb
    nch = tb // RET_CHUNK
    ntot = seq // RET_CHUNK

    def zspec(col, rev):
        blk = col // C_WIDTH
        if rev:
            return pl.BlockSpec((None, tb, C_WIDTH), lambda bi, i: (bi, nblk - 1 - i, blk))
        return pl.BlockSpec((None, tb, C_WIDTH), lambda bi, i: (bi, i, blk))

    rb = pl.pallas_call(
        functools.partial(_ret_bwd_body, nch),
        grid=(b, nblk),
        in_specs=[zspec(Z_CK, True), zspec(Z_CV, True), _const_spec(raw256.shape),
                  _const_spec(ones256.shape)],
        out_specs=pl.BlockSpec((None, nch, C_WIDTH, C_WIDTH // 2), lambda bi, i: (bi, nblk - 1 - i, 0, 0)),
        out_shape=jax.ShapeDtypeStruct((b, ntot, C_WIDTH, C_WIDTH // 2), BF16),
        scratch_shapes=[pltpu.VMEM((C_WIDTH, C_WIDTH), F32)],
        compiler_params=_cparams(("parallel", "arbitrary")),
        name="ret_bwd",
    )(z3, z3, raw256, ones256)

    consts = [raw256, raw512, ones256, ones256, gain]
    return pl.pallas_call(
        functools.partial(_ret_main_body, nch),
        grid=(b, nblk),
        in_specs=[zspec(Z_CQ, False), zspec(Z_CK, False), zspec(Z_CV, False), zspec(Z_CG, False),
                  pl.BlockSpec((None, nch, C_WIDTH, C_WIDTH // 2), lambda bi, i: (bi, i, 0, 0))]
        + [_const_spec(a.shape) for a in consts],
        out_specs=pl.BlockSpec((None, tb, C_WIDTH), lambda bi, i: (bi, i, 0)),
        out_shape=jax.ShapeDtypeStruct((b, seq, C_WIDTH), BF16),
        scratch_shapes=[pltpu.VMEM((C_WIDTH, C_WIDTH), F32)],
        compiler_params=_cparams(("parallel", "arbitrary")),
        name="ret_main",
    )(z3, z3, z3, z3, rb, *consts)


def _post_body(h_ref, oa_ref, ob_ref, oc_ref, p_ref, wo_ref, lm_ref, w1_ref, w2_ref, lp_ref,
               wg_ref, wp_ref, y_ref):
    def rms(x, g_ref):
        ms = jnp.mean(x * x, axis=-1, keepdims=True)
        return (x * lax.rsqrt(ms + EPS) * g_ref[...]).astype(BF16)

    h = h_ref[...]
    h = h + (_dot(oa_ref[...], wo_ref[0:A_WIDTH, :])
             + _dot(ob_ref[...], wo_ref[A_WIDTH:A_WIDTH + B_WIDTH, :])
             + _dot(oc_ref[...], wo_ref[A_WIDTH + B_WIDTH:, :]))
    m = rms(h, lm_ref)
    ff = D_FF // 4
    mlp = None
    for j in range(4):
        hid = _dot(m, w1_ref[:, ff * j:ff * (j + 1)])
        hid = jnp.square(jnp.maximum(hid, 0.0)).astype(BF16)
        part = _dot(hid, w2_ref[ff * j:ff * (j + 1), :])
        mlp = part if mlp is None else mlp + part
    h = h + mlp
    gate = _sigmoid(_dot(rms(h, lp_ref), wg_ref[...]))
    y_ref[...] = h + gate * _dot(p_ref[...].astype(BF16), wp_ref[...])


def _post(h2, oa, ob, oc, ple, wo, lm, w1, w2, lp, wg, wp, tm):
    n = h2.shape[0]

    def tok(width):
        return pl.BlockSpec((tm, width), lambda i: (i, 0))

    consts = [wo, lm, w1, w2, lp, wg, wp]
    return pl.pallas_call(
        _post_body,
        grid=(n // tm,),
        in_specs=[tok(D_MODEL), tok(A_WIDTH), tok(B_WIDTH), tok(C_WIDTH), tok(PLE_DIM)]
        + [_const_spec(a.shape) for a in consts],
        out_specs=tok(D_MODEL),
        out_shape=jax.ShapeDtypeStruct((n, D_MODEL), F32),
        compiler_params=_cparams(("parallel",)),
        name="post",
    )(h2, oa, ob, oc, ple, *consts)


def _rope_tables(seq, rot_dim, theta):
    half = rot_dim // 2
    inv_freq = 1.0 / (theta ** (jnp.arange(half, dtype=F32) * (2.0 / rot_dim)))
    ang = jnp.arange(seq, dtype=F32)[:, None] * inv_freq[None, :]
    cos, sin = jnp.cos(ang), jnp.sin(ang)
    pad = HEAD_DIM - rot_dim
    c = jnp.concatenate([cos, cos, jnp.ones((seq, pad), F32)], axis=1)
    sn = jnp.concatenate([-sin, jnp.zeros((seq, half + pad), F32)], axis=1)
    sp = jnp.concatenate([jnp.zeros((seq, half), F32), sin, jnp.zeros((seq, pad), F32)], axis=1)
    return jnp.concatenate([jnp.tile(t, (1, LANES // HEAD_DIM)) for t in (c, sn, sp)], axis=1)


def _block_ones(rows, cols, rblk, cblk):
    r = jnp.arange(rows)[:, None] // rblk
    c = jnp.arange(cols)[None, :] // cblk
    return r == c


def _layer_consts(i, ln_mix, w_in, attn_q_norm, attn_k_norm, gla_gate_up, gla_gate_bias, gla_out_norm,
                  ret_decay_raw, ret_out_norm, w_out, ln_mlp, w_mlp_in, w_mlp_out, ln_pe, w_pe_gate,
                  w_pe_proj):
    w = w_in[i]
    w_p = jnp.concatenate([w[:, :2304], w[:, 2336:N_IN], w[:, 2304:2336],
                           jnp.zeros((D_MODEL, Z_WIDTH - N_IN), F32)], axis=1).astype(BF16)
    gup = gla_gate_up[i].astype(BF16)
    gup_bd = jnp.zeros((LANES, 2 * B_QK), BF16)
    gup_bd = gup_bd.at[0:B_GATE_RANK, 0:B_QK].set(gup[0])
    gup_bd = gup_bd.at[B_GATE_RANK:2 * B_GATE_RANK, B_QK:].set(gup[1])
    return dict(
        ln_mix=ln_mix[i][None, :], w_p=w_p,
        qg=jnp.tile(attn_q_norm[i], 4)[None, :], kg=jnp.tile(attn_k_norm[i], 4)[None, :],
        gup_bd=gup_bd, gbias=gla_gate_bias[i].reshape(1, 2 * B_QK),
        gla_gain=gla_out_norm[i][None, :],
        raw256=jnp.repeat(ret_decay_raw[i], C_DIM, axis=1),
        raw512=jnp.repeat(ret_decay_raw[i], RET_CHUNK, axis=1),
        ret_gain=ret_out_norm[i][None, :],
        wo=w_out[i].astype(BF16), lm=ln_mlp[i][None, :], w1=w_mlp_in[i].astype(BF16),
        w2=w_mlp_out[i].astype(BF16), lp=ln_pe[i][None, :], wg=w_pe_gate[i].astype(BF16),
        wp=w_pe_proj[i].astype(BF16),
    )


def _shared_consts():
    tri = jnp.arange(GLA_CHUNK)[:, None] >= jnp.arange(GLA_CHUNK)[None, :]
    return dict(
        ones256=_block_ones(256, 256, HEAD_DIM, HEAD_DIM).astype(BF16),
        tri3=jnp.tile(tri, (1, 3)).astype(BF16),
        emask=_block_ones(B_WIDTH, B_QK, HEAD_DIM, B_KDIM).astype(F32),
        estack=_gla_diag_selector().astype(BF16),
    )


def _gla_diag_selector():
    r = jnp.arange(GLA_SUB * B_QK)
    c = jnp.arange(B_HEADS * GLA_CHUNK)
    same_s = (r // B_QK)[:, None] == (c % GLA_SUB)[None, :]
    same_head = ((r % B_QK) // B_KDIM)[:, None] == ((c % (B_HEADS * GLA_SUB)) // GLA_SUB)[None, :]
    return same_s & same_head


def _run_trunk(x, p, layers, shared):
    b, seq, _ = x.shape
    n = b * seq
    tm = 512
    tb_gla, tb_ret = 2048, 2048
    rope_a = _rope_tables(seq, ROPE_DIM, ROPE_THETA)
    rope_c = _rope_tables(seq, C_DIM, RET_THETA)
    h = x.reshape(n, D_MODEL)
    for i, lc in enumerate(layers):
        z = _proj_in(h, seq, lc["ln_mix"], lc["w_p"], shared["ones256"], lc["qg"], lc["kg"],
                     rope_a, rope_c, tm)
        z3 = z.reshape(b, seq, Z_WIDTH)
        oa = _attn(z3)
        ob = _gla(z3, lc["gup_bd"], lc["gbias"], shared["tri3"], shared["emask"], shared["estack"],
                  shared["ones256"], lc["gla_gain"], tb_gla)
        oc = _ret(z3, lc["raw256"], lc["raw512"], shared["ones256"], lc["ret_gain"], tb_ret)
        h = _post(h, oa.reshape(n, A_WIDTH), ob.reshape(n, B_WIDTH), oc.reshape(n, C_WIDTH),
                  p[i].reshape(n, PLE_DIM), lc["wo"], lc["lm"], lc["w1"], lc["w2"], lc["lp"],
                  lc["wg"], lc["wp"], tm)
    return h.reshape(b, seq, D_MODEL)


def kernel(x_prompt, x_sample, p_prompt, p_sample, ln_mix, w_in, attn_q_norm, attn_k_norm, gla_gate_up, gla_gate_bias, gla_out_norm, ret_decay_raw, ret_out_norm, w_out, ln_mlp, w_mlp_in, w_mlp_out, ln_pe, w_pe_gate, w_pe_proj):
    depth = w_in.shape[0]
    layers = [_layer_consts(i, ln_mix, w_in, attn_q_norm, attn_k_norm, gla_gate_up, gla_gate_bias,
                            gla_out_norm, ret_decay_raw, ret_out_norm, w_out, ln_mlp, w_mlp_in,
                            w_mlp_out, ln_pe, w_pe_gate, w_pe_proj) for i in range(depth)]
    shared = _shared_consts()
    y_prompt = _run_trunk(x_prompt, p_prompt, layers, shared)
    y_sample = _run_trunk(x_sample, p_sample, layers, shared)
    return (y_prompt, y_sample)
```

```python
import functools

import jax
import jax.numpy as jnp
from jax import lax
from jax.experimental import pallas as pl
from jax.experimental.pallas import tpu as pltpu

F32 = jnp.float32
BF16 = jnp.bfloat16

D_MODEL = 1024
HEAD_DIM = 64
A_HEADS = 8
A_WIDTH = 512
A_PATTERNS = ((128, 1), (512, 4), (2048, 16))
A_RADIUS = 64
A_MAX_REACH = 1024
ROPE_THETA = 500000.0
ROPE_DIM = 16
B_HEADS = 4
B_KDIM = 32
B_QK = 128
B_WIDTH = 256
B_GATE_RANK = 16
B_GATE_TAU = 16.0
C_HEADS = 4
C_DIM = 64
C_WIDTH = 256
RET_THETA = 10000.0
N_IN = 3360
D_FF = 4096
PLE_DIM = 256
EPS = 1e-6
NEG = -1e30
LOG2E = 1.4426950408889634

LANES = 128
VMEM_LIMIT = 56 * 1024 * 1024

Z_AQ, Z_AK, Z_AV = 0, 512, 1024
Z_BQ, Z_BK, Z_BV, Z_BR = 1536, 1664, 1792, 2048
Z_CQ, Z_CK, Z_CV, Z_CG = 2304, 2560, 2816, 3072
Z_BG = 3328
Z_WIDTH = 3456

GLA_CHUNK = 64
GLA_SUB = 16
GLA_GROUP = 8
RET_CHUNK = 128
ATT_TILE = 2048
ATT_SUB = 128
ATT_KEYS = ATT_SUB + 2 * A_RADIUS
ATT_SKEW = 2


def _cparams(sem):
    return pltpu.CompilerParams(dimension_semantics=sem, vmem_limit_bytes=VMEM_LIMIT)


def _const_spec(shape):
    nd = len(shape)
    return pl.BlockSpec(shape, lambda *_: (0,) * nd, pipeline_mode=pl.Buffered(1))


def _sigmoid(x):
    return 1.0 / (1.0 + jnp.exp(-x))


def _log_sigmoid(x):
    return jnp.minimum(x, 0.0) - jnp.log1p(jnp.exp(-jnp.abs(x)))


def _iota(shape, dim):
    return lax.broadcasted_iota(jnp.int32, shape, dim)


def _dot(a, b):
    return jnp.dot(a, b, preferred_element_type=F32)


def _dot_nt(a, b):
    return lax.dot_general(a, b, (((1,), (1,)), ((), ())), preferred_element_type=F32)


def _dot_tn(a, b):
    return lax.dot_general(a, b, (((0,), (0,)), ((), ())), preferred_element_type=F32)


def _head_norm(o, ones_ref, gain):
    ssum = _dot((o * o).astype(BF16), ones_ref[...])
    return o * lax.rsqrt(ssum * (1.0 / HEAD_DIM) + EPS) * gain


def _rope(y, tab_ref, shift):
    c = tab_ref[:, 0:LANES]
    sn = tab_ref[:, LANES:2 * LANES]
    sp = tab_ref[:, 2 * LANES:3 * LANES]
    outs = []
    for j in range(y.shape[1] // LANES):
        yj = y[:, j * LANES:(j + 1) * LANES]
        outs.append(yj * c + pltpu.roll(yj, LANES - shift, 1) * sn + pltpu.roll(yj, shift, 1) * sp)
    return jnp.concatenate(outs, axis=1)


def _proj_in_body(x_ref, g_ref, w_ref, ones_ref, qg_ref, kg_ref, ra_ref, rc_ref, z_ref):
    x = x_ref[...]
    ms = jnp.mean(x * x, axis=-1, keepdims=True)
    u = (x * lax.rsqrt(ms + EPS) * g_ref[...]).astype(BF16)

    def proj(a, b):
        return _dot(u, w_ref[:, a:b])

    def qk_norm(y, gain_ref):
        halves = []
        for j in range(2):
            yj = y[:, 256 * j:256 * (j + 1)]
            halves.append(_head_norm(yj, ones_ref, gain_ref[...]))
        return jnp.concatenate(halves, axis=1)

    aq = _rope(qk_norm(proj(Z_AQ, Z_AK), qg_ref), ra_ref, ROPE_DIM // 2)
    z_ref[:, Z_AQ:Z_AK] = aq * (HEAD_DIM ** -0.5 * LOG2E)
    z_ref[:, Z_AK:Z_AV] = _rope(qk_norm(proj(Z_AK, Z_AV), kg_ref), ra_ref, ROPE_DIM // 2)
    z_ref[:, Z_AV:Z_BQ] = proj(Z_AV, Z_BQ)
    z_ref[:, Z_BQ:Z_BK] = proj(Z_BQ, Z_BK) * (B_KDIM ** -0.5)
    z_ref[:, Z_BK:Z_CQ] = proj(Z_BK, Z_CQ)
    z_ref[:, Z_CQ:Z_CK] = _rope(proj(Z_CQ, Z_CK), rc_ref, C_DIM // 2)
    z_ref[:, Z_CK:Z_CV] = _rope(proj(Z_CK, Z_CV), rc_ref, C_DIM // 2) * (C_DIM ** -0.5)
    z_ref[:, Z_CV:Z_WIDTH] = proj(Z_CV, Z_WIDTH)


def _proj_in(x2, seq, ln, w_p, ones256, qg, kg, rope_a, rope_c, tm):
    n = x2.shape[0]
    per_seq = seq // tm
    return pl.pallas_call(
        _proj_in_body,
        grid=(n // tm,),
        in_specs=[
            pl.BlockSpec((tm, D_MODEL), lambda i: (i, 0)),
            _const_spec((1, D_MODEL)),
            _const_spec((D_MODEL, Z_WIDTH)),
            _const_spec((256, 256)),
            _const_spec((1, 256)),
            _const_spec((1, 256)),
            pl.BlockSpec((tm, 3 * LANES), lambda i: (i % per_seq, 0)),
            pl.BlockSpec((tm, 3 * LANES), lambda i: (i % per_seq, 0)),
        ],
        out_specs=pl.BlockSpec((tm, Z_WIDTH), lambda i: (i, 0)),
        out_shape=jax.ShapeDtypeStruct((n, Z_WIDTH), F32),
        compiler_params=_cparams(("parallel",)),
        name="proj_in",
    )(x2, ln, w_p, ones256, qg, kg, rope_a, rope_c)


def _attn_body(seq, q_ref, kp_ref, kc_ref, kn_ref, vp_ref, vc_ref, vn_ref, o_ref,
               kbuf, vbuf, m_ref, l_ref, acc_ref, band_ref):
    t = ATT_TILE
    halo = A_MAX_REACH
    sub = ATT_SUB
    tile_start = pl.program_id(2) * t
    kbuf[0:halo, :] = kp_ref[...]
    kbuf[halo:halo + t, :] = kc_ref[...]
    kbuf[halo + t:, :] = kn_ref[...]
    vbuf[0:halo, :] = vp_ref[...]
    vbuf[halo:halo + t, :] = vc_ref[...]
    vbuf[halo + t:, :] = vn_ref[...]

    low_half = _iota((sub, LANES), 1) < HEAD_DIM
    a_idx = _iota((2 * sub, ATT_KEYS), 0) & (sub - 1)
    rel = _iota((2 * sub, ATT_KEYS), 1) - a_idx
    band_ref[...] = jnp.where((rel >= 0) & (rel <= 2 * A_RADIUS), 0.0, NEG)
    c_row = _iota((1, ATT_KEYS), 1)
    ones_v = jnp.ones((ATT_KEYS, LANES), BF16)

    def scores_stage(tile):
        pat, qs, dil = tile
        span = A_RADIUS * dil
        q_sub = q_ref[pl.ds(qs, sub, stride=dil), :]
        q_st = jnp.concatenate([jnp.where(low_half, q_sub, 0.0),
                                jnp.where(low_half, 0.0, q_sub)], axis=0).astype(BF16)
        k_sub = kbuf[pl.ds(halo + qs - span, ATT_KEYS, stride=dil), :].astype(BF16)
        s = _dot_nt(q_st, k_sub) + band_ref[...]
        if qs - span < 0 or qs - span + dil * (ATT_KEYS - 1) >= t:
            key_pos = (tile_start + qs - span) + dil * c_row
            s = s + jnp.where((key_pos >= 0) & (key_pos < seq), 0.0, NEG)
        return s, jnp.broadcast_to(jnp.max(s, axis=1, keepdims=True), (2 * sub, LANES))

    def values_stage(tile, s, m_row):
        pat, qs, dil = tile
        p = jnp.exp2(s - jnp.concatenate([m_row] * (ATT_KEYS // LANES), axis=1)).astype(BF16)
        v_sub = vbuf[pl.ds(halo + qs - A_RADIUS * dil, ATT_KEYS, stride=dil), :].astype(BF16)
        return _dot(p, jnp.concatenate([v_sub, ones_v], axis=1))

    def store_stage(tile, m_row, pvl):
        pat, qs, dil = tile
        rw = pl.ds(qs, sub, stride=dil)
        m_ref[pat, rw, :] = jnp.where(low_half, m_row[0:sub], m_row[sub:])
        l_ref[pat, rw, :] = jnp.where(low_half, pvl[0:sub, LANES:], pvl[sub:, LANES:])
        acc_ref[pat, rw, :] = jnp.where(low_half, pvl[0:sub, 0:LANES], pvl[sub:, 0:LANES])

    tiles = [(pat, r + j * sub * dil, dil)
             for pat, (_, dil) in enumerate(A_PATTERNS)
             for r in range(dil) for j in range(t // (sub * dil))]
    scored, valued = {}, {}
    for step in range(len(tiles) + 2 * ATT_SKEW):
        i_store, i_val = step - 2 * ATT_SKEW, step - ATT_SKEW
        if 0 <= i_store < len(tiles):
            store_stage(tiles[i_store], *valued.pop(i_store))
        if 0 <= i_val < len(tiles):
            s, m_row = scored.pop(i_val)
            valued[i_val] = (m_row, values_stage(tiles[i_val], s, m_row))
        if step < len(tiles):
            scored[step] = scores_stage(tiles[step])

    npat = len(A_PATTERNS)
    blk = 2 * sub
    for i in range(t // blk):
        rs = pl.ds(i * blk, blk)
        m_g = [m_ref[g, rs, :] for g in range(npat)]
        m_all = functools.reduce(jnp.maximum, m_g)
        w_g = [jnp.exp2(m - m_all) for m in m_g]
        l_all = functools.reduce(lambda a, b: a + b, [w * l_ref[g, rs, :] for g, w in enumerate(w_g)])
        num = functools.reduce(lambda a, b: a + b, [w * acc_ref[g, rs, :] for g, w in enumerate(w_g)])
        o_ref[rs, :] = (num / l_all).astype(o_ref.dtype)


def _attn(z3):
    b, seq, _ = z3.shape
    t = ATT_TILE
    halo = A_MAX_REACH
    nt = seq // t
    per_tile = t // halo
    n_halo = seq // halo
    qc, kc, vc = Z_AQ // LANES, Z_AK // LANES, Z_AV // LANES

    def cur(c0):
        return pl.BlockSpec((None, t, LANES), lambda bi, hp, i: (bi, i, c0 + hp))

    def prev(c0):
        return pl.BlockSpec((None, halo, LANES),
                            lambda bi, hp, i: (bi, jnp.maximum(i * per_tile - 1, 0), c0 + hp))

    def nxt(c0):
        return pl.BlockSpec((None, halo, LANES),
                            lambda bi, hp, i: (bi, jnp.minimum((i + 1) * per_tile, n_halo - 1), c0 + hp))

    return pl.pallas_call(
        functools.partial(_attn_body, seq),
        grid=(b, A_HEADS // 2, nt),
        in_specs=[cur(qc), prev(kc), cur(kc), nxt(kc), prev(vc), cur(vc), nxt(vc)],
        out_specs=pl.BlockSpec((None, t, LANES), lambda bi, hp, i: (bi, i, hp)),
        out_shape=jax.ShapeDtypeStruct((b, seq, A_WIDTH), BF16),
        scratch_shapes=[
            pltpu.VMEM((t + 2 * halo, LANES), F32),
            pltpu.VMEM((t + 2 * halo, LANES), F32),
            pltpu.VMEM((len(A_PATTERNS), t, LANES), F32),
            pltpu.VMEM((len(A_PATTERNS), t, LANES), F32),
            pltpu.VMEM((len(A_PATTERNS), t, LANES), F32),
            pltpu.VMEM((2 * ATT_SUB, ATT_KEYS), F32),
        ],
        compiler_params=_cparams(("parallel", "parallel", "parallel")),
        name="attn",
    )(z3, z3, z3, z3, z3, z3, z3)


def _split3(x):
    hi = x.astype(BF16)
    r1 = x - hi.astype(F32)
    mid = r1.astype(BF16)
    lo = (r1 - mid.astype(F32)).astype(BF16)
    return hi, mid, lo


def _gla_log_decay(g, gup_ref, gbias_ref):
    logits = _dot(g.astype(BF16), gup_ref[...]) + gbias_ref[...]
    return _log_sigmoid(logits) * (1.0 / B_GATE_TAU)


def _cumsum_rows(la, tri3_ref):
    hi, mid, lo = _split3(la)
    return _dot(tri3_ref[...], jnp.concatenate([hi, mid, lo], axis=0))


def _gla_bwd_body(nit, k_ref, v_ref, g_ref, gup_ref, gbias_ref, tri3_ref, emask_ref, sb_ref, state):
    @pl.when(pl.program_id(1) == 0)
    def _():
        state[...] = jnp.zeros_like(state)

    c = GLA_CHUNK
    grp = GLA_GROUP
    span = grp * c

    def step(i, carry):
        it = nit - 1 - i
        rows = pl.ds(pl.multiple_of(it * span, span), span)
        logits = _dot(g_ref[rows, :].astype(BF16), gup_ref[:, B_QK:]) + gbias_ref[:, B_QK:]
        la = _log_sigmoid(logits) * (1.0 / B_GATE_TAU)
        cums = [_cumsum_rows(la[c * j:c * (j + 1)], tri3_ref) for j in range(grp)]
        cbx = jnp.concatenate(cums, axis=0) - la
        kt = (k_ref[rows, :] * jnp.exp(cbx)).astype(BF16)
        vb = v_ref[rows, :].astype(BF16)
        upd = [_dot_tn(vb[c * j:c * (j + 1)], kt[c * j:c * (j + 1)]) for j in range(grp)]
        st = state[...]
        for j in reversed(range(grp)):
            sb_ref[it * grp + j] = functools.reduce(
                lambda a, b: a + b, [st[HEAD_DIM * h:HEAD_DIM * (h + 1)] for h in range(B_HEADS)]).astype(BF16)
            st = st * jnp.exp(cums[j][c - 1:c, :]) + upd[j] * emask_ref[...]
        state[...] = st
        return carry

    lax.fori_loop(0, nit, step, 0)


def _gla_main_body(nit, q_ref, k_ref, v_ref, r_ref, g_ref, sb_ref, gup_ref, gbias_ref, tri3_ref,
                   emask_ref, estack_ref, ones_ref, gain_ref, o_ref, state):
    @pl.when(pl.program_id(1) == 0)
    def _():
        state[...] = jnp.zeros_like(state)

    c = GLA_CHUNK
    n = GLA_SUB
    nsub = c // n
    grp = GLA_GROUP
    span = grp * c
    row_8 = _iota((8, B_QK), 0)
    rho = _iota((B_HEADS * n, B_QK), 0)
    head_k_ok = (rho >> 4) == (_iota((B_HEADS * n, B_QK), 1) >> 5)
    rho_v = _iota((B_HEADS * n, B_WIDTH), 0)
    head_v_ok = (rho_v >> 4) == (_iota((B_HEADS * n, B_WIDTH), 1) >> 6)
    own_blk = ((_iota((span, B_HEADS * c), 0) & (c - 1)) >> 4) == (_iota((span, B_HEADS * c), 1) >> 6)
    zeros_k = jnp.zeros((B_HEADS * n, B_QK), F32)

    def rows_of(vals, height):
        return jnp.concatenate([jnp.broadcast_to(x, (height, x.shape[1])) for x in vals], axis=0)

    def by_head(x, ok):
        return jnp.where(ok, jnp.concatenate([x] * B_HEADS, axis=0), 0.0)

    def diag_lhs(q, k, bf2, cb2):
        nblk = span // n
        slabs = []
        for s in range(n):
            ks = rows_of([k[n * i + s:n * i + s + 1] for i in range(nblk)], n)
            pieces = []
            for i in range(nblk):
                ref_f = bf2[n * i + s:n * i + s + 1]
                ref_b = cb2[n * i + s:n * i + s + 1]
                for r0 in range(0, n, 8):
                    rs = slice(n * i + r0, n * i + r0 + 8)
                    if s <= r0:
                        pieces.append(bf2[rs] - ref_f)
                    elif s >= r0 + 8:
                        pieces.append(ref_b - cb2[rs])
                    else:
                        pieces.append(jnp.where(row_8 >= s - r0, bf2[rs] - ref_f, ref_b - cb2[rs]))
            arg = jnp.concatenate(pieces, axis=0)
            slabs.append(((q * ks) * jnp.exp2(arg)).astype(BF16))
        return jnp.concatenate(slabs, axis=1)

    def chunk_local(q, k, v, bf, cbx, kf, kb, att_diag):
        e_rows = [bf[n * j + n - 1:n * j + n, :] for j in range(nsub)]
        f_rows = [cbx[n * j:n * j + 1, :] for j in range(nsub)]
        lhs_parts = []
        for j in range(nsub - 1):
            r0 = n * (j + 1)
            part = q[r0:] * jnp.exp(bf[r0:] - e_rows[j])
            lhs_parts.append(jnp.concatenate([jnp.zeros((r0, B_QK), F32), part], axis=0))
        for j in range(1, nsub):
            r1 = n * j
            part = q[:r1] * jnp.exp(f_rows[j] - cbx[:r1])
            lhs_parts.append(jnp.concatenate([part, jnp.zeros((c - r1, B_QK), F32)], axis=0))
        rhs_rows = []
        for j in range(nsub):
            kfj = by_head(kf[n * j:n * (j + 1)], head_k_ok)
            kbj = by_head(kb[n * j:n * (j + 1)], head_k_ok)
            parts = [kfj if (jj == j and j < nsub - 1) else zeros_k for jj in range(nsub - 1)]
            parts += [kbj if (jj == j and j > 0) else zeros_k for jj in range(1, nsub)]
            rhs_rows.append(jnp.concatenate(parts, axis=1))
        att = _dot_nt(jnp.concatenate(lhs_parts, axis=1).astype(BF16),
                      jnp.concatenate(rhs_rows, axis=0).astype(BF16))
        v4 = jnp.concatenate([by_head(v[n * j:n * (j + 1)], head_v_ok) for j in range(nsub)],
                             axis=0).astype(BF16)
        return _dot((att + att_diag).astype(BF16), v4)

    def step(it, carry):
        rows = pl.ds(pl.multiple_of(it * span, span), span)
        q = q_ref[rows, :]
        k = k_ref[rows, :]
        v = v_ref[rows, :]
        la = _gla_log_decay(g_ref[rows, :], gup_ref, gbias_ref)
        cums = [_cumsum_rows(la[c * j:c * (j + 1)], tri3_ref) for j in range(grp)]
        cum = jnp.concatenate(cums, axis=0)
        bf = cum[:, :B_QK]
        cbx = cum[:, B_QK:] - la[:, B_QK:]
        tot_f = [cj[c - 1:c, :B_QK] for cj in cums]
        tot_b = [cj[c - 1:c, B_QK:] for cj in cums]
        bf2 = bf * LOG2E
        cb2 = cbx * LOG2E
        e_blk = rows_of([bf[n * j + n - 1:n * j + n, :] for j in range(span // n)], n)
        f_blk = rows_of([cbx[n * j:n * j + 1, :] for j in range(span // n)], n)
        kf = k * jnp.exp(e_blk - bf)
        kb = k * jnp.exp(cbx - f_blk)
        kt = (k * jnp.exp(rows_of(tot_f, c) - bf)).astype(BF16)
        lhs_inter = jnp.concatenate([q * jnp.exp(bf), q * jnp.exp(rows_of(tot_b, c) - cbx)],
                                    axis=1).astype(BF16)
        vb = v.astype(BF16)
        att_diag = jnp.where(own_blk, _dot(diag_lhs(q, k, bf2, cb2), estack_ref[...]), 0.0)

        outs = []
        upd = []
        for j in range(grp):
            sl = slice(c * j, c * (j + 1))
            outs.append(chunk_local(q[sl], k[sl], v[sl], bf[sl], cbx[sl], kf[sl], kb[sl], att_diag[sl]))
            upd.append(_dot_tn(vb[sl], kt[sl]))

        st = state[...]
        for j in range(grp):
            sl = slice(c * j, c * (j + 1))
            sb_full = jnp.where(emask_ref[...] > 0.0,
                                jnp.concatenate([sb_ref[it * grp + j]] * B_HEADS, axis=0), 0.0)
            rhs = jnp.concatenate([st.astype(BF16), sb_full.astype(BF16)], axis=1)
            outs[j] = outs[j] + _dot_nt(lhs_inter[sl], rhs)
            st = st * jnp.exp(tot_f[j]) + upd[j] * emask_ref[...]
        state[...] = st

        gate = r_ref[rows, :]
        o = jnp.concatenate(outs, axis=0)
        out = _head_norm(o, ones_ref, gain_ref[...]) * (gate * _sigmoid(gate))
        o_ref[rows, :] = out.astype(o_ref.dtype)
        return carry

    lax.fori_loop(0, nit, step, 0)


def _gla(z3, gup_bd, gbias, tri3, emask, estack, ones256, gain, tb):
    b, seq, _ = z3.shape
    nblk = seq // tb
    nch = tb // GLA_CHUNK
    ntot = seq // GLA_CHUNK

    def zspec(col, width, rev):
        blk = col // width
        if rev:
            return pl.BlockSpec((None, tb, width), lambda bi, i: (bi, nblk - 1 - i, blk))
        return pl.BlockSpec((None, tb, width), lambda bi, i: (bi, i, blk))

    consts = [gup_bd, gbias, tri3, emask]
    const_specs = [_const_spec(a.shape) for a in consts]
    nit = nch // GLA_GROUP
    sb = pl.pallas_call(
        functools.partial(_gla_bwd_body, nit),
        grid=(b, nblk),
        in_specs=[zspec(Z_BK, B_QK, True), zspec(Z_BV, B_WIDTH, True), zspec(Z_BG, LANES, True)]
        + const_specs,
        out_specs=pl.BlockSpec((None, nch, HEAD_DIM, B_QK), lambda bi, i: (bi, nblk - 1 - i, 0, 0)),
        out_shape=jax.ShapeDtypeStruct((b, ntot, HEAD_DIM, B_QK), BF16),
        scratch_shapes=[pltpu.VMEM((B_WIDTH, B_QK), F32)],
        compiler_params=_cparams(("parallel", "arbitrary")),
        name="gla_bwd",
    )(z3, z3, z3, *consts)

    consts2 = [gup_bd, gbias, tri3, emask, estack, ones256, gain]
    return pl.pallas_call(
        functools.partial(_gla_main_body, nit),
        grid=(b, nblk),
        in_specs=[zspec(Z_BQ, B_QK, False), zspec(Z_BK, B_QK, False), zspec(Z_BV, B_WIDTH, False),
                  zspec(Z_BR, B_WIDTH, False), zspec(Z_BG, LANES, False),
                  pl.BlockSpec((None, nch, HEAD_DIM, B_QK), lambda bi, i: (bi, i, 0, 0))]
        + [_const_spec(a.shape) for a in consts2],
        out_specs=pl.BlockSpec((None, tb, B_WIDTH), lambda bi, i: (bi, i, 0)),
        out_shape=jax.ShapeDtypeStruct((b, seq, B_WIDTH), BF16),
        scratch_shapes=[pltpu.VMEM((B_WIDTH, B_QK), F32)],
        compiler_params=_cparams(("parallel", "arbitrary")),
        name="gla_main",
    )(z3, z3, z3, z3, z3, sb, *consts2)


def _ret_bwd_body(nch, k_ref, v_ref, raw_ref, bmask_ref, rb_ref, state):
    @pl.when(pl.program_id(1) == 0)
    def _():
        state[...] = jnp.zeros_like(state)

    c = RET_CHUNK
    lg1 = _log_sigmoid(raw_ref[1:2, :])
    pos = _iota((c, C_WIDTH), 0).astype(F32)
    kdec = jnp.exp(pos * lg1)
    chunk_decay = jnp.exp(float(c) * lg1)

    upd = []
    for j in range(nch):
        rows = pl.ds(c * j, c)
        kt = (k_ref[rows, :] * kdec).astype(BF16)
        upd.append(_dot_tn(kt, v_ref[rows, :].astype(BF16)))
    st = state[...]
    half = C_WIDTH // 2
    for j in reversed(range(nch)):
        rb_ref[j] = jnp.concatenate([st[:half, :half], st[half:, half:]], axis=0).astype(BF16)
        st = chunk_decay * st + upd[j] * bmask_ref[...]
    state[...] = st


def _ret_main_body(nch, q_ref, k_ref, v_ref, g_ref, rb_ref, raw_ref, raw_s_ref, bmask_ref, ones_ref,
                   gain_ref, o_ref, state):
    @pl.when(pl.program_id(1) == 0)
    def _():
        state[...] = jnp.zeros_like(state)

    c = RET_CHUNK
    lg = _log_sigmoid(raw_ref[...])
    lg0, lg1 = lg[0:1, :], lg[1:2, :]
    lgs = _log_sigmoid(raw_s_ref[...])
    pos = _iota((c, C_WIDTH), 0).astype(F32)
    qdec_f = jnp.exp((pos + 1.0) * lg0)
    qdec_b = jnp.exp((float(c) - pos) * lg1)
    kdec_f = jnp.exp((float(c) - 1.0 - pos) * lg0)
    chunk_decay = jnp.exp(float(c) * lg0)
    t_idx = _iota((c, C_HEADS * c), 0)
    s_idx = _iota((c, C_HEADS * c), 1) & (c - 1)
    rel = (t_idx - s_idx).astype(F32)
    dmat = jnp.where(rel >= 0.0, jnp.exp(jnp.maximum(rel, 0.0) * lgs[0:1, :]),
                     jnp.exp(jnp.maximum(-rel, 0.0) * lgs[1:2, :]))
    rho = _iota((C_HEADS * c, C_WIDTH), 0)
    lane = _iota((C_HEADS * c, C_WIDTH), 1)
    head_ok = (rho >> 7) == (lane >> 6)

    outs, lhs_inter, upd = [], [], []
    for j in range(nch):
        rows = pl.ds(c * j, c)
        q = q_ref[rows, :]
        k = k_ref[rows, :]
        v = v_ref[rows, :]
        k4 = jnp.where(head_ok, jnp.concatenate([k] * C_HEADS, axis=0), 0.0).astype(BF16)
        v4 = jnp.where(head_ok, jnp.concatenate([v] * C_HEADS, axis=0), 0.0).astype(BF16)
        scores = _dot_nt(q.astype(BF16), k4) * dmat
        outs.append(_dot(scores.astype(BF16), v4))
        lhs_inter.append(jnp.concatenate([q * qdec_f, q * qdec_b], axis=1).astype(BF16))
        upd.append(_dot_tn((k * kdec_f).astype(BF16), v.astype(BF16)))
    st = state[...]
    half = C_WIDTH // 2
    zero_q = jnp.zeros((half, half), BF16)
    for j in range(nch):
        rbj = rb_ref[j]
        rb_full = jnp.concatenate([jnp.concatenate([rbj[:half], zero_q], axis=1),
                                   jnp.concatenate([zero_q, rbj[half:]], axis=1)], axis=0)
        rhs = jnp.concatenate([st.astype(BF16), rb_full], axis=0)
        outs[j] = outs[j] + _dot(lhs_inter[j], rhs)
        st = chunk_decay * st + upd[j] * bmask_ref[...]
    state[...] = st
    for j in range(nch):
        rows = pl.ds(c * j, c)
        gate = g_ref[rows, :]
        out = _head_norm(outs[j], ones_ref, gain_ref[...]) * (gate * _sigmoid(gate))
        o_ref[rows, :] = out.astype(o_ref.dtype)


def _ret(z3, raw256, raw512, ones256, gain, tb):
    b, seq, _ = z3.shape
    nblk = seq // tb
    nch = tb // RET_CHUNK
    ntot = seq // RET_CHUNK

    def zspec(col, rev):
        blk = col // C_WIDTH
        if rev:
            return pl.BlockSpec((None, tb, C_WIDTH), lambda bi, i: (bi, nblk - 1 - i, blk))
        return pl.BlockSpec((None, tb, C_WIDTH), lambda bi, i: (bi, i, blk))

    rb = pl.pallas_call(
        functools.partial(_ret_bwd_body, nch),
        grid=(b, nblk),
        in_specs=[zspec(Z_CK, True), zspec(Z_CV, True), _const_spec(raw256.shape),
                  _const_spec(ones256.shape)],
        out_specs=pl.BlockSpec((None, nch, C_WIDTH, C_WIDTH // 2), lambda bi, i: (bi, nblk - 1 - i, 0, 0)),
        out_shape=jax.ShapeDtypeStruct((b, ntot, C_WIDTH, C_WIDTH // 2), BF16),
        scratch_shapes=[pltpu.VMEM((C_WIDTH, C_WIDTH), F32)],
        compiler_params=_cparams(("parallel", "arbitrary")),
        name="ret_bwd",
    )(z3, z3, raw256, ones256)

    consts = [raw256, raw512, ones256, ones256, gain]
    return pl.pallas_call(
        functools.partial(_ret_main_body, nch),
        grid=(b, nblk),
        in_specs=[zspec(Z_CQ, False), zspec(Z_CK, False), zspec(Z_CV, False), zspec(Z_CG, False),
                  pl.BlockSpec((None, nch, C_WIDTH, C_WIDTH // 2), lambda bi, i: (bi, i, 0, 0))]
        + [_const_spec(a.shape) for a in consts],
        out_specs=pl.BlockSpec((None, tb, C_WIDTH), lambda bi, i: (bi, i, 0)),
        out_shape=jax.ShapeDtypeStruct((b, seq, C_WIDTH), BF16),
        scratch_shapes=[pltpu.VMEM((C_WIDTH, C_WIDTH), F32)],
        compiler_params=_cparams(("parallel", "arbitrary")),
        name="ret_main",
    )(z3, z3, z3, z3, rb, *consts)


def _post_body(h_ref, oa_ref, ob_ref, oc_ref, p_ref, wo_ref, lm_ref, w1_ref, w2_ref, lp_ref,
               wg_ref, wp_ref, y_ref):
    def rms(x, g_ref):
        ms = jnp.mean(x * x, axis=-1, keepdims=True)
        return (x * lax.rsqrt(ms + EPS) * g_ref[...]).astype(BF16)

    h = h_ref[...]
    h = h + (_dot(oa_ref[...], wo_ref[0:A_WIDTH, :])
             + _dot(ob_ref[...], wo_ref[A_WIDTH:A_WIDTH + B_WIDTH, :])
             + _dot(oc_ref[...], wo_ref[A_WIDTH + B_WIDTH:, :]))
    m = rms(h, lm_ref)
    ff = D_FF // 4
    mlp = None
    for j in range(4):
        hid = _dot(m, w1_ref[:, ff * j:ff * (j + 1)])
        hid = jnp.square(jnp.maximum(hid, 0.0)).astype(BF16)
        part = _dot(hid, w2_ref[ff * j:ff * (j + 1), :])
        mlp = part if mlp is None else mlp + part
    h = h + mlp
    gate = _sigmoid(_dot(rms(h, lp_ref), wg_ref[...]))
    y_ref[...] = h + gate * _dot(p_ref[...].astype(BF16), wp_ref[...])


def _post(h2, oa, ob, oc, ple, wo, lm, w1, w2, lp, wg, wp, tm):
    n = h2.shape[0]

    def tok(width):
        return pl.BlockSpec((tm, width), lambda i: (i, 0))

    consts = [wo, lm, w1, w2, lp, wg, wp]
    return pl.pallas_call(
        _post_body,
        grid=(n // tm,),
        in_specs=[tok(D_MODEL), tok(A_WIDTH), tok(B_WIDTH), tok(C_WIDTH), tok(PLE_DIM)]
        + [_const_spec(a.shape) for a in consts],
        out_specs=tok(D_MODEL),
        out_shape=jax.ShapeDtypeStruct((n, D_MODEL), F32),
        compiler_params=_cparams(("parallel",)),
        name="post",
    )(h2, oa, ob, oc, ple, *consts)


def _rope_tables(seq, rot_dim, theta):
    half = rot_dim // 2
    inv_freq = 1.0 / (theta ** (jnp.arange(half, dtype=F32) * (2.0 / rot_dim)))
    hi = (jnp.arange(seq // LANES, dtype=F32) * LANES)[:, None] * inv_freq[None, :]
    lo = jnp.arange(LANES, dtype=F32)[:, None] * inv_freq[None, :]
    ch, sh = jnp.cos(hi)[:, None, :], jnp.sin(hi)[:, None, :]
    cl, sl = jnp.cos(lo)[None, :, :], jnp.sin(lo)[None, :, :]
    cos = (ch * cl - sh * sl).reshape(seq, half)
    sin = (sh * cl + ch * sl).reshape(seq, half)
    pad = HEAD_DIM - rot_dim
    c = jnp.concatenate([cos, cos, jnp.ones((seq, pad), F32)], axis=1)
    sn = jnp.concatenate([-sin, jnp.zeros((seq, half + pad), F32)], axis=1)
    sp = jnp.concatenate([jnp.zeros((seq, half), F32), sin, jnp.zeros((seq, pad), F32)], axis=1)
    return jnp.concatenate([jnp.tile(t, (1, LANES // HEAD_DIM)) for t in (c, sn, sp)], axis=1)


def _block_ones(rows, cols, rblk, cblk):
    r = jnp.arange(rows)[:, None] // rblk
    c = jnp.arange(cols)[None, :] // cblk
    return r == c


def _layer_consts(i, ln_mix, w_in, attn_q_norm, attn_k_norm, gla_gate_up, gla_gate_bias, gla_out_norm,
                  ret_decay_raw, ret_out_norm, w_out, ln_mlp, w_mlp_in, w_mlp_out, ln_pe, w_pe_gate,
                  w_pe_proj):
    w = w_in[i]
    w_p = jnp.concatenate([w[:, :2304], w[:, 2336:N_IN], w[:, 2304:2336],
                           jnp.zeros((D_MODEL, Z_WIDTH - N_IN), F32)], axis=1).astype(BF16)
    gup = gla_gate_up[i].astype(BF16)
    gup_bd = jnp.zeros((LANES, 2 * B_QK), BF16)
    gup_bd = gup_bd.at[0:B_GATE_RANK, 0:B_QK].set(gup[0])
    gup_bd = gup_bd.at[B_GATE_RANK:2 * B_GATE_RANK, B_QK:].set(gup[1])
    return dict(
        ln_mix=ln_mix[i][None, :], w_p=w_p,
        qg=jnp.tile(attn_q_norm[i], 4)[None, :], kg=jnp.tile(attn_k_norm[i], 4)[None, :],
        gup_bd=gup_bd, gbias=gla_gate_bias[i].reshape(1, 2 * B_QK),
        gla_gain=gla_out_norm[i][None, :],
        raw256=jnp.repeat(ret_decay_raw[i], C_DIM, axis=1),
        raw512=jnp.repeat(ret_decay_raw[i], RET_CHUNK, axis=1),
        ret_gain=ret_out_norm[i][None, :],
        wo=w_out[i].astype(BF16), lm=ln_mlp[i][None, :], w1=w_mlp_in[i].astype(BF16),
        w2=w_mlp_out[i].astype(BF16), lp=ln_pe[i][None, :], wg=w_pe_gate[i].astype(BF16),
        wp=w_pe_proj[i].astype(BF16),
    )


def _shared_consts():
    tri = jnp.arange(GLA_CHUNK)[:, None] >= jnp.arange(GLA_CHUNK)[None, :]
    return dict(
        ones256=_block_ones(256, 256, HEAD_DIM, HEAD_DIM).astype(BF16),
        tri3=jnp.tile(tri, (1, 3)).astype(BF16),
        emask=_block_ones(B_WIDTH, B_QK, HEAD_DIM, B_KDIM).astype(F32),
        estack=_gla_diag_selector().astype(BF16),
    )


def _gla_diag_selector():
    r = jnp.arange(GLA_SUB * B_QK)
    c = jnp.arange(B_HEADS * GLA_CHUNK)
    same_s = (r // B_QK)[:, None] == (c % GLA_SUB)[None, :]
    same_head = ((r % B_QK) // B_KDIM)[:, None] == ((c % (B_HEADS * GLA_SUB)) // GLA_SUB)[None, :]
    return same_s & same_head


def _run_trunk(x, p, layers, shared):
    b, seq, _ = x.shape
    n = b * seq
    tm = 512
    tb_gla, tb_ret = 2048, 2048
    rope_a = _rope_tables(seq, ROPE_DIM, ROPE_THETA)
    rope_c = _rope_tables(seq, C_DIM, RET_THETA)
    h = x.reshape(n, D_MODEL)
    for i, lc in enumerate(layers):
        z = _proj_in(h, seq, lc["ln_mix"], lc["w_p"], shared["ones256"], lc["qg"], lc["kg"],
                     rope_a, rope_c, tm)
        z3 = z.reshape(b, seq, Z_WIDTH)
        oa = _attn(z3)
        ob = _gla(z3, lc["gup_bd"], lc["gbias"], shared["tri3"], shared["emask"], shared["estack"],
                  shared["ones256"], lc["gla_gain"], tb_gla)
        oc = _ret(z3, lc["raw256"], lc["raw512"], shared["ones256"], lc["ret_gain"], tb_ret)
        h = _post(h, oa.reshape(n, A_WIDTH), ob.reshape(n, B_WIDTH), oc.reshape(n, C_WIDTH),
                  p[i].reshape(n, PLE_DIM), lc["wo"], lc["lm"], lc["w1"], lc["w2"], lc["lp"],
                  lc["wg"], lc["wp"], tm)
    return h.reshape(b, seq, D_MODEL)


def kernel(x_prompt, x_sample, p_prompt, p_sample, ln_mix, w_in, attn_q_norm, attn_k_norm, gla_gate_up, gla_gate_bias, gla_out_norm, ret_decay_raw, ret_out_norm, w_out, ln_mlp, w_mlp_in, w_mlp_out, ln_pe, w_pe_gate, w_pe_proj):
    depth = w_in.shape[0]
    layers = [_layer_consts(i, ln_mix, w_in, attn_q_norm, attn_k_norm, gla_gate_up, gla_gate_bias,
                            gla_out_norm, ret_decay_raw, ret_out_norm, w_out, ln_mlp, w_mlp_in,
                            w_mlp_out, ln_pe, w_pe_gate, w_pe_proj) for i in range(depth)]
    shared = _shared_consts()
    y_prompt = _run_trunk(x_prompt, p_prompt, layers, shared)
    y_sample = _run_trunk(x_sample, p_sample, layers, shared)
    return (y_prompt, y_sample)
```

```python
import functools

import jax
import jax.numpy as jnp
from jax import lax
from jax.experimental import pallas as pl
from jax.experimental.pallas import tpu as pltpu

F32 = jnp.float32
BF16 = jnp.bfloat16

D_MODEL = 1024
HEAD_DIM = 64
A_HEADS = 8
A_WIDTH = 512
A_PATTERNS = ((128, 1), (512, 4), (2048, 16))
A_RADIUS = 64
A_MAX_REACH = 1024
ROPE_THETA = 500000.0
ROPE_DIM = 16
B_HEADS = 4
B_KDIM = 32
B_QK = 128
B_WIDTH = 256
B_GATE_RANK = 16
B_GATE_TAU = 16.0
C_HEADS = 4
C_DIM = 64
C_WIDTH = 256
RET_THETA = 10000.0
N_IN = 3360
D_FF = 4096
PLE_DIM = 256
EPS = 1e-6
NEG = -1e30
LOG2E = 1.4426950408889634

LANES = 128
VMEM_LIMIT = 56 * 1024 * 1024

Z_AQ, Z_AK, Z_AV = 0, 512, 1024
Z_BQ, Z_BK, Z_BV, Z_BR = 1536, 1664, 1792, 2048
Z_CQ, Z_CK, Z_CV, Z_CG = 2304, 2560, 2816, 3072
Z_BG = 3328
Z_WIDTH = 3456

GLA_CHUNK = 64
GLA_SUB = 16
GLA_GROUP = 8
RET_CHUNK = 128
ATT_TILE = 2048
ATT_SUB = 128
ATT_KEYS = ATT_SUB + 2 * A_RADIUS
ATT_SKEW = 2


def _cparams(sem):
    return pltpu.CompilerParams(dimension_semantics=sem, vmem_limit_bytes=VMEM_LIMIT)


def _const_spec(shape):
    nd = len(shape)
    return pl.BlockSpec(shape, lambda *_: (0,) * nd, pipeline_mode=pl.Buffered(1))


def _sigmoid(x):
    return 1.0 / (1.0 + jnp.exp(-x))


def _log_sigmoid(x):
    return jnp.minimum(x, 0.0) - jnp.log1p(jnp.exp(-jnp.abs(x)))


def _iota(shape, dim):
    return lax.broadcasted_iota(jnp.int32, shape, dim)


def _dot(a, b):
    return jnp.dot(a, b, preferred_element_type=F32)


def _dot_nt(a, b):
    return lax.dot_general(a, b, (((1,), (1,)), ((), ())), preferred_element_type=F32)


def _dot_tn(a, b):
    return lax.dot_general(a, b, (((0,), (0,)), ((), ())), preferred_element_type=F32)


def _head_norm(o, ones_ref, gain):
    ssum = _dot((o * o).astype(BF16), ones_ref[...])
    return o * lax.rsqrt(ssum * (1.0 / HEAD_DIM) + EPS) * gain


def _rope(y, tab_ref, shift):
    c = tab_ref[:, 0:LANES]
    sn = tab_ref[:, LANES:2 * LANES]
    sp = tab_ref[:, 2 * LANES:3 * LANES]
    outs = []
    for j in range(y.shape[1] // LANES):
        yj = y[:, j * LANES:(j + 1) * LANES]
        outs.append(yj * c + pltpu.roll(yj, LANES - shift, 1) * sn + pltpu.roll(yj, shift, 1) * sp)
    return jnp.concatenate(outs, axis=1)


def _proj_in_body(x_ref, g_ref, w_ref, ones_ref, qg_ref, kg_ref, ra_ref, rc_ref, z_ref):
    x = x_ref[...]
    ms = jnp.mean(x * x, axis=-1, keepdims=True)
    u = (x * lax.rsqrt(ms + EPS) * g_ref[...]).astype(BF16)

    def proj(a, b):
        return _dot(u, w_ref[:, a:b])

    def qk_norm(y, gain_ref):
        halves = []
        for j in range(2):
            yj = y[:, 256 * j:256 * (j + 1)]
            halves.append(_head_norm(yj, ones_ref, gain_ref[...]))
        return jnp.concatenate(halves, axis=1)

    aq = _rope(qk_norm(proj(Z_AQ, Z_AK), qg_ref), ra_ref, ROPE_DIM // 2)
    z_ref[:, Z_AQ:Z_AK] = aq * (HEAD_DIM ** -0.5 * LOG2E)
    z_ref[:, Z_AK:Z_AV] = _rope(qk_norm(proj(Z_AK, Z_AV), kg_ref), ra_ref, ROPE_DIM // 2)
    z_ref[:, Z_AV:Z_BQ] = proj(Z_AV, Z_BQ)
    z_ref[:, Z_BQ:Z_BK] = proj(Z_BQ, Z_BK) * (B_KDIM ** -0.5)
    z_ref[:, Z_BK:Z_CQ] = proj(Z_BK, Z_CQ)
    z_ref[:, Z_CQ:Z_CK] = _rope(proj(Z_CQ, Z_CK), rc_ref, C_DIM // 2)
    z_ref[:, Z_CK:Z_CV] = _rope(proj(Z_CK, Z_CV), rc_ref, C_DIM // 2) * (C_DIM ** -0.5)
    z_ref[:, Z_CV:Z_WIDTH] = proj(Z_CV, Z_WIDTH)


def _proj_in(x2, seq, ln, w_p, ones256, qg, kg, rope_a, rope_c, tm):
    n = x2.shape[0]
    per_seq = seq // tm
    return pl.pallas_call(
        _proj_in_body,
        grid=(n // tm,),
        in_specs=[
            pl.BlockSpec((tm, D_MODEL), lambda i: (i, 0)),
            _const_spec((1, D_MODEL)),
            _const_spec((D_MODEL, Z_WIDTH)),
            _const_spec((256, 256)),
            _const_spec((1, 256)),
            _const_spec((1, 256)),
            pl.BlockSpec((tm, 3 * LANES), lambda i: (i % per_seq, 0)),
            pl.BlockSpec((tm, 3 * LANES), lambda i: (i % per_seq, 0)),
        ],
        out_specs=pl.BlockSpec((tm, Z_WIDTH), lambda i: (i, 0)),
        out_shape=jax.ShapeDtypeStruct((n, Z_WIDTH), F32),
        compiler_params=_cparams(("parallel",)),
        name="proj_in",
    )(x2, ln, w_p, ones256, qg, kg, rope_a, rope_c)


def _attn_body(seq, q_ref, kp_ref, kc_ref, kn_ref, vp_ref, vc_ref, vn_ref, o_ref,
               kbuf, vbuf, m_ref, l_ref, acc_ref, band_ref):
    t = ATT_TILE
    halo = A_MAX_REACH
    sub = ATT_SUB
    tile_start = pl.program_id(2) * t
    kbuf[0:halo, :] = kp_ref[...]
    kbuf[halo:halo + t, :] = kc_ref[...]
    kbuf[halo + t:, :] = kn_ref[...]
    vbuf[0:halo, :] = vp_ref[...]
    vbuf[halo:halo + t, :] = vc_ref[...]
    vbuf[halo + t:, :] = vn_ref[...]

    low_half = _iota((sub, LANES), 1) < HEAD_DIM
    a_idx = _iota((2 * sub, ATT_KEYS), 0) & (sub - 1)
    rel = _iota((2 * sub, ATT_KEYS), 1) - a_idx
    band_ref[...] = jnp.where((rel >= 0) & (rel <= 2 * A_RADIUS), 0.0, NEG)
    c_row = _iota((1, ATT_KEYS), 1)
    ones_v = jnp.ones((ATT_KEYS, LANES), BF16)

    def scores_stage(tile):
        pat, qs, dil = tile
        span = A_RADIUS * dil
        q_sub = q_ref[pl.ds(qs, sub, stride=dil), :]
        q_st = jnp.concatenate([jnp.where(low_half, q_sub, 0.0),
                                jnp.where(low_half, 0.0, q_sub)], axis=0).astype(BF16)
        k_sub = kbuf[pl.ds(halo + qs - span, ATT_KEYS, stride=dil), :].astype(BF16)
        s = _dot_nt(q_st, k_sub) + band_ref[...]
        if qs - span < 0 or qs - span + dil * (ATT_KEYS - 1) >= t:
            key_pos = (tile_start + qs - span) + dil * c_row
            s = s + jnp.where((key_pos >= 0) & (key_pos < seq), 0.0, NEG)
        return s, jnp.broadcast_to(jnp.max(s, axis=1, keepdims=True), (2 * sub, LANES))

    def values_stage(tile, s, m_row):
        pat, qs, dil = tile
        p = jnp.exp2(s - jnp.concatenate([m_row] * (ATT_KEYS // LANES), axis=1)).astype(BF16)
        v_sub = vbuf[pl.ds(halo + qs - A_RADIUS * dil, ATT_KEYS, stride=dil), :].astype(BF16)
        return _dot(p, jnp.concatenate([v_sub, ones_v], axis=1))

    def store_stage(tile, m_row, pvl):
        pat, qs, dil = tile
        rw = pl.ds(qs, sub, stride=dil)
        m_ref[pat, rw, :] = jnp.where(low_half, m_row[0:sub], m_row[sub:])
        l_ref[pat, rw, :] = jnp.where(low_half, pvl[0:sub, LANES:], pvl[sub:, LANES:])
        acc_ref[pat, rw, :] = jnp.where(low_half, pvl[0:sub, 0:LANES], pvl[sub:, 0:LANES])

    tiles = [(pat, r + j * sub * dil, dil)
             for pat, (_, dil) in enumerate(A_PATTERNS)
             for r in range(dil) for j in range(t // (sub * dil))]
    scored, valued = {}, {}
    for step in range(len(tiles) + 2 * ATT_SKEW):
        i_store, i_val = step - 2 * ATT_SKEW, step - ATT_SKEW
        if 0 <= i_store < len(tiles):
            store_stage(tiles[i_store], *valued.pop(i_store))
        if 0 <= i_val < len(tiles):
            s, m_row = scored.pop(i_val)
            valued[i_val] = (m_row, values_stage(tiles[i_val], s, m_row))
        if step < len(tiles):
            scored[step] = scores_stage(tiles[step])

    npat = len(A_PATTERNS)
    blk = 2 * sub
    for i in range(t // blk):
        rs = pl.ds(i * blk, blk)
        m_g = [m_ref[g, rs, :] for g in range(npat)]
        m_all = functools.reduce(jnp.maximum, m_g)
        w_g = [jnp.exp2(m - m_all) for m in m_g]
        l_all = functools.reduce(lambda a, b: a + b, [w * l_ref[g, rs, :] for g, w in enumerate(w_g)])
        num = functools.reduce(lambda a, b: a + b, [w * acc_ref[g, rs, :] for g, w in enumerate(w_g)])
        o_ref[rs, :] = (num / l_all).astype(o_ref.dtype)


def _attn(z3):
    b, seq, _ = z3.shape
    t = ATT_TILE
    halo = A_MAX_REACH
    nt = seq // t
    per_tile = t // halo
    n_halo = seq // halo
    qc, kc, vc = Z_AQ // LANES, Z_AK // LANES, Z_AV // LANES

    def cur(c0):
        return pl.BlockSpec((None, t, LANES), lambda bi, hp, i: (bi, i, c0 + hp))

    def prev(c0):
        return pl.BlockSpec((None, halo, LANES),
                            lambda bi, hp, i: (bi, jnp.maximum(i * per_tile - 1, 0), c0 + hp))

    def nxt(c0):
        return pl.BlockSpec((None, halo, LANES),
                            lambda bi, hp, i: (bi, jnp.minimum((i + 1) * per_tile, n_halo - 1), c0 + hp))

    return pl.pallas_call(
        functools.partial(_attn_body, seq),
        grid=(b, A_HEADS // 2, nt),
        in_specs=[cur(qc), prev(kc), cur(kc), nxt(kc), prev(vc), cur(vc), nxt(vc)],
        out_specs=pl.BlockSpec((None, t, LANES), lambda bi, hp, i: (bi, i, hp)),
        out_shape=jax.ShapeDtypeStruct((b, seq, A_WIDTH), BF16),
        scratch_shapes=[
            pltpu.VMEM((t + 2 * halo, LANES), F32),
            pltpu.VMEM((t + 2 * halo, LANES), F32),
            pltpu.VMEM((len(A_PATTERNS), t, LANES), F32),
            pltpu.VMEM((len(A_PATTERNS), t, LANES), F32),
            pltpu.VMEM((len(A_PATTERNS), t, LANES), F32),
            pltpu.VMEM((2 * ATT_SUB, ATT_KEYS), F32),
        ],
        compiler_params=_cparams(("parallel", "parallel", "parallel")),
        name="attn",
    )(z3, z3, z3, z3, z3, z3, z3)


def _split3(x):
    hi = x.astype(BF16)
    r1 = x - hi.astype(F32)
    mid = r1.astype(BF16)
    lo = (r1 - mid.astype(F32)).astype(BF16)
    return hi, mid, lo


def _gla_log_decay(g, gup_ref, gbias_ref):
    logits = _dot(g.astype(BF16), gup_ref[...]) + gbias_ref[...]
    return _log_sigmoid(logits) * (1.0 / B_GATE_TAU)


def _cumsum_rows(la, tri3_ref):
    hi, mid, lo = _split3(la)
    return _dot(tri3_ref[...], jnp.concatenate([hi, mid, lo], axis=0))


def _gla_bwd_body(nit, k_ref, v_ref, g_ref, gup_ref, gbias_ref, tri3_ref, emask_ref, sb_ref, state):
    @pl.when(pl.program_id(1) == 0)
    def _():
        state[...] = jnp.zeros_like(state)

    c = GLA_CHUNK
    grp = GLA_GROUP
    span = grp * c

    def step(i, carry):
        it = nit - 1 - i
        rows = pl.ds(pl.multiple_of(it * span, span), span)
        logits = _dot(g_ref[rows, :].astype(BF16), gup_ref[:, B_QK:]) + gbias_ref[:, B_QK:]
        la = _log_sigmoid(logits) * (1.0 / B_GATE_TAU)
        cums = [_cumsum_rows(la[c * j:c * (j + 1)], tri3_ref) for j in range(grp)]
        cbx = jnp.concatenate(cums, axis=0) - la
        kt = (k_ref[rows, :] * jnp.exp(cbx)).astype(BF16)
        vb = v_ref[rows, :].astype(BF16)
        upd = [_dot_tn(vb[c * j:c * (j + 1)], kt[c * j:c * (j + 1)]) for j in range(grp)]
        st = state[...]
        for j in reversed(range(grp)):
            sb_ref[it * grp + j] = functools.reduce(
                lambda a, b: a + b, [st[HEAD_DIM * h:HEAD_DIM * (h + 1)] for h in range(B_HEADS)]).astype(BF16)
            st = st * jnp.exp(cums[j][c - 1:c, :]) + upd[j] * emask_ref[...]
        state[...] = st
        return carry

    lax.fori_loop(0, nit, step, 0)


def _gla_main_body(nit, q_ref, k_ref, v_ref, r_ref, g_ref, sb_ref, gup_ref, gbias_ref, tri3_ref,
                   emask_ref, estack_ref, ones_ref, gain_ref, o_ref, state):
    @pl.when(pl.program_id(1) == 0)
    def _():
        state[...] = jnp.zeros_like(state)

    c = GLA_CHUNK
    n = GLA_SUB
    nsub = c // n
    grp = GLA_GROUP
    span = grp * c
    row_8 = _iota((8, B_QK), 0)
    rho = _iota((B_HEADS * n, B_QK), 0)
    head_k_ok = (rho >> 4) == (_iota((B_HEADS * n, B_QK), 1) >> 5)
    rho_v = _iota((B_HEADS * n, B_WIDTH), 0)
    head_v_ok = (rho_v >> 4) == (_iota((B_HEADS * n, B_WIDTH), 1) >> 6)
    own_blk = ((_iota((span, B_HEADS * c), 0) & (c - 1)) >> 4) == (_iota((span, B_HEADS * c), 1) >> 6)
    zeros_k = jnp.zeros((B_HEADS * n, B_QK), F32)

    def rows_of(vals, height):
        return jnp.concatenate([jnp.broadcast_to(x, (height, x.shape[1])) for x in vals], axis=0)

    def by_head(x, ok):
        return jnp.where(ok, jnp.concatenate([x] * B_HEADS, axis=0), 0.0)

    def diag_lhs(q, k, bf2, cb2):
        nblk = span // n
        slabs = []
        for s in range(n):
            ks = rows_of([k[n * i + s:n * i + s + 1] for i in range(nblk)], n)
            pieces = []
            for i in range(nblk):
                ref_f = bf2[n * i + s:n * i + s + 1]
                ref_b = cb2[n * i + s:n * i + s + 1]
                for r0 in range(0, n, 8):
                    rs = slice(n * i + r0, n * i + r0 + 8)
                    if s <= r0:
                        pieces.append(bf2[rs] - ref_f)
                    elif s >= r0 + 8:
                        pieces.append(ref_b - cb2[rs])
                    else:
                        pieces.append(jnp.where(row_8 >= s - r0, bf2[rs] - ref_f, ref_b - cb2[rs]))
            arg = jnp.concatenate(pieces, axis=0)
            slabs.append(((q * ks) * jnp.exp2(arg)).astype(BF16))
        return jnp.concatenate(slabs, axis=1)

    def chunk_local(q, k, v, bf, cbx, kf, kb, att_diag):
        e_rows = [bf[n * j + n - 1:n * j + n, :] for j in range(nsub)]
        f_rows = [cbx[n * j:n * j + 1, :] for j in range(nsub)]
        lhs_parts = []
        for j in range(nsub - 1):
            r0 = n * (j + 1)
            part = q[r0:] * jnp.exp(bf[r0:] - e_rows[j])
            lhs_parts.append(jnp.concatenate([jnp.zeros((r0, B_QK), F32), part], axis=0))
        for j in range(1, nsub):
            r1 = n * j
            part = q[:r1] * jnp.exp(f_rows[j] - cbx[:r1])
            lhs_parts.append(jnp.concatenate([part, jnp.zeros((c - r1, B_QK), F32)], axis=0))
        rhs_rows = []
        for j in range(nsub):
            kfj = by_head(kf[n * j:n * (j + 1)], head_k_ok)
            kbj = by_head(kb[n * j:n * (j + 1)], head_k_ok)
            parts = [kfj if (jj == j and j < nsub - 1) else zeros_k for jj in range(nsub - 1)]
            parts += [kbj if (jj == j and j > 0) else zeros_k for jj in range(1, nsub)]
            rhs_rows.append(jnp.concatenate(parts, axis=1))
        att = _dot_nt(jnp.concatenate(lhs_parts, axis=1).astype(BF16),
                      jnp.concatenate(rhs_rows, axis=0).astype(BF16))
        v4 = jnp.concatenate([by_head(v[n * j:n * (j + 1)], head_v_ok) for j in range(nsub)],
                             axis=0).astype(BF16)
        return _dot((att + att_diag).astype(BF16), v4)

    def step(it, carry):
        rows = pl.ds(pl.multiple_of(it * span, span), span)
        q = q_ref[rows, :]
        k = k_ref[rows, :]
        v = v_ref[rows, :]
        la = _gla_log_decay(g_ref[rows, :], gup_ref, gbias_ref)
        cums = [_cumsum_rows(la[c * j:c * (j + 1)], tri3_ref) for j in range(grp)]
        cum = jnp.concatenate(cums, axis=0)
        bf = cum[:, :B_QK]
        cbx = cum[:, B_QK:] - la[:, B_QK:]
        tot_f = [cj[c - 1:c, :B_QK] for cj in cums]
        tot_b = [cj[c - 1:c, B_QK:] for cj in cums]
        bf2 = bf * LOG2E
        cb2 = cbx * LOG2E
        e_blk = rows_of([bf[n * j + n - 1:n * j + n, :] for j in range(span // n)], n)
        f_blk = rows_of([cbx[n * j:n * j + 1, :] for j in range(span // n)], n)
        kf = k * jnp.exp(e_blk - bf)
        kb = k * jnp.exp(cbx - f_blk)
        kt = (k * jnp.exp(rows_of(tot_f, c) - bf)).astype(BF16)
        lhs_inter = jnp.concatenate([q * jnp.exp(bf), q * jnp.exp(rows_of(tot_b, c) - cbx)],
                                    axis=1).astype(BF16)
        vb = v.astype(BF16)
        att_diag = jnp.where(own_blk, _dot(diag_lhs(q, k, bf2, cb2), estack_ref[...]), 0.0)

        outs = []
        upd = []
        for j in range(grp):
            sl = slice(c * j, c * (j + 1))
            outs.append(chunk_local(q[sl], k[sl], v[sl], bf[sl], cbx[sl], kf[sl], kb[sl], att_diag[sl]))
            upd.append(_dot_tn(vb[sl], kt[sl]))

        st = state[...]
        for j in range(grp):
            sl = slice(c * j, c * (j + 1))
            sb_full = jnp.where(emask_ref[...] > 0.0,
                                jnp.concatenate([sb_ref[it * grp + j]] * B_HEADS, axis=0), 0.0)
            rhs = jnp.concatenate([st.astype(BF16), sb_full.astype(BF16)], axis=1)
            outs[j] = outs[j] + _dot_nt(lhs_inter[sl], rhs)
            st = st * jnp.exp(tot_f[j]) + upd[j] * emask_ref[...]
        state[...] = st

        gate = r_ref[rows, :]
        o = jnp.concatenate(outs, axis=0)
        out = _head_norm(o, ones_ref, gain_ref[...]) * (gate * _sigmoid(gate))
        o_ref[rows, :] = out.astype(o_ref.dtype)
        return carry

    lax.fori_loop(0, nit, step, 0)


def _gla(z3, gup_bd, gbias, tri3, emask, estack, ones256, gain, tb):
    b, seq, _ = z3.shape
    nblk = seq // tb
    nch = tb // GLA_CHUNK
    ntot = seq // GLA_CHUNK

    def zspec(col, width, rev):
        blk = col // width
        if rev:
            return pl.BlockSpec((None, tb, width), lambda bi, i: (bi, nblk - 1 - i, blk))
        return pl.BlockSpec((None, tb, width), lambda bi, i: (bi, i, blk))

    consts = [gup_bd, gbias, tri3, emask]
    const_specs = [_const_spec(a.shape) for a in consts]
    nit = nch // GLA_GROUP
    sb = pl.pallas_call(
        functools.partial(_gla_bwd_body, nit),
        grid=(b, nblk),
        in_specs=[zspec(Z_BK, B_QK, True), zspec(Z_BV, B_WIDTH, True), zspec(Z_BG, LANES, True)]
        + const_specs,
        out_specs=pl.BlockSpec((None, nch, HEAD_DIM, B_QK), lambda bi, i: (bi, nblk - 1 - i, 0, 0)),
        out_shape=jax.ShapeDtypeStruct((b, ntot, HEAD_DIM, B_QK), BF16),
        scratch_shapes=[pltpu.VMEM((B_WIDTH, B_QK), F32)],
        compiler_params=_cparams(("parallel", "arbitrary")),
        name="gla_bwd",
    )(z3, z3, z3, *consts)

    consts2 = [gup_bd, gbias, tri3, emask, estack, ones256, gain]
    return pl.pallas_call(
        functools.partial(_gla_main_body, nit),
        grid=(b, nblk),
        in_specs=[zspec(Z_BQ, B_QK, False), zspec(Z_BK, B_QK, False), zspec(Z_BV, B_WIDTH, False),
                  zspec(Z_BR, B_WIDTH, False), zspec(Z_BG, LANES, False),
                  pl.BlockSpec((None, nch, HEAD_DIM, B_QK), lambda bi, i: (bi, i, 0, 0))]
        + [_const_spec(a.shape) for a in consts2],
        out_specs=pl.BlockSpec((None, tb, B_WIDTH), lambda bi, i: (bi, i, 0)),
        out_shape=jax.ShapeDtypeStruct((b, seq, B_WIDTH), BF16),
        scratch_shapes=[pltpu.VMEM((B_WIDTH, B_QK), F32)],
        compiler_params=_cparams(("parallel", "arbitrary")),
        name="gla_main",
    )(z3, z3, z3, z3, z3, sb, *consts2)


def _ret_bwd_body(nch, k_ref, v_ref, raw_ref, bmask_ref, rb_ref, state):
    @pl.when(pl.program_id(1) == 0)
    def _():
        state[...] = jnp.zeros_like(state)

    c = RET_CHUNK
    lg1 = _log_sigmoid(raw_ref[1:2, :])
    pos = _iota((c, C_WIDTH), 0).astype(F32)
    kdec = jnp.exp(pos * lg1)
    chunk_decay = jnp.exp(float(c) * lg1)

    upd = []
    for j in range(nch):
        rows = pl.ds(c * j, c)
        kt = (k_ref[rows, :] * kdec).astype(BF16)
        upd.append(_dot_tn(kt, v_ref[rows, :].astype(BF16)))
    st = state[...]
    half = C_WIDTH // 2
    for j in reversed(range(nch)):
        rb_ref[j] = jnp.concatenate([st[:half, :half], st[half:, half:]], axis=0).astype(BF16)
        st = chunk_decay * st + upd[j] * bmask_ref[...]
    state[...] = st


def _ret_main_body(nch, q_ref, k_ref, v_ref, g_ref, rb_ref, raw_ref, raw_s_ref, bmask_ref, ones_ref,
                   gain_ref, o_ref, state):
    @pl.when(pl.program_id(1) == 0)
    def _():
        state[...] = jnp.zeros_like(state)

    c = RET_CHUNK
    lg = _log_sigmoid(raw_ref[...])
    lg0, lg1 = lg[0:1, :], lg[1:2, :]
    lgs = _log_sigmoid(raw_s_ref[...])
    pos = _iota((c, C_WIDTH), 0).astype(F32)
    qdec_f = jnp.exp((pos + 1.0) * lg0)
    qdec_b = jnp.exp((float(c) - pos) * lg1)
    kdec_f = jnp.exp((float(c) - 1.0 - pos) * lg0)
    chunk_decay = jnp.exp(float(c) * lg0)
    t_idx = _iota((c, C_HEADS * c), 0)
    s_idx = _iota((c, C_HEADS * c), 1) & (c - 1)
    rel = (t_idx - s_idx).astype(F32)
    dmat = jnp.where(rel >= 0.0, jnp.exp(jnp.maximum(rel, 0.0) * lgs[0:1, :]),
                     jnp.exp(jnp.maximum(-rel, 0.0) * lgs[1:2, :]))
    rho = _iota((C_HEADS * c, C_WIDTH), 0)
    lane = _iota((C_HEADS * c, C_WIDTH), 1)
    head_ok = (rho >> 7) == (lane >> 6)

    outs, lhs_inter, upd = [], [], []
    for j in range(nch):
        rows = pl.ds(c * j, c)
        q = q_ref[rows, :]
        k = k_ref[rows, :]
        v = v_ref[rows, :]
        k4 = jnp.where(head_ok, jnp.concatenate([k] * C_HEADS, axis=0), 0.0).astype(BF16)
        v4 = jnp.where(head_ok, jnp.concatenate([v] * C_HEADS, axis=0), 0.0).astype(BF16)
        scores = _dot_nt(q.astype(BF16), k4) * dmat
        outs.append(_dot(scores.astype(BF16), v4))
        lhs_inter.append(jnp.concatenate([q * qdec_f, q * qdec_b], axis=1).astype(BF16))
        upd.append(_dot_tn((k * kdec_f).astype(BF16), v.astype(BF16)))
    st = state[...]
    half = C_WIDTH // 2
    zero_q = jnp.zeros((half, half), BF16)
    for j in range(nch):
        rbj = rb_ref[j]
        rb_full = jnp.concatenate([jnp.concatenate([rbj[:half], zero_q], axis=1),
                                   jnp.concatenate([zero_q, rbj[half:]], axis=1)], axis=0)
        rhs = jnp.concatenate([st.astype(BF16), rb_full], axis=0)
        outs[j] = outs[j] + _dot(lhs_inter[j], rhs)
        st = chunk_decay * st + upd[j] * bmask_ref[...]
    state[...] = st
    for j in range(nch):
        rows = pl.ds(c * j, c)
        gate = g_ref[rows, :]
        out = _head_norm(outs[j], ones_ref, gain_ref[...]) * (gate * _sigmoid(gate))
        o_ref[rows, :] = out.astype(o_ref.dtype)


def _ret(z3, raw256, raw512, ones256, gain, tb):
    b, seq, _ = z3.shape
    nblk = seq // tb
    nch = tb // RET_CHUNK
    ntot = seq // RET_CHUNK

    def zspec(col, rev):
        blk = col // C_WIDTH
        if rev:
            return pl.BlockSpec((None, tb, C_WIDTH), lambda bi, i: (bi, nblk - 1 - i, blk))
        return pl.BlockSpec((None, tb, C_WIDTH), lambda bi, i: (bi, i, blk))

    rb = pl.pallas_call(
        functools.partial(_ret_bwd_body, nch),
        grid=(b, nblk),
        in_specs=[zspec(Z_CK, True), zspec(Z_CV, True), _const_spec(raw256.shape),
                  _const_spec(ones256.shape)],
        out_specs=pl.BlockSpec((None, nch, C_WIDTH, C_WIDTH // 2), lambda bi, i: (bi, nblk - 1 - i, 0, 0)),
        out_shape=jax.ShapeDtypeStruct((b, ntot, C_WIDTH, C_WIDTH // 2), BF16),
        scratch_shapes=[pltpu.VMEM((C_WIDTH, C_WIDTH), F32)],
        compiler_params=_cparams(("parallel", "arbitrary")),
        name="ret_bwd",
    )(z3, z3, raw256, ones256)

    consts = [raw256, raw512, ones256, ones256, gain]
    return pl.pallas_call(
        functools.partial(_ret_main_body, nch),
        grid=(b, nblk),
        in_specs=[zspec(Z_CQ, False), zspec(Z_CK, False), zspec(Z_CV, False), zspec(Z_CG, False),
                  pl.BlockSpec((None, nch, C_WIDTH, C_WIDTH // 2), lambda bi, i: (bi, i, 0, 0))]
        + [_const_spec(a.shape) for a in consts],
        out_specs=pl.BlockSpec((None, tb, C_WIDTH), lambda bi, i: (bi, i, 0)),
        out_shape=jax.ShapeDtypeStruct((b, seq, C_WIDTH), BF16),
        scratch_shapes=[pltpu.VMEM((C_WIDTH, C_WIDTH), F32)],
        compiler_params=_cparams(("parallel", "arbitrary")),
        name="ret_main",
    )(z3, z3, z3, z3, rb, *consts)


def _post_body(h_ref, oa_ref, ob_ref, oc_ref, p_ref, wo_ref, lm_ref, w1_ref, w2_ref, lp_ref,
               wg_ref, wp_ref, y_ref):
    def rms(x, g_ref):
        ms = jnp.mean(x * x, axis=-1, keepdims=True)
        return (x * lax.rsqrt(ms + EPS) * g_ref[...]).astype(BF16)

    h = h_ref[...]
    h = h + (_dot(oa_ref[...], wo_ref[0:A_WIDTH, :])
             + _dot(ob_ref[...], wo_ref[A_WIDTH:A_WIDTH + B_WIDTH, :])
             + _dot(oc_ref[...], wo_ref[A_WIDTH + B_WIDTH:, :]))
    m = rms(h, lm_ref)
    ff = D_FF // 4
    mlp = None
    for j in range(4):
        hid = _dot(m, w1_ref[:, ff * j:ff * (j + 1)])
        hid = jnp.square(jnp.maximum(hid, 0.0)).astype(BF16)
        part = _dot(hid, w2_ref[ff * j:ff * (j + 1), :])
        mlp = part if mlp is None else mlp + part
    h = h + mlp
    gate = _sigmoid(_dot(rms(h, lp_ref), wg_ref[...]))
    y_ref[...] = h + gate * _dot(p_ref[...].astype(BF16), wp_ref[...])


def _post(h2, oa, ob, oc, ple, layer, wo, lm, w1, w2, lp, wg, wp, tm):
    n = h2.shape[0]

    def tok(width):
        return pl.BlockSpec((tm, width), lambda i: (i, 0))

    consts = [wo, lm, w1, w2, lp, wg, wp]
    return pl.pallas_call(
        _post_body,
        grid=(n // tm,),
        in_specs=[tok(D_MODEL), tok(A_WIDTH), tok(B_WIDTH), tok(C_WIDTH),
                  pl.BlockSpec((None, tm, PLE_DIM), lambda i: (layer, i, 0))]
        + [_const_spec(a.shape) for a in consts],
        out_specs=tok(D_MODEL),
        out_shape=jax.ShapeDtypeStruct((n, D_MODEL), F32),
        compiler_params=_cparams(("parallel",)),
        name="post",
    )(h2, oa, ob, oc, ple, *consts)


def _rope_tables(seq, rot_dim, theta):
    half = rot_dim // 2
    inv_freq = 1.0 / (theta ** (jnp.arange(half, dtype=F32) * (2.0 / rot_dim)))
    ang = jnp.arange(seq, dtype=F32)[:, None] * inv_freq[None, :]
    cos, sin = jnp.cos(ang), jnp.sin(ang)
    pad = HEAD_DIM - rot_dim
    c = jnp.concatenate([cos, cos, jnp.ones((seq, pad), F32)], axis=1)
    sn = jnp.concatenate([-sin, jnp.zeros((seq, half + pad), F32)], axis=1)
    sp = jnp.concatenate([jnp.zeros((seq, half), F32), sin, jnp.zeros((seq, pad), F32)], axis=1)
    return jnp.concatenate([jnp.tile(t, (1, LANES // HEAD_DIM)) for t in (c, sn, sp)], axis=1)


def _block_ones(rows, cols, rblk, cblk):
    r = jnp.arange(rows)[:, None] // rblk
    c = jnp.arange(cols)[None, :] // cblk
    return r == c


def _layer_consts(i, ln_mix, w_in, attn_q_norm, attn_k_norm, gla_gate_up, gla_gate_bias, gla_out_norm,
                  ret_decay_raw, ret_out_norm, w_out, ln_mlp, w_mlp_in, w_mlp_out, ln_pe, w_pe_gate,
                  w_pe_proj):
    w = w_in[i]
    w_p = jnp.concatenate([w[:, :2304], w[:, 2336:N_IN], w[:, 2304:2336],
                           jnp.zeros((D_MODEL, Z_WIDTH - N_IN), F32)], axis=1).astype(BF16)
    gup = gla_gate_up[i].astype(BF16)
    gup_bd = jnp.zeros((LANES, 2 * B_QK), BF16)
    gup_bd = gup_bd.at[0:B_GATE_RANK, 0:B_QK].set(gup[0])
    gup_bd = gup_bd.at[B_GATE_RANK:2 * B_GATE_RANK, B_QK:].set(gup[1])
    return dict(
        ln_mix=ln_mix[i][None, :], w_p=w_p,
        qg=jnp.tile(attn_q_norm[i], 4)[None, :], kg=jnp.tile(attn_k_norm[i], 4)[None, :],
        gup_bd=gup_bd, gbias=gla_gate_bias[i].reshape(1, 2 * B_QK),
        gla_gain=gla_out_norm[i][None, :],
        raw256=jnp.repeat(ret_decay_raw[i], C_DIM, axis=1),
        raw512=jnp.repeat(ret_decay_raw[i], RET_CHUNK, axis=1),
        ret_gain=ret_out_norm[i][None, :],
        wo=w_out[i].astype(BF16), lm=ln_mlp[i][None, :], w1=w_mlp_in[i].astype(BF16),
        w2=w_mlp_out[i].astype(BF16), lp=ln_pe[i][None, :], wg=w_pe_gate[i].astype(BF16),
        wp=w_pe_proj[i].astype(BF16),
    )


def _shared_consts():
    tri = jnp.arange(GLA_CHUNK)[:, None] >= jnp.arange(GLA_CHUNK)[None, :]
    return dict(
        ones256=_block_ones(256, 256, HEAD_DIM, HEAD_DIM).astype(BF16),
        tri3=jnp.tile(tri, (1, 3)).astype(BF16),
        emask=_block_ones(B_WIDTH, B_QK, HEAD_DIM, B_KDIM).astype(F32),
        estack=_gla_diag_selector().astype(BF16),
    )


def _gla_diag_selector():
    r = jnp.arange(GLA_SUB * B_QK)
    c = jnp.arange(B_HEADS * GLA_CHUNK)
    same_s = (r // B_QK)[:, None] == (c % GLA_SUB)[None, :]
    same_head = ((r % B_QK) // B_KDIM)[:, None] == ((c % (B_HEADS * GLA_SUB)) // GLA_SUB)[None, :]
    return same_s & same_head


def _run_trunk(x, p, layers, shared):
    b, seq, _ = x.shape
    n = b * seq
    tm = 512
    tb_gla, tb_ret = 2048, 2048
    rope_a = _rope_tables(seq, ROPE_DIM, ROPE_THETA)
    rope_c = _rope_tables(seq, C_DIM, RET_THETA)
    h = x.reshape(n, D_MODEL)
    for i, lc in enumerate(layers):
        z = _proj_in(h, seq, lc["ln_mix"], lc["w_p"], shared["ones256"], lc["qg"], lc["kg"],
                     rope_a, rope_c, tm)
        z3 = z.reshape(b, seq, Z_WIDTH)
        oa = _attn(z3)
        ob = _gla(z3, lc["gup_bd"], lc["gbias"], shared["tri3"], shared["emask"], shared["estack"],
                  shared["ones256"], lc["gla_gain"], tb_gla)
        oc = _ret(z3, lc["raw256"], lc["raw512"], shared["ones256"], lc["ret_gain"], tb_ret)
        h = _post(h, oa.reshape(n, A_WIDTH), ob.reshape(n, B_WIDTH), oc.reshape(n, C_WIDTH),
                  p.reshape(p.shape[0], n, PLE_DIM), i, lc["wo"], lc["lm"], lc["w1"], lc["w2"], lc["lp"],
                  lc["wg"], lc["wp"], tm)
    return h.reshape(b, seq, D_MODEL)


def kernel(x_prompt, x_sample, p_prompt, p_sample, ln_mix, w_in, attn_q_norm, attn_k_norm, gla_gate_up, gla_gate_bias, gla_out_norm, ret_decay_raw, ret_out_norm, w_out, ln_mlp, w_mlp_in, w_mlp_out, ln_pe, w_pe_gate, w_pe_proj):
    depth = w_in.shape[0]
    layers = [_layer_consts(i, ln_mix, w_in, attn_q_norm, attn_k_norm, gla_gate_up, gla_gate_bias,
                            gla_out_norm, ret_decay_raw, ret_out_norm, w_out, ln_mlp, w_mlp_in,
                            w_mlp_out, ln_pe, w_pe_gate, w_pe_proj) for i in range(depth)]
    shared = _shared_consts()
    y_prompt = _run_trunk(x_prompt, p_prompt, layers, shared)
    y_sample = _run_trunk(x_sample, p_sample, layers, shared)
    return (y_prompt, y_sample)
```

```python
import functools

import jax
import jax.numpy as jnp
from jax import lax
from jax.experimental import pallas as pl
from jax.experimental.pallas import tpu as pltpu

F32 = jnp.float32
BF16 = jnp.bfloat16

D_MODEL = 1024
HEAD_DIM = 64
A_HEADS = 8
A_WIDTH = 512
A_PATTERNS = ((128, 1), (512, 4), (2048, 16))
A_RADIUS = 64
A_MAX_REACH = 1024
ROPE_THETA = 500000.0
ROPE_DIM = 16
B_HEADS = 4
B_KDIM = 32
B_QK = 128
B_WIDTH = 256
B_GATE_RANK = 16
B_GATE_TAU = 16.0
C_HEADS = 4
C_DIM = 64
C_WIDTH = 256
RET_THETA = 10000.0
N_IN = 3360
D_FF = 4096
PLE_DIM = 256
EPS = 1e-6
NEG = -1e30
LOG2E = 1.4426950408889634

LANES = 128
VMEM_LIMIT = 56 * 1024 * 1024

Z_AQ, Z_AK, Z_AV = 0, 512, 1024
Z_BQ, Z_BK, Z_BV, Z_BR = 1536, 1664, 1792, 2048
Z_CQ, Z_CK, Z_CV, Z_CG = 2304, 2560, 2816, 3072
Z_BG = 3328
Z_WIDTH = 3456

GLA_CHUNK = 64
GLA_SUB = 16
GLA_GROUP = 8
RET_CHUNK = 128
ATT_TILE = 2048
ATT_SUB = 128
ATT_KEYS = ATT_SUB + 2 * A_RADIUS
ATT_SKEW = 2


def _cparams(sem):
    return pltpu.CompilerParams(dimension_semantics=sem, vmem_limit_bytes=VMEM_LIMIT)


def _const_spec(shape):
    nd = len(shape)
    return pl.BlockSpec(shape, lambda *_: (0,) * nd, pipeline_mode=pl.Buffered(1))


def _sigmoid(x):
    return 1.0 / (1.0 + jnp.exp(-x))


def _log_sigmoid(x):
    return jnp.minimum(x, 0.0) - jnp.log1p(jnp.exp(-jnp.abs(x)))


def _iota(shape, dim):
    return lax.broadcasted_iota(jnp.int32, shape, dim)


def _dot(a, b):
    return jnp.dot(a, b, preferred_element_type=F32)


def _dot_nt(a, b):
    return lax.dot_general(a, b, (((1,), (1,)), ((), ())), preferred_element_type=F32)


def _dot_tn(a, b):
    return lax.dot_general(a, b, (((0,), (0,)), ((), ())), preferred_element_type=F32)


def _head_norm(o, ones_ref, gain):
    ssum = _dot((o * o).astype(BF16), ones_ref[...])
    return o * lax.rsqrt(ssum * (1.0 / HEAD_DIM) + EPS) * gain


def _rope(y, tab_ref, shift):
    c = tab_ref[:, 0:LANES]
    sn = tab_ref[:, LANES:2 * LANES]
    sp = tab_ref[:, 2 * LANES:3 * LANES]
    outs = []
    for j in range(y.shape[1] // LANES):
        yj = y[:, j * LANES:(j + 1) * LANES]
        outs.append(yj * c + pltpu.roll(yj, LANES - shift, 1) * sn + pltpu.roll(yj, shift, 1) * sp)
    return jnp.concatenate(outs, axis=1)


def _proj_in_body(x_ref, g_ref, w_ref, ones_ref, qg_ref, kg_ref, ra_ref, rc_ref, z_ref):
    x = x_ref[...]
    ms = jnp.mean(x * x, axis=-1, keepdims=True)
    u = (x * lax.rsqrt(ms + EPS) * g_ref[...]).astype(BF16)

    def proj(a, b):
        return _dot(u, w_ref[:, a:b])

    def qk_norm(y, gain_ref):
        halves = []
        for j in range(2):
            yj = y[:, 256 * j:256 * (j + 1)]
            halves.append(_head_norm(yj, ones_ref, gain_ref[...]))
        return jnp.concatenate(halves, axis=1)

    aq = _rope(qk_norm(proj(Z_AQ, Z_AK), qg_ref), ra_ref, ROPE_DIM // 2)
    z_ref[:, Z_AQ:Z_AK] = aq * (HEAD_DIM ** -0.5 * LOG2E)
    z_ref[:, Z_AK:Z_AV] = _rope(qk_norm(proj(Z_AK, Z_AV), kg_ref), ra_ref, ROPE_DIM // 2)
    z_ref[:, Z_AV:Z_BQ] = proj(Z_AV, Z_BQ)
    z_ref[:, Z_BQ:Z_BK] = proj(Z_BQ, Z_BK) * (B_KDIM ** -0.5)
    z_ref[:, Z_BK:Z_CQ] = proj(Z_BK, Z_CQ)
    z_ref[:, Z_CQ:Z_CK] = _rope(proj(Z_CQ, Z_CK), rc_ref, C_DIM // 2)
    z_ref[:, Z_CK:Z_CV] = _rope(proj(Z_CK, Z_CV), rc_ref, C_DIM // 2) * (C_DIM ** -0.5)
    z_ref[:, Z_CV:Z_WIDTH] = proj(Z_CV, Z_WIDTH)


def _proj_in(x2, seq, ln, w_p, ones256, qg, kg, rope_a, rope_c, tm):
    n = x2.shape[0]
    per_seq = seq // tm
    return pl.pallas_call(
        _proj_in_body,
        grid=(n // tm,),
        in_specs=[
            pl.BlockSpec((tm, D_MODEL), lambda i: (i, 0)),
            _const_spec((1, D_MODEL)),
            _const_spec((D_MODEL, Z_WIDTH)),
            _const_spec((256, 256)),
            _const_spec((1, 256)),
            _const_spec((1, 256)),
            pl.BlockSpec((tm, 3 * LANES), lambda i: (i % per_seq, 0)),
            pl.BlockSpec((tm, 3 * LANES), lambda i: (i % per_seq, 0)),
        ],
        out_specs=pl.BlockSpec((tm, Z_WIDTH), lambda i: (i, 0)),
        out_shape=jax.ShapeDtypeStruct((n, Z_WIDTH), F32),
        compiler_params=_cparams(("parallel",)),
        name="proj_in",
    )(x2, ln, w_p, ones256, qg, kg, rope_a, rope_c)


def _attn_body(seq, q_ref, kp_ref, kc_ref, kn_ref, vp_ref, vc_ref, vn_ref, o_ref,
               kbuf, vbuf, m_ref, l_ref, acc_ref, band_ref):
    t = ATT_TILE
    halo = A_MAX_REACH
    sub = ATT_SUB
    tile_start = pl.program_id(2) * t
    kbuf[0:halo, :] = kp_ref[...]
    kbuf[halo:halo + t, :] = kc_ref[...]
    kbuf[halo + t:, :] = kn_ref[...]
    vbuf[0:halo, :] = vp_ref[...]
    vbuf[halo:halo + t, :] = vc_ref[...]
    vbuf[halo + t:, :] = vn_ref[...]

    low_half = _iota((sub, LANES), 1) < HEAD_DIM
    rel = _iota((sub, ATT_KEYS), 1) - _iota((sub, ATT_KEYS), 0)
    band_ref[...] = jnp.where((rel >= 0) & (rel <= 2 * A_RADIUS), 0.0, NEG)
    c_row = _iota((1, ATT_KEYS), 1)
    ones_v = jnp.ones((ATT_KEYS, LANES), BF16)

    def operands(tile):
        pat, qs, dil = tile
        span = A_RADIUS * dil
        q_sub = q_ref[pl.ds(qs, sub, stride=dil), :]
        k_sub = kbuf[pl.ds(halo + qs - span, ATT_KEYS, stride=dil), :].astype(BF16)
        v_sub = vbuf[pl.ds(halo + qs - span, ATT_KEYS, stride=dil), :].astype(BF16)
        return q_sub, k_sub, jnp.concatenate([v_sub, ones_v], axis=1)

    def scores_stage(tile, head, q_sub, k_sub):
        pat, qs, dil = tile
        span = A_RADIUS * dil
        q_h = (jnp.where(low_half, q_sub, 0.0) if head == 0 else jnp.where(low_half, 0.0, q_sub))
        s = _dot_nt(q_h.astype(BF16), k_sub) + band_ref[...]
        if qs - span < 0 or qs - span + dil * (ATT_KEYS - 1) >= t:
            key_pos = (tile_start + qs - span) + dil * c_row
            s = s + jnp.where((key_pos >= 0) & (key_pos < seq), 0.0, NEG)
        return s, jnp.broadcast_to(jnp.max(s, axis=1, keepdims=True), (sub, LANES))

    def values_stage(s, m_row, v_ones):
        p = jnp.exp2(s - jnp.concatenate([m_row] * (ATT_KEYS // LANES), axis=1)).astype(BF16)
        return _dot(p, v_ones)

    def store_stage(tile, res0, res1):
        pat, qs, dil = tile
        rw = pl.ds(qs, sub, stride=dil)
        (m0, pvl0), (m1, pvl1) = res0, res1
        m_ref[pat, rw, :] = jnp.where(low_half, m0, m1)
        l_ref[pat, rw, :] = jnp.where(low_half, pvl0[:, LANES:], pvl1[:, LANES:])
        acc_ref[pat, rw, :] = jnp.where(low_half, pvl0[:, 0:LANES], pvl1[:, 0:LANES])

    tiles = [(pat, r + j * sub * dil, dil)
             for pat, (_, dil) in enumerate(A_PATTERNS)
             for r in range(dil) for j in range(t // (sub * dil))]
    units = [(ti, head) for ti in range(len(tiles)) for head in range(2)]
    skew = 2 * ATT_SKEW
    ops, scored, valued = {}, {}, {}
    for step in range(len(units) + 2 * skew):
        i_store, i_val = step - 2 * skew, step - skew
        if 0 <= i_store < len(units) and units[i_store][1] == 1:
            ti = units[i_store][0]
            store_stage(tiles[ti], valued.pop((ti, 0)), valued.pop((ti, 1)))
            ops.pop(ti)
        if 0 <= i_val < len(units):
            ti, head = units[i_val]
            s, m_row = scored.pop((ti, head))
            valued[(ti, head)] = (m_row, values_stage(s, m_row, ops[ti][2]))
        if step < len(units):
            ti, head = units[step]
            if head == 0:
                ops[ti] = operands(tiles[ti])
            scored[(ti, head)] = scores_stage(tiles[ti], head, ops[ti][0], ops[ti][1])

    npat = len(A_PATTERNS)
    blk = 2 * sub
    for i in range(t // blk):
        rs = pl.ds(i * blk, blk)
        m_g = [m_ref[g, rs, :] for g in range(npat)]
        m_all = functools.reduce(jnp.maximum, m_g)
        w_g = [jnp.exp2(m - m_all) for m in m_g]
        l_all = functools.reduce(lambda a, b: a + b, [w * l_ref[g, rs, :] for g, w in enumerate(w_g)])
        num = functools.reduce(lambda a, b: a + b, [w * acc_ref[g, rs, :] for g, w in enumerate(w_g)])
        o_ref[rs, :] = (num / l_all).astype(o_ref.dtype)


def _attn(z3):
    b, seq, _ = z3.shape
    t = ATT_TILE
    halo = A_MAX_REACH
    nt = seq // t
    per_tile = t // halo
    n_halo = seq // halo
    qc, kc, vc = Z_AQ // LANES, Z_AK // LANES, Z_AV // LANES

    def cur(c0):
        return pl.BlockSpec((None, t, LANES), lambda bi, hp, i: (bi, i, c0 + hp))

    def prev(c0):
        return pl.BlockSpec((None, halo, LANES),
                            lambda bi, hp, i: (bi, jnp.maximum(i * per_tile - 1, 0), c0 + hp))

    def nxt(c0):
        return pl.BlockSpec((None, halo, LANES),
                            lambda bi, hp, i: (bi, jnp.minimum((i + 1) * per_tile, n_halo - 1), c0 + hp))

    return pl.pallas_call(
        functools.partial(_attn_body, seq),
        grid=(b, A_HEADS // 2, nt),
        in_specs=[cur(qc), prev(kc), cur(kc), nxt(kc), prev(vc), cur(vc), nxt(vc)],
        out_specs=pl.BlockSpec((None, t, LANES), lambda bi, hp, i: (bi, i, hp)),
        out_shape=jax.ShapeDtypeStruct((b, seq, A_WIDTH), BF16),
        scratch_shapes=[
            pltpu.VMEM((t + 2 * halo, LANES), F32),
            pltpu.VMEM((t + 2 * halo, LANES), F32),
            pltpu.VMEM((len(A_PATTERNS), t, LANES), F32),
            pltpu.VMEM((len(A_PATTERNS), t, LANES), F32),
            pltpu.VMEM((len(A_PATTERNS), t, LANES), F32),
            pltpu.VMEM((ATT_SUB, ATT_KEYS), F32),
        ],
        compiler_params=_cparams(("parallel", "parallel", "parallel")),
        name="attn",
    )(z3, z3, z3, z3, z3, z3, z3)


def _split3(x):
    hi = x.astype(BF16)
    r1 = x - hi.astype(F32)
    mid = r1.astype(BF16)
    lo = (r1 - mid.astype(F32)).astype(BF16)
    return hi, mid, lo


def _gla_log_decay(g, gup_ref, gbias_ref):
    logits = _dot(g.astype(BF16), gup_ref[...]) + gbias_ref[...]
    return _log_sigmoid(logits) * (1.0 / B_GATE_TAU)


def _cumsum_rows(la, tri3_ref):
    hi, mid, lo = _split3(la)
    return _dot(tri3_ref[...], jnp.concatenate([hi, mid, lo], axis=0))


def _gla_bwd_body(nit, k_ref, v_ref, g_ref, gup_ref, gbias_ref, tri3_ref, emask_ref, sb_ref, state):
    @pl.when(pl.program_id(1) == 0)
    def _():
        state[...] = jnp.zeros_like(state)

    c = GLA_CHUNK
    grp = GLA_GROUP
    span = grp * c

    def step(i, carry):
        it = nit - 1 - i
        rows = pl.ds(pl.multiple_of(it * span, span), span)
        logits = _dot(g_ref[rows, :].astype(BF16), gup_ref[:, B_QK:]) + gbias_ref[:, B_QK:]
        la = _log_sigmoid(logits) * (1.0 / B_GATE_TAU)
        cums = [_cumsum_rows(la[c * j:c * (j + 1)], tri3_ref) for j in range(grp)]
        cbx = jnp.concatenate(cums, axis=0) - la
        kt = (k_ref[rows, :] * jnp.exp(cbx)).astype(BF16)
        vb = v_ref[rows, :].astype(BF16)
        upd = [_dot_tn(vb[c * j:c * (j + 1)], kt[c * j:c * (j + 1)]) for j in range(grp)]
        st = state[...]
        for j in reversed(range(grp)):
            sb_ref[it * grp + j] = functools.reduce(
                lambda a, b: a + b, [st[HEAD_DIM * h:HEAD_DIM * (h + 1)] for h in range(B_HEADS)]).astype(BF16)
            st = st * jnp.exp(cums[j][c - 1:c, :]) + upd[j] * emask_ref[...]
        state[...] = st
        return carry

    lax.fori_loop(0, nit, step, 0)


def _gla_main_body(nit, q_ref, k_ref, v_ref, r_ref, g_ref, sb_ref, gup_ref, gbias_ref, tri3_ref,
                   emask_ref, estack_ref, ones_ref, gain_ref, o_ref, state):
    @pl.when(pl.program_id(1) == 0)
    def _():
        state[...] = jnp.zeros_like(state)

    c = GLA_CHUNK
    n = GLA_SUB
    nsub = c // n
    grp = GLA_GROUP
    span = grp * c
    row_8 = _iota((8, B_QK), 0)
    rho = _iota((B_HEADS * n, B_QK), 0)
    head_k_ok = (rho >> 4) == (_iota((B_HEADS * n, B_QK), 1) >> 5)
    rho_v = _iota((B_HEADS * n, B_WIDTH), 0)
    head_v_ok = (rho_v >> 4) == (_iota((B_HEADS * n, B_WIDTH), 1) >> 6)
    own_blk = ((_iota((span, B_HEADS * c), 0) & (c - 1)) >> 4) == (_iota((span, B_HEADS * c), 1) >> 6)
    zeros_k = jnp.zeros((B_HEADS * n, B_QK), F32)

    def rows_of(vals, height):
        return jnp.concatenate([jnp.broadcast_to(x, (height, x.shape[1])) for x in vals], axis=0)

    def by_head(x, ok):
        return jnp.where(ok, jnp.concatenate([x] * B_HEADS, axis=0), 0.0)

    def diag_lhs(q, k, bf2, cb2):
        nblk = span // n
        slabs = []
        for s in range(n):
            ks = rows_of([k[n * i + s:n * i + s + 1] for i in range(nblk)], n)
            pieces = []
            for i in range(nblk):
                ref_f = bf2[n * i + s:n * i + s + 1]
                ref_b = cb2[n * i + s:n * i + s + 1]
                for r0 in range(0, n, 8):
                    rs = slice(n * i + r0, n * i + r0 + 8)
                    if s <= r0:
                        pieces.append(bf2[rs] - ref_f)
                    elif s >= r0 + 8:
                        pieces.append(ref_b - cb2[rs])
                    else:
                        pieces.append(jnp.where(row_8 >= s - r0, bf2[rs] - ref_f, ref_b - cb2[rs]))
            arg = jnp.concatenate(pieces, axis=0)
            slabs.append(((q * ks) * jnp.exp2(arg)).astype(BF16))
        return jnp.concatenate(slabs, axis=1)

    def chunk_local(q, k, v, bf, cbx, kf, kb, att_diag):
        e_rows = [bf[n * j + n - 1:n * j + n, :] for j in range(nsub)]
        f_rows = [cbx[n * j:n * j + 1, :] for j in range(nsub)]
        lhs_parts = []
        for j in range(nsub - 1):
            r0 = n * (j + 1)
            part = q[r0:] * jnp.exp(bf[r0:] - e_rows[j])
            lhs_parts.append(jnp.concatenate([jnp.zeros((r0, B_QK), F32), part], axis=0))
        for j in range(1, nsub):
            r1 = n * j
            part = q[:r1] * jnp.exp(f_rows[j] - cbx[:r1])
            lhs_parts.append(jnp.concatenate([part, jnp.zeros((c - r1, B_QK), F32)], axis=0))
        rhs_rows = []
        for j in range(nsub):
            kfj = by_head(kf[n * j:n * (j + 1)], head_k_ok)
            kbj = by_head(kb[n * j:n * (j + 1)], head_k_ok)
            parts = [kfj if (jj == j and j < nsub - 1) else zeros_k for jj in range(nsub - 1)]
            parts += [kbj if (jj == j and j > 0) else zeros_k for jj in range(1, nsub)]
            rhs_rows.append(jnp.concatenate(parts, axis=1))
        att = _dot_nt(jnp.concatenate(lhs_parts, axis=1).astype(BF16),
                      jnp.concatenate(rhs_rows, axis=0).astype(BF16))
        v4 = jnp.concatenate([by_head(v[n * j:n * (j + 1)], head_v_ok) for j in range(nsub)],
                             axis=0).astype(BF16)
        return _dot((att + att_diag).astype(BF16), v4)

    def step(it, carry):
        rows = pl.ds(pl.multiple_of(it * span, span), span)
        q = q_ref[rows, :]
        k = k_ref[rows, :]
        v = v_ref[rows, :]
        la = _gla_log_decay(g_ref[rows, :], gup_ref, gbias_ref)
        cums = [_cumsum_rows(la[c * j:c * (j + 1)], tri3_ref) for j in range(grp)]
        cum = jnp.concatenate(cums, axis=0)
        bf = cum[:, :B_QK]
        cbx = cum[:, B_QK:] - la[:, B_QK:]
        tot_f = [cj[c - 1:c, :B_QK] for cj in cums]
        tot_b = [cj[c - 1:c, B_QK:] for cj in cums]
        bf2 = bf * LOG2E
        cb2 = cbx * LOG2E
        e_blk = rows_of([bf[n * j + n - 1:n * j + n, :] for j in range(span // n)], n)
        f_blk = rows_of([cbx[n * j:n * j + 1, :] for j in range(span // n)], n)
        kf = k * jnp.exp(e_blk - bf)
        kb = k * jnp.exp(cbx - f_blk)
        kt = (k * jnp.exp(rows_of(tot_f, c) - bf)).astype(BF16)
        lhs_inter = jnp.concatenate([q * jnp.exp(bf), q * jnp.exp(rows_of(tot_b, c) - cbx)],
                                    axis=1).astype(BF16)
        vb = v.astype(BF16)
        att_diag = jnp.where(own_blk, _dot(diag_lhs(q, k, bf2, cb2), estack_ref[...]), 0.0)

        outs = []
        upd = []
        for j in range(grp):
            sl = slice(c * j, c * (j + 1))
            outs.append(chunk_local(q[sl], k[sl], v[sl], bf[sl], cbx[sl], kf[sl], kb[sl], att_diag[sl]))
            upd.append(_dot_tn(vb[sl], kt[sl]))

        st = state[...]
        for j in range(grp):
            sl = slice(c * j, c * (j + 1))
            sb_full = jnp.where(emask_ref[...] > 0.0,
                                jnp.concatenate([sb_ref[it * grp + j]] * B_HEADS, axis=0), 0.0)
            rhs = jnp.concatenate([st.astype(BF16), sb_full.astype(BF16)], axis=1)
            outs[j] = outs[j] + _dot_nt(lhs_inter[sl], rhs)
            st = st * jnp.exp(tot_f[j]) + upd[j] * emask_ref[...]
        state[...] = st

        gate = r_ref[rows, :]
        o = jnp.concatenate(outs, axis=0)
        out = _head_norm(o, ones_ref, gain_ref[...]) * (gate * _sigmoid(gate))
        o_ref[rows, :] = out.astype(o_ref.dtype)
        return carry

    lax.fori_loop(0, nit, step, 0)


def _gla(z3, gup_bd, gbias, tri3, emask, estack, ones256, gain, tb):
    b, seq, _ = z3.shape
    nblk = seq // tb
    nch = tb // GLA_CHUNK
    ntot = seq // GLA_CHUNK

    def zspec(col, width, rev):
        blk = col // width
        if rev:
            return pl.BlockSpec((None, tb, width), lambda bi, i: (bi, nblk - 1 - i, blk))
        return pl.BlockSpec((None, tb, width), lambda bi, i: (bi, i, blk))

    consts = [gup_bd, gbias, tri3, emask]
    const_specs = [_const_spec(a.shape) for a in consts]
    nit = nch // GLA_GROUP
    sb = pl.pallas_call(
        functools.partial(_gla_bwd_body, nit),
        grid=(b, nblk),
        in_specs=[zspec(Z_BK, B_QK, True), zspec(Z_BV, B_WIDTH, True), zspec(Z_BG, LANES, True)]
        + const_specs,
        out_specs=pl.BlockSpec((None, nch, HEAD_DIM, B_QK), lambda bi, i: (bi, nblk - 1 - i, 0, 0)),
        out_shape=jax.ShapeDtypeStruct((b, ntot, HEAD_DIM, B_QK), BF16),
        scratch_shapes=[pltpu.VMEM((B_WIDTH, B_QK), F32)],
        compiler_params=_cparams(("parallel", "arbitrary")),
        name="gla_bwd",
    )(z3, z3, z3, *consts)

    consts2 = [gup_bd, gbias, tri3, emask, estack, ones256, gain]
    return pl.pallas_call(
        functools.partial(_gla_main_body, nit),
        grid=(b, nblk),
        in_specs=[zspec(Z_BQ, B_QK, False), zspec(Z_BK, B_QK, False), zspec(Z_BV, B_WIDTH, False),
                  zspec(Z_BR, B_WIDTH, False), zspec(Z_BG, LANES, False),
                  pl.BlockSpec((None, nch, HEAD_DIM, B_QK), lambda bi, i: (bi, i, 0, 0))]
        + [_const_spec(a.shape) for a in consts2],
        out_specs=pl.BlockSpec((None, tb, B_WIDTH), lambda bi, i: (bi, i, 0)),
        out_shape=jax.ShapeDtypeStruct((b, seq, B_WIDTH), BF16),
        scratch_shapes=[pltpu.VMEM((B_WIDTH, B_QK), F32)],
        compiler_params=_cparams(("parallel", "arbitrary")),
        name="gla_main",
    )(z3, z3, z3, z3, z3, sb, *consts2)


def _ret_bwd_body(nch, k_ref, v_ref, raw_ref, bmask_ref, rb_ref, state):
    @pl.when(pl.program_id(1) == 0)
    def _():
        state[...] = jnp.zeros_like(state)

    c = RET_CHUNK
    lg1 = _log_sigmoid(raw_ref[1:2, :])
    pos = _iota((c, C_WIDTH), 0).astype(F32)
    kdec = jnp.exp(pos * lg1)
    chunk_decay = jnp.exp(float(c) * lg1)

    upd = []
    for j in range(nch):
        rows = pl.ds(c * j, c)
        kt = (k_ref[rows, :] * kdec).astype(BF16)
        upd.append(_dot_tn(kt, v_ref[rows, :].astype(BF16)))
    st = state[...]
    half = C_WIDTH // 2
    for j in reversed(range(nch)):
        rb_ref[j] = jnp.concatenate([st[:half, :half], st[half:, half:]], axis=0).astype(BF16)
        st = chunk_decay * st + upd[j] * bmask_ref[...]
    state[...] = st


def _ret_main_body(nch, q_ref, k_ref, v_ref, g_ref, rb_ref, raw_ref, raw_s_ref, bmask_ref, ones_ref,
                   gain_ref, o_ref, state):
    @pl.when(pl.program_id(1) == 0)
    def _():
        state[...] = jnp.zeros_like(state)

    c = RET_CHUNK
    lg = _log_sigmoid(raw_ref[...])
    lg0, lg1 = lg[0:1, :], lg[1:2, :]
    lgs = _log_sigmoid(raw_s_ref[...])
    pos = _iota((c, C_WIDTH), 0).astype(F32)
    qdec_f = jnp.exp((pos + 1.0) * lg0)
    qdec_b = jnp.exp((float(c) - pos) * lg1)
    kdec_f = jnp.exp((float(c) - 1.0 - pos) * lg0)
    chunk_decay = jnp.exp(float(c) * lg0)
    t_idx = _iota((c, C_HEADS * c), 0)
    s_idx = _iota((c, C_HEADS * c), 1) & (c - 1)
    rel = (t_idx - s_idx).astype(F32)
    dmat = jnp.where(rel >= 0.0, jnp.exp(jnp.maximum(rel, 0.0) * lgs[0:1, :]),
                     jnp.exp(jnp.maximum(-rel, 0.0) * lgs[1:2, :]))
    rho = _iota((C_HEADS * c, C_WIDTH), 0)
    lane = _iota((C_HEADS * c, C_WIDTH), 1)
    head_ok = (rho >> 7) == (lane >> 6)

    outs, lhs_inter, upd = [], [], []
    for j in range(nch):
        rows = pl.ds(c * j, c)
        q = q_ref[rows, :]
        k = k_ref[rows, :]
        v = v_ref[rows, :]
        k4 = jnp.where(head_ok, jnp.concatenate([k] * C_HEADS, axis=0), 0.0).astype(BF16)
        v4 = jnp.where(head_ok, jnp.concatenate([v] * C_HEADS, axis=0), 0.0).astype(BF16)
        scores = _dot_nt(q.astype(BF16), k4) * dmat
        outs.append(_dot(scores.astype(BF16), v4))
        lhs_inter.append(jnp.concatenate([q * qdec_f, q * qdec_b], axis=1).astype(BF16))
        upd.append(_dot_tn((k * kdec_f).astype(BF16), v.astype(BF16)))
    st = state[...]
    half = C_WIDTH // 2
    zero_q = jnp.zeros((half, half), BF16)
    for j in range(nch):
        rbj = rb_ref[j]
        rb_full = jnp.concatenate([jnp.concatenate([rbj[:half], zero_q], axis=1),
                                   jnp.concatenate([zero_q, rbj[half:]], axis=1)], axis=0)
        rhs = jnp.concatenate([st.astype(BF16), rb_full], axis=0)
        outs[j] = outs[j] + _dot(lhs_inter[j], rhs)
        st = chunk_decay * st + upd[j] * bmask_ref[...]
    state[...] = st
    for j in range(nch):
        rows = pl.ds(c * j, c)
        gate = g_ref[rows, :]
        out = _head_norm(outs[j], ones_ref, gain_ref[...]) * (gate * _sigmoid(gate))
        o_ref[rows, :] = out.astype(o_ref.dtype)


def _ret(z3, raw256, raw512, ones256, gain, tb):
    b, seq, _ = z3.shape
    nblk = seq // tb
    nch = tb // RET_CHUNK
    ntot = seq // RET_CHUNK

    def zspec(col, rev):
        blk = col // C_WIDTH
        if rev:
            return pl.BlockSpec((None, tb, C_WIDTH), lambda bi, i: (bi, nblk - 1 - i, blk))
        return pl.BlockSpec((None, tb, C_WIDTH), lambda bi, i: (bi, i, blk))

    rb = pl.pallas_call(
        functools.partial(_ret_bwd_body, nch),
        grid=(b, nblk),
        in_specs=[zspec(Z_CK, True), zspec(Z_CV, True), _const_spec(raw256.shape),
                  _const_spec(ones256.shape)],
        out_specs=pl.BlockSpec((None, nch, C_WIDTH, C_WIDTH // 2), lambda bi, i: (bi, nblk - 1 - i, 0, 0)),
        out_shape=jax.ShapeDtypeStruct((b, ntot, C_WIDTH, C_WIDTH // 2), BF16),
        scratch_shapes=[pltpu.VMEM((C_WIDTH, C_WIDTH), F32)],
        compiler_params=_cparams(("parallel", "arbitrary")),
        name="ret_bwd",
    )(z3, z3, raw256, ones256)

    consts = [raw256, raw512, ones256, ones256, gain]
    return pl.pallas_call(
        functools.partial(_ret_main_body, nch),
        grid=(b, nblk),
        in_specs=[zspec(Z_CQ, False), zspec(Z_CK, False), zspec(Z_CV, False), zspec(Z_CG, False),
                  pl.BlockSpec((None, nch, C_WIDTH, C_WIDTH // 2), lambda bi, i: (bi, i, 0, 0))]
        + [_const_spec(a.shape) for a in consts],
        out_specs=pl.BlockSpec((None, tb, C_WIDTH), lambda bi, i: (bi, i, 0)),
        out_shape=jax.ShapeDtypeStruct((b, seq, C_WIDTH), BF16),
        scratch_shapes=[pltpu.VMEM((C_WIDTH, C_WIDTH), F32)],
        compiler_params=_cparams(("parallel", "arbitrary")),
        name="ret_main",
    )(z3, z3, z3, z3, rb, *consts)


def _post_body(h_ref, oa_ref, ob_ref, oc_ref, p_ref, wo_ref, lm_ref, w1_ref, w2_ref, lp_ref,
               wg_ref, wp_ref, y_ref):
    def rms(x, g_ref):
        ms = jnp.mean(x * x, axis=-1, keepdims=True)
        return (x * lax.rsqrt(ms + EPS) * g_ref[...]).astype(BF16)

    h = h_ref[...]
    h = h + (_dot(oa_ref[...], wo_ref[0:A_WIDTH, :])
             + _dot(ob_ref[...], wo_ref[A_WIDTH:A_WIDTH + B_WIDTH, :])
             + _dot(oc_ref[...], wo_ref[A_WIDTH + B_WIDTH:, :]))
    m = rms(h, lm_ref)
    ff = D_FF // 4
    mlp = None
    for j in range(4):
        hid = _dot(m, w1_ref[:, ff * j:ff * (j + 1)])
        hid = jnp.square(jnp.maximum(hid, 0.0)).astype(BF16)
        part = _dot(hid, w2_ref[ff * j:ff * (j + 1), :])
        mlp = part if mlp is None else mlp + part
    h = h + mlp
    gate = _sigmoid(_dot(rms(h, lp_ref), wg_ref[...]))
    y_ref[...] = h + gate * _dot(p_ref[...].astype(BF16), wp_ref[...])


def _post(h2, oa, ob, oc, ple, layer, wo, lm, w1, w2, lp, wg, wp, tm):
    n = h2.shape[0]

    def tok(width):
        return pl.BlockSpec((tm, width), lambda i: (i, 0))

    consts = [wo, lm, w1, w2, lp, wg, wp]
    return pl.pallas_call(
        _post_body,
        grid=(n // tm,),
        in_specs=[tok(D_MODEL), tok(A_WIDTH), tok(B_WIDTH), tok(C_WIDTH),
                  pl.BlockSpec((None, tm, PLE_DIM), lambda i: (layer, i, 0))]
        + [_const_spec(a.shape) for a in consts],
        out_specs=tok(D_MODEL),
        out_shape=jax.ShapeDtypeStruct((n, D_MODEL), F32),
        compiler_params=_cparams(("parallel",)),
        name="post",
    )(h2, oa, ob, oc, ple, *consts)


def _rope_tables(seq, rot_dim, theta):
    half = rot_dim // 2
    inv_freq = 1.0 / (theta ** (jnp.arange(half, dtype=F32) * (2.0 / rot_dim)))
    ang = jnp.arange(seq, dtype=F32)[:, None] * inv_freq[None, :]
    cos, sin = jnp.cos(ang), jnp.sin(ang)
    pad = HEAD_DIM - rot_dim
    c = jnp.concatenate([cos, cos, jnp.ones((seq, pad), F32)], axis=1)
    sn = jnp.concatenate([-sin, jnp.zeros((seq, half + pad), F32)], axis=1)
    sp = jnp.concatenate([jnp.zeros((seq, half), F32), sin, jnp.zeros((seq, pad), F32)], axis=1)
    return jnp.concatenate([jnp.tile(t, (1, LANES // HEAD_DIM)) for t in (c, sn, sp)], axis=1)


def _block_ones(rows, cols, rblk, cblk):
    r = jnp.arange(rows)[:, None] // rblk
    c = jnp.arange(cols)[None, :] // cblk
    return r == c


def _layer_consts(i, ln_mix, w_in, attn_q_norm, attn_k_norm, gla_gate_up, gla_gate_bias, gla_out_norm,
                  ret_decay_raw, ret_out_norm, w_out, ln_mlp, w_mlp_in, w_mlp_out, ln_pe, w_pe_gate,
                  w_pe_proj):
    w = w_in[i]
    w_p = jnp.concatenate([w[:, :2304], w[:, 2336:N_IN], w[:, 2304:2336],
                           jnp.zeros((D_MODEL, Z_WIDTH - N_IN), F32)], axis=1).astype(BF16)
    gup = gla_gate_up[i].astype(BF16)
    gup_bd = jnp.zeros((LANES, 2 * B_QK), BF16)
    gup_bd = gup_bd.at[0:B_GATE_RANK, 0:B_QK].set(gup[0])
    gup_bd = gup_bd.at[B_GATE_RANK:2 * B_GATE_RANK, B_QK:].set(gup[1])
    return dict(
        ln_mix=ln_mix[i][None, :], w_p=w_p,
        qg=jnp.tile(attn_q_norm[i], 4)[None, :], kg=jnp.tile(attn_k_norm[i], 4)[None, :],
        gup_bd=gup_bd, gbias=gla_gate_bias[i].reshape(1, 2 * B_QK),
        gla_gain=gla_out_norm[i][None, :],
        raw256=jnp.repeat(ret_decay_raw[i], C_DIM, axis=1),
        raw512=jnp.repeat(ret_decay_raw[i], RET_CHUNK, axis=1),
        ret_gain=ret_out_norm[i][None, :],
        wo=w_out[i].astype(BF16), lm=ln_mlp[i][None, :], w1=w_mlp_in[i].astype(BF16),
        w2=w_mlp_out[i].astype(BF16), lp=ln_pe[i][None, :], wg=w_pe_gate[i].astype(BF16),
        wp=w_pe_proj[i].astype(BF16),
    )


def _shared_consts():
    tri = jnp.arange(GLA_CHUNK)[:, None] >= jnp.arange(GLA_CHUNK)[None, :]
    return dict(
        ones256=_block_ones(256, 256, HEAD_DIM, HEAD_DIM).astype(BF16),
        tri3=jnp.tile(tri, (1, 3)).astype(BF16),
        emask=_block_ones(B_WIDTH, B_QK, HEAD_DIM, B_KDIM).astype(F32),
        estack=_gla_diag_selector().astype(BF16),
    )


def _gla_diag_selector():
    r = jnp.arange(GLA_SUB * B_QK)
    c = jnp.arange(B_HEADS * GLA_CHUNK)
    same_s = (r // B_QK)[:, None] == (c % GLA_SUB)[None, :]
    same_head = ((r % B_QK) // B_KDIM)[:, None] == ((c % (B_HEADS * GLA_SUB)) // GLA_SUB)[None, :]
    return same_s & same_head


def _run_trunk(x, p, layers, shared):
    b, seq, _ = x.shape
    n = b * seq
    tm = 512
    tb_gla, tb_ret = 2048, 2048
    rope_a = _rope_tables(seq, ROPE_DIM, ROPE_THETA)
    rope_c = _rope_tables(seq, C_DIM, RET_THETA)
    h = x.reshape(n, D_MODEL)
    for i, lc in enumerate(layers):
        z = _proj_in(h, seq, lc["ln_mix"], lc["w_p"], shared["ones256"], lc["qg"], lc["kg"],
                     rope_a, rope_c, tm)
        z3 = z.reshape(b, seq, Z_WIDTH)
        oa = _attn(z3)
        ob = _gla(z3, lc["gup_bd"], lc["gbias"], shared["tri3"], shared["emask"], shared["estack"],
                  shared["ones256"], lc["gla_gain"], tb_gla)
        oc = _ret(z3, lc["raw256"], lc["raw512"], shared["ones256"], lc["ret_gain"], tb_ret)
        h = _post(h, oa.reshape(n, A_WIDTH), ob.reshape(n, B_WIDTH), oc.reshape(n, C_WIDTH),
                  p.reshape(p.shape[0], n, PLE_DIM), i, lc["wo"], lc["lm"], lc["w1"], lc["w2"], lc["lp"],
                  lc["wg"], lc["wp"], tm)
    return h.reshape(b, seq, D_MODEL)


def kernel(x_prompt, x_sample, p_prompt, p_sample, ln_mix, w_in, attn_q_norm, attn_k_norm, gla_gate_up, gla_gate_bias, gla_out_norm, ret_decay_raw, ret_out_norm, w_out, ln_mlp, w_mlp_in, w_mlp_out, ln_pe, w_pe_gate, w_pe_proj):
    depth = w_in.shape[0]
    layers = [_layer_consts(i, ln_mix, w_in, attn_q_norm, attn_k_norm, gla_gate_up, gla_gate_bias,
                            gla_out_norm, ret_decay_raw, ret_out_norm, w_out, ln_mlp, w_mlp_in,
                            w_mlp_out, ln_pe, w_pe_gate, w_pe_proj) for i in range(depth)]
    shared = _shared_consts()
    y_prompt = _run_trunk(x_prompt, p_prompt, layers, shared)
    y_sample = _run_trunk(x_sample, p_sample, layers, shared)
    return (y_prompt, y_sample)
```
